```python
import math
import jax, jax.numpy as jnp
from jax import lax
import numpy as np

D_MODEL = 1024
BATCH = 8
SEQ = 2048
DEPTH = 2

GRID_W = 64
N_GROUPS = 4
W_NA = D_MODEL // N_GROUPS
W_SC = D_MODEL // N_GROUPS
W_CF = D_MODEL // N_GROUPS
W_SSM = D_MODEL // N_GROUPS
D_MIX = W_NA + W_SC + W_CF + W_SSM
IN_COLS = 3 * W_NA + 3 * W_SC + 2 * W_CF + W_SSM
NA_HEAD_DIM = 64
NA_HEADS = W_NA // NA_HEAD_DIM
NA_WIN_H = 8
NA_WIN_W = 16
NA_QBLK_W = 16
NA_KBLK_W = NA_QBLK_W + NA_WIN_W
SC_WIDTH = 3
CF_WIDTH = 31
SSM_GROUP_CH = 16
SSM_GROUPS = W_SSM // SSM_GROUP_CH
SSM_STATE = 64
D_FF_DENSE = ((8 * D_MODEL // 3 + 127) // 128) * 128
N_EXPERTS = 8
TOP_K = 2
D_FF_EXPERT = 7 * D_MODEL // 2
N_DENSE = (DEPTH + 1) // 2
N_MOE = DEPTH // 2
EPS = 1e-6
NEG_INF = -1e30

kernel_name = "hybrid_parallel_group_encoder"


def rms_norm(x, g):
    xf = x.astype(jnp.float32)
    y = xf * lax.rsqrt(jnp.mean(xf * xf, axis=-1, keepdims=True) + EPS)
    return (y * g.astype(jnp.float32)).astype(x.dtype)


def layer_norm(x, g, b):
    xf = x.astype(jnp.float32)
    mu = jnp.mean(xf, axis=-1, keepdims=True)
    xc = xf - mu
    var = jnp.mean(xc * xc, axis=-1, keepdims=True)
    return (xc * lax.rsqrt(var + EPS) * g.astype(jnp.float32) + b.astype(jnp.float32)).astype(x.dtype)


def depthwise_conv(x, w):
    k, c = w.shape
    return lax.conv_general_dilated(
        x, w[:, None, :].astype(x.dtype), window_strides=(1,), padding=[(k // 2, k // 2)],
        dimension_numbers=("NWC", "WIO", "NWC"), feature_group_count=c)


def neighbourhood_attention(q, k, v, rpb):
    bsz, seq_len, n_heads, head_dim = q.shape
    rows = seq_len // GRID_W
    kh = min(NA_WIN_H, rows)
    ncb = GRID_W // NA_QBLK_W
    r = np.arange(rows)
    row_start = np.clip(r - kh // 2, 0, rows - kh)
    key_rows = row_start[:, None] + np.arange(kh)
    j = np.arange(ncb)
    q_cols = j[:, None] * NA_QBLK_W + np.arange(NA_QBLK_W)
    q_col_start = np.clip(q_cols - NA_WIN_W // 2, 0, GRID_W - NA_WIN_W)
    kblk_start = np.clip(j * NA_QBLK_W - NA_WIN_W // 2, 0, GRID_W - NA_KBLK_W)
    key_cols = kblk_start[:, None] + np.arange(NA_KBLK_W)
    key_idx = key_rows[:, None, :, None] * GRID_W + key_cols[None, :, None, :]
    flat = key_idx.reshape(-1)
    kg = jnp.take(k, flat, axis=1).reshape(bsz, rows, ncb, kh, NA_KBLK_W, n_heads, head_dim)
    vg = jnp.take(v, flat, axis=1).reshape(bsz, rows, ncb, kh, NA_KBLK_W, n_heads, head_dim)
    qb = q.reshape(bsz, rows, ncb, NA_QBLK_W, n_heads, head_dim)
    s = jnp.einsum("brjqhd,brjkwhd->bhrjqkw", qb, kg).astype(jnp.float32) * (head_dim ** -0.5)
    dr_idx = key_rows - r[:, None] + NA_WIN_H - 1
    dc = key_cols[:, None, :] - q_cols[:, :, None]
    dc_idx = np.clip(dc + NA_WIN_W - 1, 0, 2 * NA_WIN_W - 2)
    bias = rpb.astype(jnp.float32)[:, dr_idx[:, None, None, :, None], dc_idx[None, :, :, None, :]]
    in_win = (key_cols[:, None, :] >= q_col_start[:, :, None]) & (key_cols[:, None, :] < q_col_start[:, :, None] + NA_WIN_W)
    s = jnp.where(in_win[None, None, None, :, :, None, :], s + bias[None], NEG_INF)
    p = jax.nn.softmax(s.reshape(bsz, n_heads, rows, ncb, NA_QBLK_W, kh * NA_KBLK_W), axis=-1)
    p = p.reshape(bsz, n_heads, rows, ncb, NA_QBLK_W, kh, NA_KBLK_W).astype(v.dtype)
    o = jnp.einsum("bhrjqkw,brjkwhd->brjqhd", p, vg)
    return o.reshape(bsz, seq_len, n_heads * head_dim)


def _linear_recurrence(e1, e2):
    a1, b1 = e1
    a2, b2 = e2
    return a2 * a1, a2 * b1 + b2


def s5_mixer(u, a_re, a_im, log_dt, b_re, b_im, c_re, c_im, d_skip, w_glu, b_glu):
    bsz, seq_len, _ = u.shape
    f32 = jnp.float32
    uf = u.astype(f32).reshape(bsz, seq_len, SSM_GROUPS, SSM_GROUP_CH)
    uc = uf.astype(jnp.complex64)
    a = lax.complex(a_re.astype(f32), a_im.astype(f32))
    dt = jnp.exp(log_dt.astype(f32))[..., None]
    a_bar = jnp.exp(a * dt)
    b_bar = ((a_bar - 1.0) / a)[..., None] * lax.complex(b_re.astype(f32), b_im.astype(f32))
    c = lax.complex(c_re.astype(f32), c_im.astype(f32))
    y = d_skip.astype(f32).reshape(SSM_GROUPS, SSM_GROUP_CH) * uf
    for direction in range(2):
        bu = jnp.einsum("gps,blgs->blgp", b_bar[direction], uc)
        decay = jnp.broadcast_to(a_bar[direction], bu.shape)
        _, state = lax.associative_scan(_linear_recurrence, (decay, bu), axis=1, reverse=(direction == 1))
        y = y + jnp.einsum("gsp,blgp->blgs", c[direction], state).real
    y = jax.nn.gelu(y.reshape(bsz, seq_len, W_SSM)).astype(u.dtype)
    return y * jax.nn.sigmoid(y @ w_glu + b_glu)


def token_mixer(h, w_in, na_rpb, sc_conv_w, cf_conv_w, cf_conv_b, cf_ln_g, cf_ln_b,
                ssm_a_re, ssm_a_im, ssm_log_dt, ssm_b_re, ssm_b_im, ssm_c_re, ssm_c_im,
                ssm_d, ssm_w_glu, ssm_b_glu, grp_norm_g, w_out):
    bsz, seq_len, _ = h.shape
    z = h @ w_in
    cuts = [3 * W_NA, 3 * W_NA + 3 * W_SC, 3 * W_NA + 3 * W_SC + 2 * W_CF]
    z_na, z_sc, z_cf, z_ssm = jnp.split(z, cuts, axis=-1)
    qkv = z_na.reshape(bsz, seq_len, 3, NA_HEADS, NA_HEAD_DIM)
    y_na = neighbourhood_attention(qkv[:, :, 0], qkv[:, :, 1], qkv[:, :, 2], na_rpb)
    sc_b, sc_c, sc_x = jnp.split(z_sc, 3, axis=-1)
    y_sc = sc_b * depthwise_conv(sc_c * sc_x, sc_conv_w)
    cf_a, cf_g = jnp.split(z_cf, 2, axis=-1)
    cf = depthwise_conv(cf_a * jax.nn.sigmoid(cf_g), cf_conv_w) + cf_conv_b
    y_cf = jax.nn.silu(layer_norm(cf, cf_ln_g, cf_ln_b))
    y_ssm = s5_mixer(z_ssm, ssm_a_re, ssm_a_im, ssm_log_dt, ssm_b_re, ssm_b_im,
                     ssm_c_re, ssm_c_im, ssm_d, ssm_w_glu, ssm_b_glu)
    y = jnp.concatenate([y_na, y_sc, y_cf, y_ssm], axis=-1).reshape(bsz, seq_len, N_GROUPS, D_MIX // N_GROUPS)
    y = rms_norm(y, grp_norm_g.reshape(N_GROUPS, D_MIX // N_GROUPS)).reshape(bsz, seq_len, D_MIX)
    return y @ w_out


def swiglu(x, w_gate, w_up, w_down):
    return (jax.nn.silu(x @ w_gate) * (x @ w_up)) @ w_down


def moe_swiglu(x, w_router, w_gate, w_up, w_down):
    bsz, seq_len, d = x.shape
    xt = x.reshape(-1, d)
    logits = (xt @ w_router).astype(jnp.float32)
    top_v, top_i = lax.top_k(logits, TOP_K)
    gates = jax.nn.softmax(top_v, axis=-1)
    combine = jnp.sum(jax.nn.one_hot(top_i, N_EXPERTS, dtype=jnp.float32) * gates[..., None], axis=1)
    combine = combine.astype(x.dtype)
    y = jnp.zeros_like(xt)
    for e in range(N_EXPERTS):
        y = y + combine[:, e:e + 1] * swiglu(xt, w_gate[e], w_up[e], w_down[e])
    return y.reshape(bsz, seq_len, d)


def setup_inputs(seed: int = 0) -> dict:
    key = jax.random.key(seed)
    ks = iter(jax.random.split(key, 48))
    f32 = jnp.float32

    def nrm(shape, scale):
        return jax.random.normal(next(ks), shape, f32) * scale

    n_idx = jnp.arange(SSM_STATE, dtype=f32)
    inputs = {
        "x": nrm((BATCH, SEQ, D_MODEL), 1.0),
        "norm1_g": 1.0 + nrm((DEPTH, D_MODEL), 0.05),
        "w_in": nrm((DEPTH, D_MODEL, IN_COLS), D_MODEL ** -0.5),
        "na_rpb": nrm((DEPTH, NA_HEADS, 2 * NA_WIN_H - 1, 2 * NA_WIN_W - 1), 0.1),
        "sc_conv_w": nrm((DEPTH, SC_WIDTH, W_SC), SC_WIDTH ** -0.5),
        "cf_conv_w": nrm((DEPTH, CF_WIDTH, W_CF), CF_WIDTH ** -0.5),
        "cf_conv_b": nrm((DEPTH, W_CF), 0.01),
        "cf_ln_g": 1.0 + nrm((DEPTH, W_CF), 0.05),
        "cf_ln_b": nrm((DEPTH, W_CF), 0.01),
        "ssm_a_re": -0.5 + nrm((DEPTH, 2, SSM_GROUPS, SSM_STATE), 0.01),
        "ssm_a_im": math.pi * n_idx + nrm((DEPTH, 2, SSM_GROUPS, SSM_STATE), 0.01),
        "ssm_log_dt": jax.random.uniform(next(ks), (DEPTH, 2, SSM_GROUPS), f32, math.log(1e-3), math.log(1e-1)),
        "ssm_b_re": nrm((DEPTH, 2, SSM_GROUPS, SSM_STATE, SSM_GROUP_CH), (2 * SSM_GROUP_CH) ** -0.5),
        "ssm_b_im": nrm((DEPTH, 2, SSM_GROUPS, SSM_STATE, SSM_GROUP_CH), (2 * SSM_GROUP_CH) ** -0.5),
        "ssm_c_re": nrm((DEPTH, 2, SSM_GROUPS, SSM_GROUP_CH, SSM_STATE), SSM_STATE ** -0.5),
        "ssm_c_im": nrm((DEPTH, 2, SSM_GROUPS, SSM_GROUP_CH, SSM_STATE), SSM_STATE ** -0.5),
        "ssm_d": nrm((DEPTH, W_SSM), 0.5),
        "ssm_w_glu": nrm((DEPTH, W_SSM, W_SSM), W_SSM ** -0.5),
        "ssm_b_glu": nrm((DEPTH, W_SSM), 0.01),
        "grp_norm_g": 1.0 + nrm((DEPTH, D_MIX), 0.05),
        "w_out": nrm((DEPTH, D_MIX, D_MODEL), D_MIX ** -0.5),
        "norm2_g": 1.0 + nrm((DEPTH, D_MODEL), 0.05),
        "ffn_w_gate": nrm((N_DENSE, D_MODEL, D_FF_DENSE), D_MODEL ** -0.5),
        "ffn_w_up": nrm((N_DENSE, D_MODEL, D_FF_DENSE), D_MODEL ** -0.5),
        "ffn_w_down": nrm((N_DENSE, D_FF_DENSE, D_MODEL), D_FF_DENSE ** -0.5),
        "moe_w_router": nrm((N_MOE, D_MODEL, N_EXPERTS), D_MODEL ** -0.5),
        "moe_w_gate": nrm((N_MOE, N_EXPERTS, D_MODEL, D_FF_EXPERT), D_MODEL ** -0.5),
        "moe_w_up": nrm((N_MOE, N_EXPERTS, D_MODEL, D_FF_EXPERT), D_MODEL ** -0.5),
        "moe_w_down": nrm((N_MOE, N_EXPERTS, D_FF_EXPERT, D_MODEL), D_FF_EXPERT ** -0.5),
        "final_norm_g": 1.0 + nrm((D_MODEL,), 0.05),
    }
    return inputs


def reference(x, norm1_g, w_in, na_rpb, sc_conv_w, cf_conv_w, cf_conv_b, cf_ln_g, cf_ln_b,
              ssm_a_re, ssm_a_im, ssm_log_dt, ssm_b_re, ssm_b_im, ssm_c_re, ssm_c_im,
              ssm_d, ssm_w_glu, ssm_b_glu, grp_norm_g, w_out, norm2_g,
              ffn_w_gate, ffn_w_up, ffn_w_down, moe_w_router, moe_w_gate, moe_w_up, moe_w_down,
              final_norm_g):
    for i in range(DEPTH):
        h = rms_norm(x, norm1_g[i])
        x = x + token_mixer(h, w_in[i], na_rpb[i], sc_conv_w[i], cf_conv_w[i], cf_conv_b[i],
                            cf_ln_g[i], cf_ln_b[i], ssm_a_re[i], ssm_a_im[i], ssm_log_dt[i],
                            ssm_b_re[i], ssm_b_im[i], ssm_c_re[i], ssm_c_im[i], ssm_d[i],
                            ssm_w_glu[i], ssm_b_glu[i], grp_norm_g[i], w_out[i])
        h = rms_norm(x, norm2_g[i])
        if i % 2 == 0:
            j = i // 2
            x = x + swiglu(h, ffn_w_gate[j], ffn_w_up[j], ffn_w_down[j])
        else:
            j = i // 2
            x = x + moe_swiglu(h, moe_w_router[j], moe_w_gate[j], moe_w_up[j], moe_w_down[j])
    return rms_norm(x, final_norm_g)
```

```python
import functools
import math

import jax
import jax.numpy as jnp
import numpy as np
from jax import lax
from jax.experimental import pallas as pl
from jax.experimental.pallas import tpu as pltpu

F32 = jnp.float32
BF16 = jnp.bfloat16
EPS = 1e-6
NEG_INF = -1e30

GRID_W = 64
NA_HEAD_DIM = 64
NA_WIN_H = 8
NA_WIN_W = 16
GROUP_W = 256
SSM_GROUP_CH = 16
SSM_STATE = 64
N_EXPERTS = 8
VMEM_LIMIT = 56 * 1024 * 1024


def _cparams(sem, vmem=VMEM_LIMIT):
    return pltpu.CompilerParams(dimension_semantics=sem, vmem_limit_bytes=vmem)


def _rms(x, g):
    return x * lax.rsqrt(jnp.mean(x * x, axis=-1, keepdims=True) + EPS) * g


def _in_proj_kernel(x_ref, g_ref, w_ref, qkv_ref, rest_ref, ssm_ref):
    h = _rms(x_ref[...], g_ref[...]).astype(BF16)
    n_qkv = qkv_ref.shape[1] // GROUP_W
    n_rest = rest_ref.shape[1] // GROUP_W
    for j in range(n_qkv + n_rest + 1):
        z = jnp.dot(h, w_ref[:, j * GROUP_W:(j + 1) * GROUP_W], preferred_element_type=F32)
        if j < n_qkv:
            qkv_ref[:, j * GROUP_W:(j + 1) * GROUP_W] = z.astype(BF16)
        elif j < n_qkv + n_rest:
            k = j - n_qkv
            rest_ref[:, k * GROUP_W:(k + 1) * GROUP_W] = z
        else:
            ssm_ref[...] = z


def _in_proj(x2, g, w, bsz, seq, tm=512):
    t, d = x2.shape
    n_cols = w.shape[1]
    lt = seq // tm
    return pl.pallas_call(
        _in_proj_kernel,
        grid=(t // tm,),
        in_specs=[
            pl.BlockSpec((tm, d), lambda i: (i, 0)),
            pl.BlockSpec((1, d), lambda i: (0, 0)),
            pl.BlockSpec((d, n_cols), lambda i: (0, 0)),
        ],
        out_specs=[
            pl.BlockSpec((tm, 3 * GROUP_W), lambda i: (i, 0)),
            pl.BlockSpec((tm, 5 * GROUP_W), lambda i: (i, 0)),
            pl.BlockSpec((tm, GROUP_W), lambda i: (i % lt, i // lt)),
        ],
        out_shape=[
            jax.ShapeDtypeStruct((t, 3 * GROUP_W), BF16),
            jax.ShapeDtypeStruct((t, 5 * GROUP_W), F32),
            jax.ShapeDtypeStruct((seq, bsz * GROUP_W), F32),
        ],
        compiler_params=_cparams(("arbitrary",)),
        name="in_proj",
    )(x2, g.reshape(1, d), w)


def _na_bias_table(rpb, rows):
    n_heads = rpb.shape[0]
    kh = NA_WIN_H
    c = np.arange(GRID_W)
    qcs = np.clip(c - NA_WIN_W // 2, 0, GRID_W - NA_WIN_W)
    kc = np.arange(GRID_W)
    in_win = (kc[None, :] >= qcs[:, None]) & (kc[None, :] < qcs[:, None] + NA_WIN_W)
    dc_idx = np.clip(kc[None, :] - c[:, None] + NA_WIN_W - 1, 0, 2 * NA_WIN_W - 2)
    rep_rows = [0, 1, 2, 3, 4, rows - 3, rows - 2, rows - 1]
    tabs = []
    for r in rep_rows:
        rs = int(np.clip(r - kh // 2, 0, rows - kh))
        dr_idx = rs + np.arange(kh) - r + NA_WIN_H - 1
        b = rpb.astype(F32)[:, dr_idx[:, None, None], dc_idx[None, :, :]]
        b = jnp.where(jnp.asarray(in_win)[None, None], b, NEG_INF)
        b = jnp.transpose(b, (0, 2, 1, 3)).reshape(n_heads * GRID_W, kh * GRID_W)
        tabs.append(b)
    return jnp.stack(tabs)


def _na_kernel(q_ref, k_ref, v_ref, bias_ref, o_ref, *, rows, n_heads):
    kh = NA_WIN_H
    lane = lax.broadcasted_iota(jnp.int32, (GRID_W, n_heads * NA_HEAD_DIM), 1)
    masks = [(lane >= h * NA_HEAD_DIM) & (lane < (h + 1) * NA_HEAD_DIM) for h in range(n_heads)]
    scale = NA_HEAD_DIM ** -0.5

    def body(r, carry):
        q0 = pl.multiple_of(r * GRID_W, GRID_W)
        q = q_ref[pl.ds(q0, GRID_W), :].astype(F32) * scale
        qs = jnp.concatenate([jnp.where(m, q, 0.0) for m in masks], axis=0).astype(BF16)
        rs = jnp.clip(r - kh // 2, 0, rows - kh)
        k0 = pl.multiple_of(rs * GRID_W, GRID_W)
        ks = k_ref[pl.ds(k0, kh * GRID_W), :]
        vs = v_ref[pl.ds(k0, kh * GRID_W), :]
        s = lax.dot_general(qs, ks, (((1,), (1,)), ((), ())), preferred_element_type=F32)
        var = jnp.where(r < kh // 2, r, jnp.where(r > rows - kh // 2, r - (rows - 8), kh // 2))
        s = s + bias_ref[var]
        m = jnp.max(s, axis=-1, keepdims=True)
        p = jnp.exp(s - m)
        den = jnp.sum(p, axis=-1, keepdims=True)
        o = jnp.dot(p.astype(BF16), vs, preferred_element_type=F32) / den
        out = jnp.where(masks[0], o[0:GRID_W], 0.0)
        for h in range(1, n_heads):
            out = out + jnp.where(masks[h], o[h * GRID_W:(h + 1) * GRID_W], 0.0)
        o_ref[pl.ds(q0, GRID_W), :] = out
        return carry

    lax.fori_loop(0, rows, body, 0)


def _na_attention(qkv, bias, bsz, seq):
    rows = seq // GRID_W
    n_heads = GROUP_W // NA_HEAD_DIM
    t = qkv.shape[0]
    blk = lambda j: pl.BlockSpec((seq, GROUP_W), lambda b, j=j: (b, j))
    return pl.pallas_call(
        functools.partial(_na_kernel, rows=rows, n_heads=n_heads),
        grid=(bsz,),
        in_specs=[blk(0), blk(1), blk(2),
                  pl.BlockSpec(bias.shape, lambda b: (0, 0, 0))],
        out_specs=pl.BlockSpec((seq, GROUP_W), lambda b: (b, 0)),
        out_shape=jax.ShapeDtypeStruct((t, GROUP_W), F32),
        compiler_params=_cparams(("arbitrary",)),
        name="na_attention",
    )(qkv, qkv, qkv, bias)


CONV_PAD = 16
CONV_CHUNK = 128


def _conv_kernel(b_ref, c_ref, x_ref, a_ref, g_ref, scw_ref, cfw_ref, cfb_ref, lng_ref, lnb_ref,
                 ysc_ref, ycf_ref, psc, pcf):
    seq = b_ref.shape[0]
    n_chunks = seq // CONV_CHUNK
    zeros = jnp.zeros((CONV_PAD, GROUP_W), F32)
    psc[0:CONV_PAD, :] = zeros
    pcf[0:CONV_PAD, :] = zeros
    psc[CONV_PAD + seq:2 * CONV_PAD + seq, :] = zeros
    pcf[CONV_PAD + seq:2 * CONV_PAD + seq, :] = zeros

    def fill(i, carry):
        r0 = pl.multiple_of(i * CONV_CHUNK, CONV_CHUNK)
        rows = pl.ds(r0, CONV_CHUNK)
        dst = pl.ds(r0 + CONV_PAD, CONV_CHUNK)
        psc[dst, :] = c_ref[rows, :] * x_ref[rows, :]
        pcf[dst, :] = a_ref[rows, :] * jax.nn.sigmoid(g_ref[rows, :])
        return carry

    lax.fori_loop(0, n_chunks, fill, 0)

    k_sc = scw_ref.shape[0]
    k_cf = cfw_ref.shape[0]

    def conv(i, carry):
        r0 = pl.multiple_of(i * CONV_CHUNK, CONV_CHUNK)
        rows = pl.ds(r0, CONV_CHUNK)
        win = psc[pl.ds(r0, CONV_CHUNK + 2 * CONV_PAD), :]
        acc = None
        for k in range(k_sc):
            o = CONV_PAD + k - k_sc // 2
            term = win[o:o + CONV_CHUNK, :] * scw_ref[k:k + 1, :]
            acc = term if acc is None else acc + term
        ysc_ref[rows, :] = b_ref[rows, :] * acc

        win = pcf[pl.ds(r0, CONV_CHUNK + 2 * CONV_PAD), :]
        acc = None
        for k in range(k_cf):
            o = CONV_PAD + k - k_cf // 2
            term = win[o:o + CONV_CHUNK, :] * cfw_ref[k:k + 1, :]
            acc = term if acc is None else acc + term
        cf = acc + cfb_ref[...]
        mu = jnp.mean(cf, axis=-1, keepdims=True)
        xc = cf - mu
        var = jnp.mean(xc * xc, axis=-1, keepdims=True)
        ln = xc * lax.rsqrt(var + EPS) * lng_ref[...] + lnb_ref[...]
        ycf_ref[rows, :] = ln * jax.nn.sigmoid(ln)
        return carry

    lax.fori_loop(0, n_chunks, conv, 0)


def _conv_mixers(rest, sc_w, cf_w, cf_b, ln_g, ln_b, bsz, seq):
    t = rest.shape[0]
    blk = lambda j: pl.BlockSpec((seq, GROUP_W), lambda b, j=j: (b, j))
    full = lambda a: pl.BlockSpec(a.shape, lambda b: (0,) * a.ndim)
    cf_b, ln_g, ln_b = (a.reshape(1, GROUP_W) for a in (cf_b, ln_g, ln_b))
    out_spec = pl.BlockSpec((seq, GROUP_W), lambda b: (b, 0))
    return pl.pallas_call(
        _conv_kernel,
        grid=(bsz,),
        in_specs=[blk(0), blk(1), blk(2), blk(3), blk(4),
                  full(sc_w), full(cf_w), full(cf_b), full(ln_g), full(ln_b)],
        out_specs=[out_spec, out_spec],
        out_shape=[jax.ShapeDtypeStruct((t, GROUP_W), F32)] * 2,
        scratch_shapes=[pltpu.VMEM((seq + 2 * CONV_PAD, GROUP_W), F32)] * 2,
        compiler_params=_cparams(("arbitrary",)),
        name="conv_mixers",
    )(rest, rest, rest, rest, rest, sc_w, cf_w, cf_b, ln_g, ln_b)


S5_TL = 64
S5_LANE_SPLIT = 2


def _s5_discretise(a_re, a_im, log_dt, b_re, b_im, c_re, c_im, bsz):
    f32 = F32
    a_re, a_im, log_dt = a_re.astype(f32), a_im.astype(f32), log_dt.astype(f32)
    n_dir, n_grp, n_state = a_re.shape
    n_ch = b_re.shape[-1]
    dt = jnp.exp(log_dt)[..., None]
    mag = jnp.exp(a_re * dt)
    abr = mag * jnp.cos(a_im * dt)
    abi = mag * jnp.sin(a_im * dt)
    den = a_re * a_re + a_im * a_im
    fr = ((abr - 1.0) * a_re + abi * a_im) / den
    fi = (abi * a_re - (abr - 1.0) * a_im) / den
    bbr = fr[..., None] * b_re - fi[..., None] * b_im
    bbi = fr[..., None] * b_im + fi[..., None] * b_re
    eye = jnp.eye(n_grp, dtype=f32)
    blk_b = lambda m: jnp.einsum("dgps,gh->dgshp", m, eye).reshape(n_dir, n_grp * n_ch, n_grp * n_state)
    blk_c = lambda m: jnp.einsum("dgsp,gh->dgphs", m, eye).reshape(n_dir, n_grp * n_state, n_grp * n_ch)
    bmat = jnp.concatenate([blk_b(bbr), blk_b(bbi)], axis=-1)
    cmat = jnp.concatenate([blk_c(c_re.astype(f32)), blk_c(-c_im.astype(f32))], axis=1)
    amat = jnp.stack([abr.reshape(n_dir, -1), abi.reshape(n_dir, -1)], axis=1)
    amat = jnp.broadcast_to(amat[:, :, None, :], (n_dir, 2, bsz, n_grp * n_state))
    return bmat.astype(BF16), cmat.astype(BF16), amat


def _s5_scan_kernel(u_ref, b_ref, c_ref, a_ref, y_ref, xs, st, *, bsz):
    d = pl.program_id(0)
    i = pl.program_id(1)
    n_state = a_ref.shape[-1]

    @pl.when(i == 0)
    def _():
        st[...] = jnp.zeros_like(st)

    xs[...] = jnp.dot(u_ref[...].astype(BF16), b_ref[...], preferred_element_type=F32)

    cw = n_state // S5_LANE_SPLIT
    for c in range(S5_LANE_SPLIT):
        re = slice(c * cw, (c + 1) * cw)
        im = slice(n_state + c * cw, n_state + (c + 1) * cw)
        a_r = a_ref[0, :, re]
        a_i = a_ref[1, :, re]

        def step(t, carry, re=re, im=im, a_r=a_r, a_i=a_i):
            s_r, s_i = carry
            tt = t + d * (S5_TL - 1 - 2 * t)
            rows = pl.ds(pl.multiple_of(tt * bsz, bsz), bsz)
            n_r = a_r * s_r - a_i * s_i + xs[rows, re]
            n_i = a_r * s_i + a_i * s_r + xs[rows, im]
            xs[rows, re] = n_r
            xs[rows, im] = n_i
            return n_r, n_i

        s_r, s_i = lax.fori_loop(0, S5_TL, step, (st[:, re], st[:, im]), unroll=4)
        st[:, re] = s_r
        st[:, im] = s_i

    y_ref[...] = jnp.dot(xs[...].astype(BF16), c_ref[...], preferred_element_type=F32)


def _s5_scan(u_t, bmat, cmat, amat, bsz, seq):
    n_chunks = seq // S5_TL
    rows = S5_TL * bsz
    n_in = u_t.shape[1]
    n_st2 = bmat.shape[-1]
    chunk = lambda d, i: i + d * (n_chunks - 1 - 2 * i)
    return pl.pallas_call(
        functools.partial(_s5_scan_kernel, bsz=bsz),
        grid=(2, n_chunks),
        in_specs=[
            pl.BlockSpec((rows, n_in), lambda d, i: (chunk(d, i), 0)),
            pl.BlockSpec((None, n_in, n_st2), lambda d, i: (d, 0, 0)),
            pl.BlockSpec((None, n_st2, n_in), lambda d, i: (d, 0, 0)),
            pl.BlockSpec((None, 2, bsz, n_st2 // 2), lambda d, i: (d, 0, 0, 0)),
        ],
        out_specs=pl.BlockSpec((None, rows, n_in), lambda d, i: (d, chunk(d, i), 0)),
        out_shape=jax.ShapeDtypeStruct((2, seq * bsz, n_in), F32),
        scratch_shapes=[pltpu.VMEM((rows, n_st2), F32), pltpu.VMEM((bsz, n_st2), F32)],
        compiler_params=_cparams(("arbitrary", "arbitrary")),
        name="s5_scan",
    )(u_t, bmat, cmat, amat)


def _s5_post_kernel(u_ref, yf_ref, yb_ref, d_ref, w_ref, b_ref, o_ref):
    y = d_ref[...] * u_ref[...] + yf_ref[...] + yb_ref[...]
    y = jax.nn.gelu(y)
    gate = jnp.dot(y.astype(BF16), w_ref[...], preferred_element_type=F32) + b_ref[...]
    o_ref[...] = y * jax.nn.sigmoid(gate)


def _s5_post(u_lb, y_dirs, d_skip, w_glu, b_glu, bsz, seq, tl=512):
    lt = seq // tl
    return pl.pallas_call(
        _s5_post_kernel,
        grid=(bsz, lt),
        in_specs=[
            pl.BlockSpec((tl, GROUP_W), lambda b, l: (l, b)),
            pl.BlockSpec((None, tl, GROUP_W), lambda b, l: (0, l, b)),
            pl.BlockSpec((None, tl, GROUP_W), lambda b, l: (1, l, b)),
            pl.BlockSpec((1, GROUP_W), lambda b, l: (0, 0)),
            pl.BlockSpec((GROUP_W, GROUP_W), lambda b, l: (0, 0)),
            pl.BlockSpec((1, GROUP_W), lambda b, l: (0, 0)),
        ],
        out_specs=pl.BlockSpec((tl, GROUP_W), lambda b, l: (b * lt + l, 0)),
        out_shape=jax.ShapeDtypeStruct((bsz * seq, GROUP_W), F32),
        compiler_params=_cparams(("arbitrary", "arbitrary")),
        name="s5_post",
    )(u_lb, y_dirs, y_dirs, d_skip.reshape(1, GROUP_W), w_glu, b_glu.reshape(1, GROUP_W))


def _out_proj_kernel(y0_ref, y1_ref, y2_ref, y3_ref, gn_ref, w_ref, x_ref, o_ref):
    acc = x_ref[...]
    for j, y_ref in enumerate((y0_ref, y1_ref, y2_ref, y3_ref)):
        yn = _rms(y_ref[...], gn_ref[j:j + 1, :]).astype(BF16)
        acc = acc + jnp.dot(yn, w_ref[j * GROUP_W:(j + 1) * GROUP_W, :], preferred_element_type=F32)
    o_ref[...] = acc


def _out_proj(ys, gn, w, x2, tm=512):
    t, d = x2.shape
    yspec = pl.BlockSpec((tm, GROUP_W), lambda i: (i, 0))
    return pl.pallas_call(
        _out_proj_kernel,
        grid=(t // tm,),
        in_specs=[yspec] * 4 + [
            pl.BlockSpec((4, GROUP_W), lambda i: (0, 0)),
            pl.BlockSpec(w.shape, lambda i: (0, 0)),
            pl.BlockSpec((tm, d), lambda i: (i, 0)),
        ],
        out_specs=pl.BlockSpec((tm, d), lambda i: (i, 0)),
        out_shape=jax.ShapeDtypeStruct((t, d), F32),
        compiler_params=_cparams(("arbitrary",)),
        name="out_proj",
    )(*ys, gn.reshape(4, GROUP_W), w, x2)


def _ffn_kernel(x_ref, g_ref, wg_ref, wu_ref, wd_ref, o_ref, h_ref, acc_ref):
    f = pl.program_id(1)

    @pl.when(f == 0)
    def _():
        h_ref[...] = _rms(x_ref[...], g_ref[...]).astype(BF16)
        acc_ref[...] = x_ref[...]

    h = h_ref[...]
    gate = jnp.dot(h, wg_ref[...], preferred_element_type=F32)
    up = jnp.dot(h, wu_ref[...], preferred_element_type=F32)
    act = (gate * jax.nn.sigmoid(gate) * up).astype(BF16)
    acc_ref[...] += jnp.dot(act, wd_ref[...], preferred_element_type=F32)

    @pl.when(f == pl.num_programs(1) - 1)
    def _():
        o_ref[...] = acc_ref[...]


def _dense_ffn(x2, g, wg, wu, wd, tm=512, tf=1408):
    t, d = x2.shape
    dff = wg.shape[1]
    return pl.pallas_call(
        _ffn_kernel,
        grid=(t // tm, dff // tf),
        in_specs=[
            pl.BlockSpec((tm, d), lambda i, f: (i, 0)),
            pl.BlockSpec((1, d), lambda i, f: (0, 0)),
            pl.BlockSpec((d, tf), lambda i, f: (0, f)),
            pl.BlockSpec((d, tf), lambda i, f: (0, f)),
            pl.BlockSpec((tf, d), lambda i, f: (f, 0)),
        ],
        out_specs=pl.BlockSpec((tm, d), lambda i, f: (i, 0)),
        out_shape=jax.ShapeDtypeStruct((t, d), F32),
        scratch_shapes=[pltpu.VMEM((tm, d), BF16), pltpu.VMEM((tm, d), F32)],
        compiler_params=_cparams(("arbitrary", "arbitrary")),
        name="dense_ffn",
    )(x2, g.reshape(1, d), wg, wu, wd)


ROUTE_COLS = 8
ROUTER_LANES = 128


def _router_kernel(x_ref, g_ref, wr_ref, h_ref, route_ref):
    h = _rms(x_ref[...], g_ref[...])
    h_ref[...] = h
    logits = jnp.dot(h, wr_ref[...], preferred_element_type=F32, precision=lax.Precision.HIGHEST)
    lane = lax.broadcasted_iota(jnp.int32, logits.shape, 1)
    minus_inf = -jnp.inf
    l1 = jnp.where(lane < N_EXPERTS, logits, minus_inf)
    m1 = jnp.max(l1, axis=-1, keepdims=True)
    i1 = jnp.min(jnp.where(l1 == m1, lane, ROUTER_LANES), axis=-1, keepdims=True)
    l2 = jnp.where(lane == i1, minus_inf, l1)
    m2 = jnp.max(l2, axis=-1, keepdims=True)
    i2 = jnp.min(jnp.where(l2 == m2, lane, ROUTER_LANES), axis=-1, keepdims=True)
    e2 = jnp.exp(m2 - m1)
    den = 1.0 + e2
    rec = jnp.where(lane == 0, i1.astype(F32),
                    jnp.where(lane == 1, i2.astype(F32),
                              jnp.where(lane == 2, 1.0 / den,
                                        jnp.where(lane == 3, e2 / den, 0.0))))
    route_ref[...] = rec[:, :ROUTE_COLS]


def _router(x2, g, w_router, tm=512):
    t, d = x2.shape
    wr = jnp.zeros((d, ROUTER_LANES), F32).at[:, :N_EXPERTS].set(w_router.astype(F32))
    return pl.pallas_call(
        _router_kernel,
        grid=(t // tm,),
        in_specs=[
            pl.BlockSpec((tm, d), lambda i: (i, 0)),
            pl.BlockSpec((1, d), lambda i: (0, 0)),
            pl.BlockSpec((d, ROUTER_LANES), lambda i: (0, 0)),
        ],
        out_specs=[
            pl.BlockSpec((tm, d), lambda i: (i, 0)),
            pl.BlockSpec((tm, ROUTE_COLS), lambda i: (i, 0)),
        ],
        out_shape=[
            jax.ShapeDtypeStruct((t, d), F32),
            jax.ShapeDtypeStruct((t, ROUTE_COLS), F32),
        ],
        compiler_params=_cparams(("arbitrary",)),
        name="moe_router",
    )(x2, g.reshape(1, d), wr)


def _moe_plan(route, tm, n_tiles):
    experts = route[:, :2].astype(jnp.int32).reshape(-1)
    onehot = (experts[:, None] == jnp.arange(N_EXPERTS, dtype=jnp.int32)[None, :]).astype(jnp.int32)
    csum = jnp.cumsum(onehot, axis=0)
    rank = jnp.sum((csum - onehot) * onehot, axis=1)
    count = csum[-1]
    padded = ((count + tm - 1) // tm) * tm
    end = jnp.cumsum(padded)
    start = end - padded
    slot = jnp.sum(onehot * start[None, :], axis=1) + rank
    n_active = end[-1] // tm
    tile_start = jnp.arange(n_tiles, dtype=jnp.int32) * tm
    tile_expert = jnp.sum((tile_start[:, None] >= end[None, :]).astype(jnp.int32), axis=1)
    tile_expert = jnp.minimum(tile_expert, N_EXPERTS - 1)
    last = tile_expert[jnp.maximum(n_active - 1, 0)]
    tile_expert = jnp.where(jnp.arange(n_tiles) < n_active, tile_expert, last)
    return slot.astype(jnp.int32), tile_expert.astype(jnp.int32), n_active.astype(jnp.int32).reshape(1)


def _row_copy(src, src_row, dst, dst_row, sem):
    return pltpu.make_async_copy(src.at[pl.ds(src_row, 1)], dst.at[pl.ds(dst_row, 1)], sem)


def _dispatch_kernel(slot_ref, h_ref, xs_in_ref, xs_ref, sems, *, tmd):
    del xs_in_ref
    i = pl.program_id(0)
    n = pl.num_programs(0)
    par = i % 2

    def issue(j, carry):
        t = i * tmd + j
        _row_copy(h_ref, t, xs_ref, slot_ref[0, 0, 2 * j], sems.at[par]).start()
        _row_copy(h_ref, t, xs_ref, slot_ref[0, 0, 2 * j + 1], sems.at[par]).start()
        return carry

    lax.fori_loop(0, tmd, issue, 0)

    def drain(p):
        def wait(j, carry):
            _row_copy(h_ref, 0, xs_ref, 0, sems.at[p]).wait()
            return carry
        lax.fori_loop(0, 2 * tmd, wait, 0)

    @pl.when(i > 0)
    def _():
        drain(1 - par)

    @pl.when(i == n - 1)
    def _():
        drain(par)


def _dispatch(h, slot, cap, tmd=256):
    t, d = h.shape
    n_blk = t // tmd
    slot3 = slot.reshape(n_blk, 1, 2 * tmd)
    return pl.pallas_call(
        functools.partial(_dispatch_kernel, tmd=tmd),
        grid=(n_blk,),
        in_specs=[
            pl.BlockSpec((1, 1, 2 * tmd), lambda i: (i, 0, 0), memory_space=pltpu.SMEM),
            pl.BlockSpec(memory_space=pl.ANY),
            pl.BlockSpec(memory_space=pl.ANY),
        ],
        out_specs=pl.BlockSpec(memory_space=pl.ANY),
        out_shape=jax.ShapeDtypeStruct((cap, d), F32),
        scratch_shapes=[pltpu.SemaphoreType.DMA((2,))],
        input_output_aliases={2: 0},
        compiler_params=_cparams(("arbitrary",)),
        name="moe_dispatch",
    )(slot3, h, jnp.zeros((cap, d), F32))


def _moe_ffn_kernel(te_ref, nact_ref, xs_ref, wg_ref, wu_ref, wd_ref, ys_ref, acc_ref):
    del te_ref
    i = pl.program_id(0)
    f = pl.program_id(1)

    @pl.when(i < nact_ref[0])
    def _():
        @pl.when(f == 0)
        def _():
            acc_ref[...] = jnp.zeros_like(acc_ref)

        x = xs_ref[...].astype(BF16)
        gate = jnp.dot(x, wg_ref[...], preferred_element_type=F32)
        up = jnp.dot(x, wu_ref[...], preferred_element_type=F32)
        act = (gate * jax.nn.sigmoid(gate) * up).astype(BF16)
        acc_ref[...] += jnp.dot(act, wd_ref[...], preferred_element_type=F32)

        @pl.when(f == pl.num_programs(1) - 1)
        def _():
            ys_ref[...] = acc_ref[...]

    @pl.when((i >= nact_ref[0]) & (f == pl.num_programs(1) - 1))
    def _():
        ys_ref[...] = jnp.zeros_like(ys_ref)


def _moe_ffn(xs, tile_expert, n_active, wg, wu, wd, tm, tf=512):
    cap, d = xs.shape
    dff = wg.shape[-1]
    n_f = dff // tf
    row = lambda i, f, te, na: (jnp.minimum(i, na[0] - 1), 0)
    fidx = lambda i, f, na: jnp.where(i < na[0], f, n_f - 1)
    grid_spec = pltpu.PrefetchScalarGridSpec(
        num_scalar_prefetch=2,
        grid=(cap // tm, n_f),
        in_specs=[
            pl.BlockSpec((tm, d), row),
            pl.BlockSpec((None, d, tf), lambda i, f, te, na: (te[i], 0, fidx(i, f, na))),
            pl.BlockSpec((None, d, tf), lambda i, f, te, na: (te[i], 0, fidx(i, f, na))),
            pl.BlockSpec((None, tf, d), lambda i, f, te, na: (te[i], fidx(i, f, na), 0)),
        ],
        out_specs=pl.BlockSpec((tm, d), lambda i, f, te, na: (i, 0)),
        scratch_shapes=[pltpu.VMEM((tm, d), F32)],
    )
    return pl.pallas_call(
        _moe_ffn_kernel,
        grid_spec=grid_spec,
        out_shape=jax.ShapeDtypeStruct((cap, d), F32),
        compiler_params=_cparams(("arbitrary", "arbitrary")),
        name="moe_ffn",
    )(tile_expert, n_active, xs, wg, wu, wd)


def _combine_kernel(slot_ref, route_ref, x_ref, fg_ref, ys_ref, o_ref, buf, sem, *, tmc, final_norm):
    def issue(j, carry):
        _row_copy(ys_ref, slot_ref[0, 0, 2 * j], buf.at[0], j, sem).start()
        _row_copy(ys_ref, slot_ref[0, 0, 2 * j + 1], buf.at[1], j, sem).start()
        return carry

    lax.fori_loop(0, tmc, issue, 0)

    def wait(j, carry):
        _row_copy(ys_ref, 0, buf.at[0], 0, sem).wait()
        return carry

    lax.fori_loop(0, 2 * tmc, wait, 0)

    r = route_ref[...]
    y = x_ref[...] + r[:, 2:3] * buf[0] + r[:, 3:4] * buf[1]
    o_ref[...] = _rms(y, fg_ref[...]) if final_norm else y


def _combine(ys, slot, route, x2, final_g, final_norm, tmc=256):
    t, d = x2.shape
    n_blk = t // tmc
    slot3 = slot.reshape(n_blk, 1, 2 * tmc)
    return pl.pallas_call(
        functools.partial(_combine_kernel, tmc=tmc, final_norm=final_norm),
        grid=(n_blk,),
        in_specs=[
            pl.BlockSpec((1, 1, 2 * tmc), lambda i: (i, 0, 0), memory_space=pltpu.SMEM),
            pl.BlockSpec((tmc, ROUTE_COLS), lambda i: (i, 0)),
            pl.BlockSpec((tmc, d), lambda i: (i, 0)),
            pl.BlockSpec((1, d), lambda i: (0, 0)),
            pl.BlockSpec(memory_space=pl.ANY),
        ],
        out_specs=pl.BlockSpec((tmc, d), lambda i: (i, 0)),
        out_shape=jax.ShapeDtypeStruct((t, d), F32),
        scratch_shapes=[pltpu.VMEM((2, tmc, d), F32), pltpu.SemaphoreType.DMA(())],
        compiler_params=_cparams(("arbitrary",)),
        name="moe_combine",
    )(slot3, route, x2, final_g.reshape(1, d), ys)


def _final_norm_kernel(x_ref, g_ref, o_ref):
    o_ref[...] = _rms(x_ref[...], g_ref[...])


def _final_norm(x2, g, tm=512):
    t, d = x2.shape
    return pl.pallas_call(
        _final_norm_kernel,
        grid=(t // tm,),
        in_specs=[pl.BlockSpec((tm, d), lambda i: (i, 0)), pl.BlockSpec((1, d), lambda i: (0, 0))],
        out_specs=pl.BlockSpec((tm, d), lambda i: (i, 0)),
        out_shape=jax.ShapeDtypeStruct((t, d), F32),
        compiler_params=_cparams(("arbitrary",)),
        name="final_norm",
    )(x2, g.reshape(1, d))


MOE_TM = 512


def kernel(x, norm1_g, w_in, na_rpb, sc_conv_w, cf_conv_w, cf_conv_b, cf_ln_g, cf_ln_b, ssm_a_re, ssm_a_im,
           ssm_log_dt, ssm_b_re, ssm_b_im, ssm_c_re, ssm_c_im, ssm_d, ssm_w_glu, ssm_b_glu, grp_norm_g, w_out,
           norm2_g, ffn_w_gate, ffn_w_up, ffn_w_down, moe_w_router, moe_w_gate, moe_w_up, moe_w_down,
           final_norm_g):
    bsz, seq, d = x.shape
    depth = w_in.shape[0]
    t = bsz * seq
    rows = seq // GRID_W
    x2 = x.reshape(t, d).astype(F32)
    final_done = False
    for i in range(depth):
        qkv, rest, ssm_lb = _in_proj(x2, norm1_g[i], w_in[i].astype(BF16), bsz, seq)
        y_na = _na_attention(qkv, _na_bias_table(na_rpb[i], rows), bsz, seq)
        y_sc, y_cf = _conv_mixers(rest, sc_conv_w[i], cf_conv_w[i], cf_conv_b[i], cf_ln_g[i], cf_ln_b[i],
                                  bsz, seq)
        bmat, cmat, amat = _s5_discretise(ssm_a_re[i], ssm_a_im[i], ssm_log_dt[i], ssm_b_re[i], ssm_b_im[i],
                                          ssm_c_re[i], ssm_c_im[i], bsz)
        y_dirs = _s5_scan(ssm_lb.reshape(seq * bsz, GROUP_W), bmat, cmat, amat, bsz, seq)
        y_ssm = _s5_post(ssm_lb, y_dirs.reshape(2, seq, bsz * GROUP_W), ssm_d[i],
                         ssm_w_glu[i].astype(BF16), ssm_b_glu[i], bsz, seq)
        x2 = _out_proj((y_na, y_sc, y_cf, y_ssm), grp_norm_g[i], w_out[i].astype(BF16), x2)
        j = i // 2
        if i % 2 == 0:
            x2 = _dense_ffn(x2, norm2_g[i], ffn_w_gate[j].astype(BF16), ffn_w_up[j].astype(BF16),
                            ffn_w_down[j].astype(BF16))
        else:
            h, route = _router(x2, norm2_g[i], moe_w_router[j])
            n_tiles = (2 * t) // MOE_TM + N_EXPERTS
            slot, tile_expert, n_active = _moe_plan(route, MOE_TM, n_tiles)
            xs = _dispatch(h, slot, n_tiles * MOE_TM)
            ys = _moe_ffn(xs, tile_expert, n_active, moe_w_gate[j].astype(BF16), moe_w_up[j].astype(BF16),
                          moe_w_down[j].astype(BF16), MOE_TM)
            final_done = i == depth - 1
            x2 = _combine(ys, slot, route, x2, final_norm_g, final_done)
    if not final_done:
        x2 = _final_norm(x2, final_norm_g)
    return x2.reshape(bsz, seq, d)
```

```python
import functools
import math

import jax
import jax.numpy as jnp
import numpy as np
from jax import lax
from jax.experimental import pallas as pl
from jax.experimental.pallas import tpu as pltpu

F32 = jnp.float32
BF16 = jnp.bfloat16
EPS = 1e-6
NEG_INF = -1e30

GRID_W = 64
NA_HEAD_DIM = 64
NA_WIN_H = 8
NA_WIN_W = 16
GROUP_W = 256
SSM_GROUP_CH = 16
SSM_STATE = 64
N_EXPERTS = 8
VMEM_LIMIT = 56 * 1024 * 1024


def _cparams(sem, vmem=VMEM_LIMIT):
    return pltpu.CompilerParams(dimension_semantics=sem, vmem_limit_bytes=vmem)


def _rms(x, g):
    return x * lax.rsqrt(jnp.mean(x * x, axis=-1, keepdims=True) + EPS) * g


SUBLANES = 8
LANES = 128


def _load_token_tiles(ref, n):
    return jnp.concatenate([ref[pl.ds(s, n, stride=SUBLANES), :] for s in range(SUBLANES)], axis=1)


def _store_token_tiles(ref, val):
    for s in range(SUBLANES):
        ref[pl.ds(s, val.shape[0], stride=SUBLANES), :] = val[:, s * LANES:(s + 1) * LANES]


def _in_proj_kernel(x_ref, g_ref, w_ref, qkv_ref, rest_ref, ssm_ref):
    h = _rms(x_ref[...], g_ref[...]).astype(BF16)
    n_qkv = qkv_ref.shape[1] // GROUP_W
    n_rest = rest_ref.shape[1] // GROUP_W
    for j in range(n_qkv + n_rest + 1):
        z = jnp.dot(h, w_ref[:, j * GROUP_W:(j + 1) * GROUP_W], preferred_element_type=F32)
        if j < n_qkv:
            qkv_ref[:, j * GROUP_W:(j + 1) * GROUP_W] = z.astype(BF16)
        elif j < n_qkv + n_rest:
            k = j - n_qkv
            rest_ref[:, k * GROUP_W:(k + 1) * GROUP_W] = z
        else:
            ssm_ref[...] = z


def _in_proj(x2, g, w, bsz, seq, tm=512):
    t, d = x2.shape
    n_cols = w.shape[1]
    lt = seq // tm
    return pl.pallas_call(
        _in_proj_kernel,
        grid=(t // tm,),
        in_specs=[
            pl.BlockSpec((tm, d), lambda i: (i, 0)),
            pl.BlockSpec((1, d), lambda i: (0, 0)),
            pl.BlockSpec((d, n_cols), lambda i: (0, 0)),
        ],
        out_specs=[
            pl.BlockSpec((tm, 3 * GROUP_W), lambda i: (i, 0)),
            pl.BlockSpec((tm, 5 * GROUP_W), lambda i: (i, 0)),
            pl.BlockSpec((tm, GROUP_W), lambda i: (i % lt, i // lt)),
        ],
        out_shape=[
            jax.ShapeDtypeStruct((t, 3 * GROUP_W), BF16),
            jax.ShapeDtypeStruct((t, 5 * GROUP_W), F32),
            jax.ShapeDtypeStruct((seq, bsz * GROUP_W), F32),
        ],
        compiler_params=_cparams(("arbitrary",)),
        name="in_proj",
    )(x2, g.reshape(1, d), w)


def _na_bias_table(rpb, rows):
    n_heads = rpb.shape[0]
    kh = NA_WIN_H
    c = np.arange(GRID_W)
    qcs = np.clip(c - NA_WIN_W // 2, 0, GRID_W - NA_WIN_W)
    kc = np.arange(GRID_W)
    in_win = (kc[None, :] >= qcs[:, None]) & (kc[None, :] < qcs[:, None] + NA_WIN_W)
    dc_idx = np.clip(kc[None, :] - c[:, None] + NA_WIN_W - 1, 0, 2 * NA_WIN_W - 2)
    onehot_dc = (dc_idx[:, :, None] == np.arange(2 * NA_WIN_W - 1)).astype(np.float32)
    tab = jnp.einsum("hab,ckb->hcak", rpb.astype(F32), jnp.asarray(onehot_dc),
                     precision=lax.Precision.HIGHEST)
    tab = jnp.where(jnp.asarray(in_win)[None, :, None, :], tab, NEG_INF)
    rep_rows = [0, 1, 2, 3, 4, rows - 3, rows - 2, rows - 1]
    tabs = []
    for r in rep_rows:
        rs = int(np.clip(r - kh // 2, 0, rows - kh))
        dr0 = rs - r + NA_WIN_H - 1
        tabs.append(tab[:, :, dr0:dr0 + kh, :].reshape(n_heads * GRID_W, kh * GRID_W))
    return jnp.stack(tabs)


def _na_kernel(q_ref, k_ref, v_ref, bias_ref, o_ref, *, rows, n_heads):
    kh = NA_WIN_H
    lane = lax.broadcasted_iota(jnp.int32, (GRID_W, n_heads * NA_HEAD_DIM), 1)
    masks = [(lane >= h * NA_HEAD_DIM) & (lane < (h + 1) * NA_HEAD_DIM) for h in range(n_heads)]
    scale = NA_HEAD_DIM ** -0.5

    def body(r, carry):
        q0 = pl.multiple_of(r * GRID_W, GRID_W)
        q = q_ref[pl.ds(q0, GRID_W), :].astype(F32) * scale
        qs = jnp.concatenate([jnp.where(m, q, 0.0) for m in masks], axis=0).astype(BF16)
        rs = jnp.clip(r - kh // 2, 0, rows - kh)
        k0 = pl.multiple_of(rs * GRID_W, GRID_W)
        ks = k_ref[pl.ds(k0, kh * GRID_W), :]
        vs = v_ref[pl.ds(k0, kh * GRID_W), :]
        s = lax.dot_general(qs, ks, (((1,), (1,)), ((), ())), preferred_element_type=F32)
        var = jnp.where(r < kh // 2, r, jnp.where(r > rows - kh // 2, r - (rows - 8), kh // 2))
        s = s + bias_ref[var]
        m = jnp.max(s, axis=-1, keepdims=True)
        p = jnp.exp(s - m)
        den = jnp.sum(p, axis=-1, keepdims=True)
        o = jnp.dot(p.astype(BF16), vs, preferred_element_type=F32) / den
        out = jnp.where(masks[0], o[0:GRID_W], 0.0)
        for h in range(1, n_heads):
            out = out + jnp.where(masks[h], o[h * GRID_W:(h + 1) * GRID_W], 0.0)
        o_ref[pl.ds(q0, GRID_W), :] = out
        return carry

    lax.fori_loop(0, rows, body, 0)


def _na_attention(qkv, bias, bsz, seq):
    rows = seq // GRID_W
    n_heads = GROUP_W // NA_HEAD_DIM
    t = qkv.shape[0]
    blk = lambda j: pl.BlockSpec((seq, GROUP_W), lambda b, j=j: (b, j))
    return pl.pallas_call(
        functools.partial(_na_kernel, rows=rows, n_heads=n_heads),
        grid=(bsz,),
        in_specs=[blk(0), blk(1), blk(2),
                  pl.BlockSpec(bias.shape, lambda b: (0, 0, 0))],
        out_specs=pl.BlockSpec((seq, GROUP_W), lambda b: (b, 0)),
        out_shape=jax.ShapeDtypeStruct((t, GROUP_W), F32),
        compiler_params=_cparams(("arbitrary",)),
        name="na_attention",
    )(qkv, qkv, qkv, bias)


CONV_PAD = 16
CONV_CHUNK = 128


def _conv_kernel(b_ref, c_ref, x_ref, a_ref, g_ref, scw_ref, cfw_ref, cfb_ref, lng_ref, lnb_ref,
                 ysc_ref, ycf_ref, psc, pcf):
    seq = b_ref.shape[0]
    n_chunks = seq // CONV_CHUNK
    zeros = jnp.zeros((CONV_PAD, GROUP_W), F32)
    psc[0:CONV_PAD, :] = zeros
    pcf[0:CONV_PAD, :] = zeros
    psc[CONV_PAD + seq:2 * CONV_PAD + seq, :] = zeros
    pcf[CONV_PAD + seq:2 * CONV_PAD + seq, :] = zeros

    def fill(i, carry):
        r0 = pl.multiple_of(i * CONV_CHUNK, CONV_CHUNK)
        rows = pl.ds(r0, CONV_CHUNK)
        dst = pl.ds(r0 + CONV_PAD, CONV_CHUNK)
        psc[dst, :] = c_ref[rows, :] * x_ref[rows, :]
        pcf[dst, :] = a_ref[rows, :] * jax.nn.sigmoid(g_ref[rows, :])
        return carry

    lax.fori_loop(0, n_chunks, fill, 0)

    k_sc = scw_ref.shape[0]
    k_cf = cfw_ref.shape[0]

    def conv(i, carry):
        r0 = pl.multiple_of(i * CONV_CHUNK, CONV_CHUNK)
        rows = pl.ds(r0, CONV_CHUNK)
        win = psc[pl.ds(r0, CONV_CHUNK + 2 * CONV_PAD), :]
        acc = None
        for k in range(k_sc):
            o = CONV_PAD + k - k_sc // 2
            term = win[o:o + CONV_CHUNK, :] * scw_ref[k:k + 1, :]
            acc = term if acc is None else acc + term
        ysc_ref[rows, :] = b_ref[rows, :] * acc

        win = pcf[pl.ds(r0, CONV_CHUNK + 2 * CONV_PAD), :]
        acc = None
        for k in range(k_cf):
            o = CONV_PAD + k - k_cf // 2
            term = win[o:o + CONV_CHUNK, :] * cfw_ref[k:k + 1, :]
            acc = term if acc is None else acc + term
        cf = acc + cfb_ref[...]
        mu = jnp.mean(cf, axis=-1, keepdims=True)
        xc = cf - mu
        var = jnp.mean(xc * xc, axis=-1, keepdims=True)
        ln = xc * lax.rsqrt(var + EPS) * lng_ref[...] + lnb_ref[...]
        ycf_ref[rows, :] = ln * jax.nn.sigmoid(ln)
        return carry

    lax.fori_loop(0, n_chunks, conv, 0)


def _conv_mixers(rest, sc_w, cf_w, cf_b, ln_g, ln_b, bsz, seq):
    t = rest.shape[0]
    blk = lambda j: pl.BlockSpec((seq, GROUP_W), lambda b, j=j: (b, j))
    full = lambda a: pl.BlockSpec(a.shape, lambda b: (0,) * a.ndim)
    cf_b, ln_g, ln_b = (a.reshape(1, GROUP_W) for a in (cf_b, ln_g, ln_b))
    out_spec = pl.BlockSpec((seq, GROUP_W), lambda b: (b, 0))
    return pl.pallas_call(
        _conv_kernel,
        grid=(bsz,),
        in_specs=[blk(0), blk(1), blk(2), blk(3), blk(4),
                  full(sc_w), full(cf_w), full(cf_b), full(ln_g), full(ln_b)],
        out_specs=[out_spec, out_spec],
        out_shape=[jax.ShapeDtypeStruct((t, GROUP_W), F32)] * 2,
        scratch_shapes=[pltpu.VMEM((seq + 2 * CONV_PAD, GROUP_W), F32)] * 2,
        compiler_params=_cparams(("arbitrary",)),
        name="conv_mixers",
    )(rest, rest, rest, rest, rest, sc_w, cf_w, cf_b, ln_g, ln_b)


S5_TL = 64
S5_LANE_SPLIT = 2


def _s5_discretise(a_re, a_im, log_dt, b_re, b_im, c_re, c_im, bsz):
    f32 = F32
    a_re, a_im, log_dt = a_re.astype(f32), a_im.astype(f32), log_dt.astype(f32)
    n_dir, n_grp, n_state = a_re.shape
    n_ch = b_re.shape[-1]
    dt = jnp.exp(log_dt)[..., None]
    mag = jnp.exp(a_re * dt)
    abr = mag * jnp.cos(a_im * dt)
    abi = mag * jnp.sin(a_im * dt)
    den = a_re * a_re + a_im * a_im
    fr = ((abr - 1.0) * a_re + abi * a_im) / den
    fi = (abi * a_re - (abr - 1.0) * a_im) / den
    bbr = fr[..., None] * b_re - fi[..., None] * b_im
    bbi = fr[..., None] * b_im + fi[..., None] * b_re
    eye = jnp.eye(n_grp, dtype=f32)
    blk_b = lambda m: jnp.einsum("dgps,gh->dgshp", m, eye).reshape(n_dir, n_grp * n_ch, n_grp * n_state)
    blk_c = lambda m: jnp.einsum("dgsp,gh->dgphs", m, eye).reshape(n_dir, n_grp * n_state, n_grp * n_ch)
    bmat = jnp.concatenate([blk_b(bbr), blk_b(bbi)], axis=-1)
    cmat = jnp.concatenate([blk_c(c_re.astype(f32)), blk_c(-c_im.astype(f32))], axis=1)
    amat = jnp.stack([abr.reshape(n_dir, -1), abi.reshape(n_dir, -1)], axis=1)
    amat = jnp.broadcast_to(amat[:, :, None, :], (n_dir, 2, bsz, n_grp * n_state))
    return bmat.astype(BF16), cmat.astype(BF16), amat


def _s5_scan_kernel(u_ref, b_ref, c_ref, a_ref, y_ref, xs, st, *, bsz):
    d = pl.program_id(0)
    i = pl.program_id(1)
    n_state = a_ref.shape[-1]

    @pl.when(i == 0)
    def _():
        st[...] = jnp.zeros_like(st)

    xs[...] = jnp.dot(u_ref[...].astype(BF16), b_ref[...], preferred_element_type=F32)

    cw = n_state // S5_LANE_SPLIT
    for c in range(S5_LANE_SPLIT):
        re = slice(c * cw, (c + 1) * cw)
        im = slice(n_state + c * cw, n_state + (c + 1) * cw)
        a_r = a_ref[0, :, re]
        a_i = a_ref[1, :, re]

        def step(t, carry, re=re, im=im, a_r=a_r, a_i=a_i):
            s_r, s_i = carry
            tt = t + d * (S5_TL - 1 - 2 * t)
            rows = pl.ds(pl.multiple_of(tt * bsz, bsz), bsz)
            n_r = a_r * s_r - a_i * s_i + xs[rows, re]
            n_i = a_r * s_i + a_i * s_r + xs[rows, im]
            xs[rows, re] = n_r
            xs[rows, im] = n_i
            return n_r, n_i

        s_r, s_i = lax.fori_loop(0, S5_TL, step, (st[:, re], st[:, im]), unroll=4)
        st[:, re] = s_r
        st[:, im] = s_i

    y_ref[...] = jnp.dot(xs[...].astype(BF16), c_ref[...], preferred_element_type=F32)


def _s5_scan(u_t, bmat, cmat, amat, bsz, seq):
    n_chunks = seq // S5_TL
    rows = S5_TL * bsz
    n_in = u_t.shape[1]
    n_st2 = bmat.shape[-1]
    chunk = lambda d, i: i + d * (n_chunks - 1 - 2 * i)
    return pl.pallas_call(
        functools.partial(_s5_scan_kernel, bsz=bsz),
        grid=(2, n_chunks),
        in_specs=[
            pl.BlockSpec((rows, n_in), lambda d, i: (chunk(d, i), 0)),
            pl.BlockSpec((None, n_in, n_st2), lambda d, i: (d, 0, 0)),
            pl.BlockSpec((None, n_st2, n_in), lambda d, i: (d, 0, 0)),
            pl.BlockSpec((None, 2, bsz, n_st2 // 2), lambda d, i: (d, 0, 0, 0)),
        ],
        out_specs=pl.BlockSpec((None, rows, n_in), lambda d, i: (d, chunk(d, i), 0)),
        out_shape=jax.ShapeDtypeStruct((2, seq * bsz, n_in), F32),
        scratch_shapes=[pltpu.VMEM((rows, n_st2), F32), pltpu.VMEM((bsz, n_st2), F32)],
        compiler_params=_cparams(("arbitrary", "arbitrary")),
        name="s5_scan",
    )(u_t, bmat, cmat, amat)


def _s5_post_kernel(u_ref, yf_ref, yb_ref, d_ref, w_ref, b_ref, o_ref):
    y = d_ref[...] * u_ref[...] + yf_ref[...] + yb_ref[...]
    y = jax.nn.gelu(y)
    gate = jnp.dot(y.astype(BF16), w_ref[...], preferred_element_type=F32) + b_ref[...]
    o_ref[...] = y * jax.nn.sigmoid(gate)


def _s5_post(u_lb, y_dirs, d_skip, w_glu, b_glu, bsz, seq, tl=512):
    lt = seq // tl
    return pl.pallas_call(
        _s5_post_kernel,
        grid=(bsz, lt),
        in_specs=[
            pl.BlockSpec((tl, GROUP_W), lambda b, l: (l, b)),
            pl.BlockSpec((None, tl, GROUP_W), lambda b, l: (0, l, b)),
            pl.BlockSpec((None, tl, GROUP_W), lambda b, l: (1, l, b)),
            pl.BlockSpec((1, GROUP_W), lambda b, l: (0, 0)),
            pl.BlockSpec((GROUP_W, GROUP_W), lambda b, l: (0, 0)),
            pl.BlockSpec((1, GROUP_W), lambda b, l: (0, 0)),
        ],
        out_specs=pl.BlockSpec((tl, GROUP_W), lambda b, l: (b * lt + l, 0)),
        out_shape=jax.ShapeDtypeStruct((bsz * seq, GROUP_W), F32),
        compiler_params=_cparams(("arbitrary", "arbitrary")),
        name="s5_post",
    )(u_lb, y_dirs, y_dirs, d_skip.reshape(1, GROUP_W), w_glu, b_glu.reshape(1, GROUP_W))


def _out_proj_kernel(y0_ref, y1_ref, y2_ref, y3_ref, gn_ref, w_ref, x_ref, o_ref, *, token_tiles):
    acc = x_ref[...]
    for j, y_ref in enumerate((y0_ref, y1_ref, y2_ref, y3_ref)):
        yn = _rms(y_ref[...], gn_ref[j:j + 1, :]).astype(BF16)
        acc = acc + jnp.dot(yn, w_ref[j * GROUP_W:(j + 1) * GROUP_W, :], preferred_element_type=F32)
    if token_tiles:
        _store_token_tiles(o_ref, acc)
    else:
        o_ref[...] = acc


def _out_proj(ys, gn, w, x2, token_tiles, tm=512):
    t, d = x2.shape
    yspec = pl.BlockSpec((tm, GROUP_W), lambda i: (i, 0))
    out_block, out_rows = ((tm * SUBLANES, LANES), t * SUBLANES) if token_tiles else ((tm, d), t)
    return pl.pallas_call(
        functools.partial(_out_proj_kernel, token_tiles=token_tiles),
        grid=(t // tm,),
        in_specs=[yspec] * 4 + [
            pl.BlockSpec((4, GROUP_W), lambda i: (0, 0)),
            pl.BlockSpec(w.shape, lambda i: (0, 0)),
            pl.BlockSpec((tm, d), lambda i: (i, 0)),
        ],
        out_specs=pl.BlockSpec(out_block, lambda i: (i, 0)),
        out_shape=jax.ShapeDtypeStruct((out_rows, out_block[1]), F32),
        compiler_params=_cparams(("arbitrary",)),
        name="out_proj",
    )(*ys, gn.reshape(4, GROUP_W), w, x2)


def _ffn_kernel(x_ref, g_ref, wg_ref, wu_ref, wd_ref, o_ref, h_ref, acc_ref):
    f = pl.program_id(1)

    @pl.when(f == 0)
    def _():
        h_ref[...] = _rms(x_ref[...], g_ref[...]).astype(BF16)
        acc_ref[...] = x_ref[...]

    h = h_ref[...]
    gate = jnp.dot(h, wg_ref[...], preferred_element_type=F32)
    up = jnp.dot(h, wu_ref[...], preferred_element_type=F32)
    act = (gate * jax.nn.sigmoid(gate) * up).astype(BF16)
    acc_ref[...] += jnp.dot(act, wd_ref[...], preferred_element_type=F32)

    @pl.when(f == pl.num_programs(1) - 1)
    def _():
        o_ref[...] = acc_ref[...]


def _dense_ffn(x2, g, wg, wu, wd, tm=512, tf=1408):
    t, d = x2.shape
    dff = wg.shape[1]
    return pl.pallas_call(
        _ffn_kernel,
        grid=(t // tm, dff // tf),
        in_specs=[
            pl.BlockSpec((tm, d), lambda i, f: (i, 0)),
            pl.BlockSpec((1, d), lambda i, f: (0, 0)),
            pl.BlockSpec((d, tf), lambda i, f: (0, f)),
            pl.BlockSpec((d, tf), lambda i, f: (0, f)),
            pl.BlockSpec((tf, d), lambda i, f: (f, 0)),
        ],
        out_specs=pl.BlockSpec((tm, d), lambda i, f: (i, 0)),
        out_shape=jax.ShapeDtypeStruct((t, d), F32),
        scratch_shapes=[pltpu.VMEM((tm, d), BF16), pltpu.VMEM((tm, d), F32)],
        compiler_params=_cparams(("arbitrary", "arbitrary")),
        name="dense_ffn",
    )(x2, g.reshape(1, d), wg, wu, wd)


ROUTE_COLS = 8
ROUTER_LANES = 128


def _router_kernel(x_ref, g_ref, wr_ref, route_ref):
    h = _rms(_load_token_tiles(x_ref, route_ref.shape[0]), g_ref[...])
    logits = jnp.dot(h, wr_ref[...], preferred_element_type=F32, precision=lax.Precision.HIGHEST)
    lane = lax.broadcasted_iota(jnp.int32, logits.shape, 1)
    minus_inf = -jnp.inf
    l1 = jnp.where(lane < N_EXPERTS, logits, minus_inf)
    m1 = jnp.max(l1, axis=-1, keepdims=True)
    i1 = jnp.min(jnp.where(l1 == m1, lane, ROUTER_LANES), axis=-1, keepdims=True)
    l2 = jnp.where(lane == i1, minus_inf, l1)
    m2 = jnp.max(l2, axis=-1, keepdims=True)
    i2 = jnp.min(jnp.where(l2 == m2, lane, ROUTER_LANES), axis=-1, keepdims=True)
    e2 = jnp.exp(m2 - m1)
    den = 1.0 + e2
    rec = jnp.where(lane == 0, i1.astype(F32),
                    jnp.where(lane == 1, i2.astype(F32),
                              jnp.where(lane == 2, 1.0 / den,
                                        jnp.where(lane == 3, e2 / den, 0.0))))
    route_ref[...] = rec[:, :ROUTE_COLS]


def _router(xt, g, w_router, tm=512):
    t = xt.shape[0] // SUBLANES
    d = SUBLANES * LANES
    wr = jnp.zeros((d, ROUTER_LANES), F32).at[:, :N_EXPERTS].set(w_router.astype(F32))
    return pl.pallas_call(
        _router_kernel,
        grid=(t // tm,),
        in_specs=[
            pl.BlockSpec((tm * SUBLANES, LANES), lambda i: (i, 0)),
            pl.BlockSpec((1, d), lambda i: (0, 0)),
            pl.BlockSpec((d, ROUTER_LANES), lambda i: (0, 0)),
        ],
        out_specs=pl.BlockSpec((tm, ROUTE_COLS), lambda i: (i, 0)),
        out_shape=jax.ShapeDtypeStruct((t, ROUTE_COLS), F32),
        compiler_params=_cparams(("arbitrary",)),
        name="moe_router",
    )(xt, g.reshape(1, d), wr)


def _moe_plan(route, tm, n_tiles):
    experts = route[:, :2].astype(jnp.int32).reshape(-1)
    onehot = (experts[:, None] == jnp.arange(N_EXPERTS, dtype=jnp.int32)[None, :]).astype(jnp.int32)
    csum = jnp.cumsum(onehot, axis=0)
    rank = jnp.sum((csum - onehot) * onehot, axis=1)
    count = csum[-1]
    padded = ((count + tm - 1) // tm) * tm
    end = jnp.cumsum(padded)
    start = end - padded
    slot = jnp.sum(onehot * start[None, :], axis=1) + rank
    n_active = end[-1] // tm
    tile_start = jnp.arange(n_tiles, dtype=jnp.int32) * tm
    tile_expert = jnp.sum((tile_start[:, None] >= end[None, :]).astype(jnp.int32), axis=1)
    tile_expert = jnp.minimum(tile_expert, N_EXPERTS - 1)
    last = tile_expert[jnp.maximum(n_active - 1, 0)]
    tile_expert = jnp.where(jnp.arange(n_tiles) < n_active, tile_expert, last)
    return slot.astype(jnp.int32), tile_expert.astype(jnp.int32), n_active.astype(jnp.int32).reshape(1)


def _token_copy(src, src_tok, dst, dst_tok, sem):
    rows = lambda tok: pl.ds(pl.multiple_of(tok * SUBLANES, SUBLANES), SUBLANES)
    return pltpu.make_async_copy(src.at[rows(src_tok)], dst.at[rows(dst_tok)], sem)


def _invert_kernel(slot_ref, inv_ref):
    def clear(s, carry):
        inv_ref[s] = -1
        return carry

    lax.fori_loop(0, inv_ref.shape[0], clear, 0, unroll=8)

    def place(a, carry):
        inv_ref[slot_ref[a]] = a
        return carry

    lax.fori_loop(0, slot_ref.shape[0], place, 0, unroll=8)


def _invert(slot, cap):
    smem = pl.BlockSpec(memory_space=pltpu.SMEM)
    return pl.pallas_call(
        _invert_kernel,
        in_specs=[smem],
        out_specs=smem,
        out_shape=jax.ShapeDtypeStruct((cap,), jnp.int32),
        name="moe_invert",
    )(slot)


def _tokens_done(ref, n_tok, sem):
    rows = pl.ds(0, n_tok * SUBLANES)
    pltpu.make_async_copy(ref.at[rows], ref.at[rows], sem).wait()


def _moe_ffn_kernel(te_ref, nact_ref, inv_ref, x_hbm, g_ref, wg_ref, wu_ref, wd_ref, yk_hbm,
                    xbuf, xb16, ybuf, acc_ref, sem_in, sem_out, *, tm, n_tok, n_f):
    del te_ref
    i = pl.program_id(0)
    f = pl.program_id(1)
    par = i % 2
    nact = nact_ref[0]
    per_step = tm // n_f

    def fetch(tile, p, j):
        a = inv_ref[tile * tm + j]
        _token_copy(x_hbm, jnp.maximum(a, 0) >> 1, xbuf.at[p], j, sem_in.at[p]).start()

    def send(a, p, j):
        dst = jnp.where(a >= 0, (a & 1) * n_tok + (a >> 1), 2 * n_tok + p * tm + j)
        _token_copy(ybuf.at[p], j, yk_hbm, dst, sem_out.at[p]).start()

    @pl.when((i == 0) & (f == 0))
    def _():
        def first(j, carry):
            fetch(0, 0, j)
            return carry
        lax.fori_loop(0, tm, first, 0)
        ybuf[...] = jnp.zeros_like(ybuf)
        for p in range(2):
            spare_rows = pl.ds((2 * n_tok + p * tm) * SUBLANES, tm * SUBLANES)
            spare = pltpu.make_async_copy(ybuf.at[p], yk_hbm.at[spare_rows], sem_out.at[p])
            spare.start()
            spare.wait()
        _tokens_done(xbuf.at[0], tm, sem_in.at[0])

    @pl.when((i >= 1) & (i <= nact) & (f == 0))
    def _():
        _tokens_done(xbuf.at[par], tm, sem_in.at[par])

    @pl.when(i < nact)
    def _():
        @pl.when(f == 0)
        def _():
            xb16[...] = _rms(_load_token_tiles(xbuf.at[par], tm), g_ref[...]).astype(BF16)
            acc_ref[...] = jnp.zeros_like(acc_ref)

        prev = jnp.maximum(i - 1, 0) * tm
        for jj in range(per_step):
            j = f * per_step + jj
            fetch(i + 1, 1 - par, j)
            send(jnp.where(i > 0, inv_ref[prev + j], -1), 1 - par, j)

        x = xb16[...]
        gate = jnp.dot(x, wg_ref[...], preferred_element_type=F32)
        up = jnp.dot(x, wu_ref[...], preferred_element_type=F32)
        act = (gate * jax.nn.sigmoid(gate) * up).astype(BF16)
        acc_ref[...] += jnp.dot(act, wd_ref[...], preferred_element_type=F32)

        @pl.when(f == n_f - 1)
        def _():
            @pl.when(i >= 1)
            def _():
                _tokens_done(ybuf.at[par], tm, sem_out.at[par])

            _store_token_tiles(ybuf.at[par], acc_ref[...])

    @pl.when((i == nact) & (f == 0))
    def _():
        def last(j, carry):
            send(inv_ref[(i - 1) * tm + j], 1 - par, j)
            return carry
        lax.fori_loop(0, tm, last, 0)
        _tokens_done(ybuf.at[0], tm, sem_out.at[0])
        _tokens_done(ybuf.at[1], tm, sem_out.at[1])


def _moe_ffn(xt, g, inv, tile_expert, n_active, wg, wu, wd, tm):
    n_tok = xt.shape[0] // SUBLANES
    d = SUBLANES * LANES
    n_tiles = tile_expert.shape[0]
    dff = wg.shape[-1]
    n_f = 4
    tf = dff // n_f
    fidx = lambda i, f, na: jnp.where(i < na[0], f, n_f - 1)
    grid_spec = pltpu.PrefetchScalarGridSpec(
        num_scalar_prefetch=3,
        grid=(n_tiles, n_f),
        in_specs=[
            pl.BlockSpec(memory_space=pl.ANY),
            pl.BlockSpec((1, d), lambda i, f, te, na, inv: (0, 0)),
            pl.BlockSpec((None, d, tf), lambda i, f, te, na, inv: (te[i], 0, fidx(i, f, na))),
            pl.BlockSpec((None, d, tf), lambda i, f, te, na, inv: (te[i], 0, fidx(i, f, na))),
            pl.BlockSpec((None, tf, d), lambda i, f, te, na, inv: (te[i], fidx(i, f, na), 0)),
        ],
        out_specs=pl.BlockSpec(memory_space=pl.ANY),
        scratch_shapes=[
            pltpu.VMEM((2, tm * SUBLANES, LANES), F32),
            pltpu.VMEM((tm, d), BF16),
            pltpu.VMEM((2, tm * SUBLANES, LANES), F32),
            pltpu.VMEM((tm, d), F32),
            pltpu.SemaphoreType.DMA((2,)),
            pltpu.SemaphoreType.DMA((2,)),
        ],
    )
    return pl.pallas_call(
        functools.partial(_moe_ffn_kernel, tm=tm, n_tok=n_tok, n_f=n_f),
        grid_spec=grid_spec,
        out_shape=jax.ShapeDtypeStruct(((2 * n_tok + 2 * tm) * SUBLANES, LANES), F32),
        compiler_params=_cparams(("arbitrary", "arbitrary")),
        name="moe_ffn",
    )(tile_expert, n_active, inv, xt, g.reshape(1, d), wg, wu, wd)


def _combine_kernel(route_ref, x_ref, y0_ref, y1_ref, fg_ref, o_ref, *, final_norm):
    r = route_ref[...]
    tm = r.shape[0]
    y = (_load_token_tiles(x_ref, tm) + r[:, 2:3] * _load_token_tiles(y0_ref, tm)
         + r[:, 3:4] * _load_token_tiles(y1_ref, tm))
    o_ref[...] = _rms(y, fg_ref[...]) if final_norm else y


def _combine(yk, route, xt, final_g, final_norm, tm=512):
    t = xt.shape[0] // SUBLANES
    d = SUBLANES * LANES
    n_blk = t // tm
    tiles = lambda f: pl.BlockSpec((tm * SUBLANES, LANES), f)
    return pl.pallas_call(
        functools.partial(_combine_kernel, final_norm=final_norm),
        grid=(n_blk,),
        in_specs=[
            pl.BlockSpec((tm, ROUTE_COLS), lambda i: (i, 0)),
            tiles(lambda i: (i, 0)),
            tiles(lambda i: (i, 0)),
            tiles(lambda i: (n_blk + i, 0)),
            pl.BlockSpec((1, d), lambda i: (0, 0)),
        ],
        out_specs=pl.BlockSpec((tm, d), lambda i: (i, 0)),
        out_shape=jax.ShapeDtypeStruct((t, d), F32),
        compiler_params=_cparams(("arbitrary",)),
        name="moe_combine",
    )(route, xt, yk, yk, final_g.reshape(1, d))


def _final_norm_kernel(x_ref, g_ref, o_ref):
    o_ref[...] = _rms(x_ref[...], g_ref[...])


def _final_norm(x2, g, tm=512):
    t, d = x2.shape
    return pl.pallas_call(
        _final_norm_kernel,
        grid=(t // tm,),
        in_specs=[pl.BlockSpec((tm, d), lambda i: (i, 0)), pl.BlockSpec((1, d), lambda i: (0, 0))],
        out_specs=pl.BlockSpec((tm, d), lambda i: (i, 0)),
        out_shape=jax.ShapeDtypeStruct((t, d), F32),
        compiler_params=_cparams(("arbitrary",)),
        name="final_norm",
    )(x2, g.reshape(1, d))


MOE_TM = 512


def kernel(x, norm1_g, w_in, na_rpb, sc_conv_w, cf_conv_w, cf_conv_b, cf_ln_g, cf_ln_b, ssm_a_re, ssm_a_im,
           ssm_log_dt, ssm_b_re, ssm_b_im, ssm_c_re, ssm_c_im, ssm_d, ssm_w_glu, ssm_b_glu, grp_norm_g, w_out,
           norm2_g, ffn_w_gate, ffn_w_up, ffn_w_down, moe_w_router, moe_w_gate, moe_w_up, moe_w_down,
           final_norm_g):
    bsz, seq, d = x.shape
    depth = w_in.shape[0]
    t = bsz * seq
    rows = seq // GRID_W
    x2 = x.reshape(t, d).astype(F32)
    final_done = False
    for i in range(depth):
        qkv, rest, ssm_lb = _in_proj(x2, norm1_g[i], w_in[i].astype(BF16), bsz, seq)
        y_na = _na_attention(qkv, _na_bias_table(na_rpb[i], rows), bsz, seq)
        y_sc, y_cf = _conv_mixers(rest, sc_conv_w[i], cf_conv_w[i], cf_conv_b[i], cf_ln_g[i], cf_ln_b[i],
                                  bsz, seq)
        bmat, cmat, amat = _s5_discretise(ssm_a_re[i], ssm_a_im[i], ssm_log_dt[i], ssm_b_re[i], ssm_b_im[i],
                                          ssm_c_re[i], ssm_c_im[i], bsz)
        y_dirs = _s5_scan(ssm_lb.reshape(seq * bsz, GROUP_W), bmat, cmat, amat, bsz, seq)
        y_ssm = _s5_post(ssm_lb, y_dirs.reshape(2, seq, bsz * GROUP_W), ssm_d[i],
                         ssm_w_glu[i].astype(BF16), ssm_b_glu[i], bsz, seq)
        x2 = _out_proj((y_na, y_sc, y_cf, y_ssm), grp_norm_g[i], w_out[i].astype(BF16), x2,
                       token_tiles=(i % 2 == 1))
        j = i // 2
        if i % 2 == 0:
            x2 = _dense_ffn(x2, norm2_g[i], ffn_w_gate[j].astype(BF16), ffn_w_up[j].astype(BF16),
                            ffn_w_down[j].astype(BF16))
        else:
            route = _router(x2, norm2_g[i], moe_w_router[j])
            n_tiles = (2 * t) // MOE_TM + N_EXPERTS + 1
            slot, tile_expert, n_active = _moe_plan(route, MOE_TM, n_tiles)
            inv = _invert(slot, n_tiles * MOE_TM)
            yk = _moe_ffn(x2, norm2_g[i], inv, tile_expert, n_active, moe_w_gate[j].astype(BF16),
                          moe_w_up[j].astype(BF16), moe_w_down[j].astype(BF16), MOE_TM)
            final_done = i == depth - 1
            x2 = _combine(yk, route, x2, final_norm_g, final_done)
    if not final_done:
        x2 = _final_norm(x2, final_norm_g)
    return x2.reshape(bsz, seq, d)
```

```python
import functools
import math

import jax
import jax.numpy as jnp
import numpy as np
from jax import lax
from jax.experimental import pallas as pl
from jax.experimental.pallas import tpu as pltpu

F32 = jnp.float32
BF16 = jnp.bfloat16
EPS = 1e-6
NEG_INF = -1e30

GRID_W = 64
NA_HEAD_DIM = 64
NA_WIN_H = 8
NA_WIN_W = 16
GROUP_W = 256
SSM_GROUP_CH = 16
SSM_STATE = 64
N_EXPERTS = 8
VMEM_LIMIT = 56 * 1024 * 1024


def _cparams(sem, vmem=VMEM_LIMIT):
    return pltpu.CompilerParams(dimension_semantics=sem, vmem_limit_bytes=vmem)


def _rms(x, g):
    return x * lax.rsqrt(jnp.mean(x * x, axis=-1, keepdims=True) + EPS) * g


SUBLANES = 8
LANES = 128


def _load_token_tiles(ref, n):
    return jnp.concatenate([ref[pl.ds(s, n, stride=SUBLANES), :] for s in range(SUBLANES)], axis=1)


def _store_token_tiles(ref, val):
    for s in range(SUBLANES):
        ref[pl.ds(s, val.shape[0], stride=SUBLANES), :] = val[:, s * LANES:(s + 1) * LANES]


def _in_proj_kernel(x_ref, g_ref, w_ref, qkv_ref, rest_ref, ssm_ref):
    h = _rms(x_ref[...], g_ref[...]).astype(BF16)
    n_qkv = qkv_ref.shape[1] // GROUP_W
    n_rest = rest_ref.shape[1] // GROUP_W
    for j in range(n_qkv + n_rest + 1):
        z = jnp.dot(h, w_ref[:, j * GROUP_W:(j + 1) * GROUP_W], preferred_element_type=F32)
        if j < n_qkv:
            qkv_ref[:, j * GROUP_W:(j + 1) * GROUP_W] = z.astype(BF16)
        elif j < n_qkv + n_rest:
            k = j - n_qkv
            rest_ref[:, k * GROUP_W:(k + 1) * GROUP_W] = z
        else:
            ssm_ref[...] = z


def _in_proj(x2, g, w, bsz, seq, tm=512):
    t, d = x2.shape
    n_cols = w.shape[1]
    lt = seq // tm
    return pl.pallas_call(
        _in_proj_kernel,
        grid=(t // tm,),
        in_specs=[
            pl.BlockSpec((tm, d), lambda i: (i, 0)),
            pl.BlockSpec((1, d), lambda i: (0, 0)),
            pl.BlockSpec((d, n_cols), lambda i: (0, 0)),
        ],
        out_specs=[
            pl.BlockSpec((tm, 3 * GROUP_W), lambda i: (i, 0)),
            pl.BlockSpec((tm, 5 * GROUP_W), lambda i: (i, 0)),
            pl.BlockSpec((tm, GROUP_W), lambda i: (i % lt, i // lt)),
        ],
        out_shape=[
            jax.ShapeDtypeStruct((t, 3 * GROUP_W), BF16),
            jax.ShapeDtypeStruct((t, 5 * GROUP_W), F32),
            jax.ShapeDtypeStruct((seq, bsz * GROUP_W), F32),
        ],
        compiler_params=_cparams(("arbitrary",)),
        name="in_proj",
    )(x2, g.reshape(1, d), w)


def _na_bias_table(rpb, rows):
    n_heads = rpb.shape[0]
    kh = NA_WIN_H
    c = np.arange(GRID_W)
    qcs = np.clip(c - NA_WIN_W // 2, 0, GRID_W - NA_WIN_W)
    kc = np.arange(GRID_W)
    in_win = (kc[None, :] >= qcs[:, None]) & (kc[None, :] < qcs[:, None] + NA_WIN_W)
    dc_idx = np.clip(kc[None, :] - c[:, None] + NA_WIN_W - 1, 0, 2 * NA_WIN_W - 2)
    onehot_dc = (dc_idx[:, :, None] == np.arange(2 * NA_WIN_W - 1)).astype(np.float32)
    tab = jnp.einsum("hab,ckb->hcak", rpb.astype(F32), jnp.asarray(onehot_dc),
                     precision=lax.Precision.HIGHEST)
    tab = jnp.where(jnp.asarray(in_win)[None, :, None, :], tab, NEG_INF)
    rep_rows = [0, 1, 2, 3, 4, rows - 3, rows - 2, rows - 1]
    tabs = []
    for r in rep_rows:
        rs = int(np.clip(r - kh // 2, 0, rows - kh))
        dr0 = rs - r + NA_WIN_H - 1
        tabs.append(tab[:, :, dr0:dr0 + kh, :].reshape(n_heads * GRID_W, kh * GRID_W))
    return jnp.stack(tabs)


def _na_kernel(q_ref, k_ref, v_ref, bias_ref, o_ref, *, rows, n_heads):
    kh = NA_WIN_H
    lane = lax.broadcasted_iota(jnp.int32, (GRID_W, n_heads * NA_HEAD_DIM), 1)
    masks = [(lane >= h * NA_HEAD_DIM) & (lane < (h + 1) * NA_HEAD_DIM) for h in range(n_heads)]
    scale = NA_HEAD_DIM ** -0.5

    def body(r, carry):
        q0 = pl.multiple_of(r * GRID_W, GRID_W)
        q = q_ref[pl.ds(q0, GRID_W), :].astype(F32) * scale
        qs = jnp.concatenate([jnp.where(m, q, 0.0) for m in masks], axis=0).astype(BF16)
        rs = jnp.clip(r - kh // 2, 0, rows - kh)
        k0 = pl.multiple_of(rs * GRID_W, GRID_W)
        ks = k_ref[pl.ds(k0, kh * GRID_W), :]
        vs = v_ref[pl.ds(k0, kh * GRID_W), :]
        s = lax.dot_general(qs, ks, (((1,), (1,)), ((), ())), preferred_element_type=F32)
        var = jnp.where(r < kh // 2, r, jnp.where(r > rows - kh // 2, r - (rows - 8), kh // 2))
        s = s + bias_ref[var]
        m = jnp.max(s, axis=-1, keepdims=True)
        p = jnp.exp(s - m)
        den = jnp.sum(p, axis=-1, keepdims=True)
        o = jnp.dot(p.astype(BF16), vs, preferred_element_type=F32) / den
        out = jnp.where(masks[0], o[0:GRID_W], 0.0)
        for h in range(1, n_heads):
            out = out + jnp.where(masks[h], o[h * GRID_W:(h + 1) * GRID_W], 0.0)
        o_ref[pl.ds(q0, GRID_W), :] = out
        return carry

    lax.fori_loop(0, rows, body, 0, unroll=2)


def _na_attention(qkv, bias, bsz, seq):
    rows = seq // GRID_W
    n_heads = GROUP_W // NA_HEAD_DIM
    t = qkv.shape[0]
    blk = lambda j: pl.BlockSpec((seq, GROUP_W), lambda b, j=j: (b, j))
    return pl.pallas_call(
        functools.partial(_na_kernel, rows=rows, n_heads=n_heads),
        grid=(bsz,),
        in_specs=[blk(0), blk(1), blk(2),
                  pl.BlockSpec(bias.shape, lambda b: (0, 0, 0))],
        out_specs=pl.BlockSpec((seq, GROUP_W), lambda b: (b, 0)),
        out_shape=jax.ShapeDtypeStruct((t, GROUP_W), F32),
        compiler_params=_cparams(("arbitrary",)),
        name="na_attention",
    )(qkv, qkv, qkv, bias)


CONV_PAD = 16
CONV_CHUNK = 128


def _conv_kernel(b_ref, c_ref, x_ref, a_ref, g_ref, scw_ref, cfw_ref, cfb_ref, lng_ref, lnb_ref,
                 ysc_ref, ycf_ref, psc, pcf):
    seq = b_ref.shape[0]
    n_chunks = seq // CONV_CHUNK
    zeros = jnp.zeros((CONV_PAD, GROUP_W), F32)
    psc[0:CONV_PAD, :] = zeros
    pcf[0:CONV_PAD, :] = zeros
    psc[CONV_PAD + seq:2 * CONV_PAD + seq, :] = zeros
    pcf[CONV_PAD + seq:2 * CONV_PAD + seq, :] = zeros

    def fill(i, carry):
        r0 = pl.multiple_of(i * CONV_CHUNK, CONV_CHUNK)
        rows = pl.ds(r0, CONV_CHUNK)
        dst = pl.ds(r0 + CONV_PAD, CONV_CHUNK)
        psc[dst, :] = c_ref[rows, :] * x_ref[rows, :]
        pcf[dst, :] = a_ref[rows, :] * jax.nn.sigmoid(g_ref[rows, :])
        return carry

    lax.fori_loop(0, n_chunks, fill, 0)

    def taps(win, w_ref):
        n_taps = w_ref.shape[0]
        offs = [CONV_PAD + k - n_taps // 2 for k in range(n_taps)]
        n_win = win.shape[0]
        acc = None
        for s in range(SUBLANES):
            ks = [k for k in range(n_taps) if offs[k] % SUBLANES == s]
            if not ks:
                continue
            shifted = pltpu.roll(win, n_win - s, axis=0) if s else win
            for k in ks:
                q = offs[k] - s
                term = shifted[q:q + CONV_CHUNK, :] * w_ref[k:k + 1, :]
                acc = term if acc is None else acc + term
        return acc

    def conv(i, carry):
        r0 = pl.multiple_of(i * CONV_CHUNK, CONV_CHUNK)
        rows = pl.ds(r0, CONV_CHUNK)
        window = pl.ds(r0, CONV_CHUNK + 2 * CONV_PAD)
        ysc_ref[rows, :] = b_ref[rows, :] * taps(psc[window, :], scw_ref)
        cf = taps(pcf[window, :], cfw_ref) + cfb_ref[...]
        mu = jnp.mean(cf, axis=-1, keepdims=True)
        xc = cf - mu
        var = jnp.mean(xc * xc, axis=-1, keepdims=True)
        ln = xc * lax.rsqrt(var + EPS) * lng_ref[...] + lnb_ref[...]
        ycf_ref[rows, :] = ln * jax.nn.sigmoid(ln)
        return carry

    lax.fori_loop(0, n_chunks, conv, 0)


def _conv_mixers(rest, sc_w, cf_w, cf_b, ln_g, ln_b, bsz, seq):
    t = rest.shape[0]
    blk = lambda j: pl.BlockSpec((seq, GROUP_W), lambda b, j=j: (b, j))
    full = lambda a: pl.BlockSpec(a.shape, lambda b: (0,) * a.ndim)
    cf_b, ln_g, ln_b = (a.reshape(1, GROUP_W) for a in (cf_b, ln_g, ln_b))
    out_spec = pl.BlockSpec((seq, GROUP_W), lambda b: (b, 0))
    return pl.pallas_call(
        _conv_kernel,
        grid=(bsz,),
        in_specs=[blk(0), blk(1), blk(2), blk(3), blk(4),
                  full(sc_w), full(cf_w), full(cf_b), full(ln_g), full(ln_b)],
        out_specs=[out_spec, out_spec],
        out_shape=[jax.ShapeDtypeStruct((t, GROUP_W), F32)] * 2,
        scratch_shapes=[pltpu.VMEM((seq + 2 * CONV_PAD, GROUP_W), F32)] * 2,
        compiler_params=_cparams(("arbitrary",)),
        name="conv_mixers",
    )(rest, rest, rest, rest, rest, sc_w, cf_w, cf_b, ln_g, ln_b)


S5_TL = 64
S5_LANE_SPLIT = 2


def _s5_discretise(a_re, a_im, log_dt, b_re, b_im, c_re, c_im, bsz):
    f32 = F32
    a_re, a_im, log_dt = a_re.astype(f32), a_im.astype(f32), log_dt.astype(f32)
    n_dir, n_grp, n_state = a_re.shape
    n_ch = b_re.shape[-1]
    dt = jnp.exp(log_dt)[..., None]
    mag = jnp.exp(a_re * dt)
    abr = mag * jnp.cos(a_im * dt)
    abi = mag * jnp.sin(a_im * dt)
    den = a_re * a_re + a_im * a_im
    fr = ((abr - 1.0) * a_re + abi * a_im) / den
    fi = (abi * a_re - (abr - 1.0) * a_im) / den
    bbr = fr[..., None] * b_re - fi[..., None] * b_im
    bbi = fr[..., None] * b_im + fi[..., None] * b_re
    eye = jnp.eye(n_grp, dtype=f32)
    blk_b = lambda m: jnp.einsum("dgps,gh->dgshp", m, eye).reshape(n_dir, n_grp * n_ch, n_grp * n_state)
    blk_c = lambda m: jnp.einsum("dgsp,gh->dgphs", m, eye).reshape(n_dir, n_grp * n_state, n_grp * n_ch)
    bmat = jnp.concatenate([blk_b(bbr), blk_b(bbi)], axis=-1)
    cmat = jnp.concatenate([blk_c(c_re.astype(f32)), blk_c(-c_im.astype(f32))], axis=1)
    amat = jnp.stack([abr.reshape(n_dir, -1), abi.reshape(n_dir, -1)], axis=1)
    amat = jnp.broadcast_to(amat[:, :, None, :], (n_dir, 2, bsz, n_grp * n_state))
    return bmat.astype(BF16), cmat.astype(BF16), amat


def _s5_scan_kernel(u_ref, b_ref, c_ref, a_ref, y_ref, xs, st, *, bsz):
    d = pl.program_id(0)
    i = pl.program_id(1)
    n_state = a_ref.shape[-1]

    @pl.when(i == 0)
    def _():
        st[...] = jnp.zeros_like(st)

    xs[...] = jnp.dot(u_ref[...].astype(BF16), b_ref[...], preferred_element_type=F32)

    cw = n_state // S5_LANE_SPLIT
    for c in range(S5_LANE_SPLIT):
        re = slice(c * cw, (c + 1) * cw)
        im = slice(n_state + c * cw, n_state + (c + 1) * cw)
        a_r = a_ref[0, :, re]
        a_i = a_ref[1, :, re]

        def step(t, carry, re=re, im=im, a_r=a_r, a_i=a_i):
            s_r, s_i = carry
            tt = t + d * (S5_TL - 1 - 2 * t)
            rows = pl.ds(pl.multiple_of(tt * bsz, bsz), bsz)
            n_r = a_r * s_r - a_i * s_i + xs[rows, re]
            n_i = a_r * s_i + a_i * s_r + xs[rows, im]
            xs[rows, re] = n_r
            xs[rows, im] = n_i
            return n_r, n_i

        s_r, s_i = lax.fori_loop(0, S5_TL, step, (st[:, re], st[:, im]), unroll=4)
        st[:, re] = s_r
        st[:, im] = s_i

    y_ref[...] = jnp.dot(xs[...].astype(BF16), c_ref[...], preferred_element_type=F32)


def _s5_scan(u_t, bmat, cmat, amat, bsz, seq):
    n_chunks = seq // S5_TL
    rows = S5_TL * bsz
    n_in = u_t.shape[1]
    n_st2 = bmat.shape[-1]
    chunk = lambda d, i: i + d * (n_chunks - 1 - 2 * i)
    return pl.pallas_call(
        functools.partial(_s5_scan_kernel, bsz=bsz),
        grid=(2, n_chunks),
        in_specs=[
            pl.BlockSpec((rows, n_in), lambda d, i: (chunk(d, i), 0)),
            pl.BlockSpec((None, n_in, n_st2), lambda d, i: (d, 0, 0)),
            pl.BlockSpec((None, n_st2, n_in), lambda d, i: (d, 0, 0)),
            pl.BlockSpec((None, 2, bsz, n_st2 // 2), lambda d, i: (d, 0, 0, 0)),
        ],
        out_specs=pl.BlockSpec((None, rows, n_in), lambda d, i: (d, chunk(d, i), 0)),
        out_shape=jax.ShapeDtypeStruct((2, seq * bsz, n_in), F32),
        scratch_shapes=[pltpu.VMEM((rows, n_st2), F32), pltpu.VMEM((bsz, n_st2), F32)],
        compiler_params=_cparams(("arbitrary", "arbitrary")),
        name="s5_scan",
    )(u_t, bmat, cmat, amat)


def _s5_post_kernel(u_ref, yf_ref, yb_ref, d_ref, w_ref, b_ref, o_ref):
    y = d_ref[...] * u_ref[...] + yf_ref[...] + yb_ref[...]
    y = jax.nn.gelu(y)
    gate = jnp.dot(y.astype(BF16), w_ref[...], preferred_element_type=F32) + b_ref[...]
    o_ref[...] = y * jax.nn.sigmoid(gate)


def _s5_post(u_lb, y_dirs, d_skip, w_glu, b_glu, bsz, seq, tl=512):
    lt = seq // tl
    return pl.pallas_call(
        _s5_post_kernel,
        grid=(bsz, lt),
        in_specs=[
            pl.BlockSpec((tl, GROUP_W), lambda b, l: (l, b)),
            pl.BlockSpec((None, tl, GROUP_W), lambda b, l: (0, l, b)),
            pl.BlockSpec((None, tl, GROUP_W), lambda b, l: (1, l, b)),
            pl.BlockSpec((1, GROUP_W), lambda b, l: (0, 0)),
            pl.BlockSpec((GROUP_W, GROUP_W), lambda b, l: (0, 0)),
            pl.BlockSpec((1, GROUP_W), lambda b, l: (0, 0)),
        ],
        out_specs=pl.BlockSpec((tl, GROUP_W), lambda b, l: (b * lt + l, 0)),
        out_shape=jax.ShapeDtypeStruct((bsz * seq, GROUP_W), F32),
        compiler_params=_cparams(("arbitrary", "arbitrary")),
        name="s5_post",
    )(u_lb, y_dirs, y_dirs, d_skip.reshape(1, GROUP_W), w_glu, b_glu.reshape(1, GROUP_W))


def _out_proj_kernel(y0_ref, y1_ref, y2_ref, y3_ref, gn_ref, w_ref, x_ref, o_ref, *, token_tiles):
    acc = x_ref[...]
    for j, y_ref in enumerate((y0_ref, y1_ref, y2_ref, y3_ref)):
        yn = _rms(y_ref[...], gn_ref[j:j + 1, :]).astype(BF16)
        acc = acc + jnp.dot(yn, w_ref[j * GROUP_W:(j + 1) * GROUP_W, :], preferred_element_type=F32)
    if token_tiles:
        _store_token_tiles(o_ref, acc)
    else:
        o_ref[...] = acc


def _out_proj(ys, gn, w, x2, token_tiles, tm=512):
    t, d = x2.shape
    yspec = pl.BlockSpec((tm, GROUP_W), lambda i: (i, 0))
    out_block, out_rows = ((tm * SUBLANES, LANES), t * SUBLANES) if token_tiles else ((tm, d), t)
    return pl.pallas_call(
        functools.partial(_out_proj_kernel, token_tiles=token_tiles),
        grid=(t // tm,),
        in_specs=[yspec] * 4 + [
            pl.BlockSpec((4, GROUP_W), lambda i: (0, 0)),
            pl.BlockSpec(w.shape, lambda i: (0, 0)),
            pl.BlockSpec((tm, d), lambda i: (i, 0)),
        ],
        out_specs=pl.BlockSpec(out_block, lambda i: (i, 0)),
        out_shape=jax.ShapeDtypeStruct((out_rows, out_block[1]), F32),
        compiler_params=_cparams(("arbitrary",)),
        name="out_proj",
    )(*ys, gn.reshape(4, GROUP_W), w, x2)


def _ffn_kernel(x_ref, g_ref, wg_ref, wu_ref, wd_ref, o_ref, h_ref, acc_ref):
    f = pl.program_id(1)

    @pl.when(f == 0)
    def _():
        h_ref[...] = _rms(x_ref[...], g_ref[...]).astype(BF16)
        acc_ref[...] = x_ref[...]

    h = h_ref[...]
    gate = jnp.dot(h, wg_ref[...], preferred_element_type=F32)
    up = jnp.dot(h, wu_ref[...], preferred_element_type=F32)
    act = (gate * jax.nn.sigmoid(gate) * up).astype(BF16)
    acc_ref[...] += jnp.dot(act, wd_ref[...], preferred_element_type=F32)

    @pl.when(f == pl.num_programs(1) - 1)
    def _():
        o_ref[...] = acc_ref[...]


def _dense_ffn(x2, g, wg, wu, wd, tm=512, tf=1408):
    t, d = x2.shape
    dff = wg.shape[1]
    return pl.pallas_call(
        _ffn_kernel,
        grid=(t // tm, dff // tf),
        in_specs=[
            pl.BlockSpec((tm, d), lambda i, f: (i, 0)),
            pl.BlockSpec((1, d), lambda i, f: (0, 0)),
            pl.BlockSpec((d, tf), lambda i, f: (0, f)),
            pl.BlockSpec((d, tf), lambda i, f: (0, f)),
            pl.BlockSpec((tf, d), lambda i, f: (f, 0)),
        ],
        out_specs=pl.BlockSpec((tm, d), lambda i, f: (i, 0)),
        out_shape=jax.ShapeDtypeStruct((t, d), F32),
        scratch_shapes=[pltpu.VMEM((tm, d), BF16), pltpu.VMEM((tm, d), F32)],
        compiler_params=_cparams(("arbitrary", "arbitrary")),
        name="dense_ffn",
    )(x2, g.reshape(1, d), wg, wu, wd)


ROUTE_COLS = 8
ROUTER_LANES = 128


def _router_kernel(x_ref, g_ref, wr_ref, route_ref):
    h = _rms(_load_token_tiles(x_ref, route_ref.shape[0]), g_ref[...])
    logits = jnp.dot(h, wr_ref[...], preferred_element_type=F32, precision=lax.Precision.HIGHEST)
    lane = lax.broadcasted_iota(jnp.int32, logits.shape, 1)
    minus_inf = -jnp.inf
    l1 = jnp.where(lane < N_EXPERTS, logits, minus_inf)
    m1 = jnp.max(l1, axis=-1, keepdims=True)
    i1 = jnp.min(jnp.where(l1 == m1, lane, ROUTER_LANES), axis=-1, keepdims=True)
    l2 = jnp.where(lane == i1, minus_inf, l1)
    m2 = jnp.max(l2, axis=-1, keepdims=True)
    i2 = jnp.min(jnp.where(l2 == m2, lane, ROUTER_LANES), axis=-1, keepdims=True)
    e2 = jnp.exp(m2 - m1)
    den = 1.0 + e2
    rec = jnp.where(lane == 0, i1.astype(F32),
                    jnp.where(lane == 1, i2.astype(F32),
                              jnp.where(lane == 2, 1.0 / den,
                                        jnp.where(lane == 3, e2 / den, 0.0))))
    route_ref[...] = rec[:, :ROUTE_COLS]


def _router(xt, g, w_router, tm=512):
    t = xt.shape[0] // SUBLANES
    d = SUBLANES * LANES
    wr = jnp.zeros((d, ROUTER_LANES), F32).at[:, :N_EXPERTS].set(w_router.astype(F32))
    return pl.pallas_call(
        _router_kernel,
        grid=(t // tm,),
        in_specs=[
            pl.BlockSpec((tm * SUBLANES, LANES), lambda i: (i, 0)),
            pl.BlockSpec((1, d), lambda i: (0, 0)),
            pl.BlockSpec((d, ROUTER_LANES), lambda i: (0, 0)),
        ],
        out_specs=pl.BlockSpec((tm, ROUTE_COLS), lambda i: (i, 0)),
        out_shape=jax.ShapeDtypeStruct((t, ROUTE_COLS), F32),
        compiler_params=_cparams(("arbitrary",)),
        name="moe_router",
    )(xt, g.reshape(1, d), wr)


def _moe_plan(route, tm, n_tiles):
    experts = route[:, :2].astype(jnp.int32).reshape(-1)
    onehot = (experts[:, None] == jnp.arange(N_EXPERTS, dtype=jnp.int32)[None, :]).astype(jnp.int32)
    csum = jnp.cumsum(onehot, axis=0)
    rank = jnp.sum((csum - onehot) * onehot, axis=1)
    count = csum[-1]
    padded = ((count + tm - 1) // tm) * tm
    end = jnp.cumsum(padded)
    start = end - padded
    slot = jnp.sum(onehot * start[None, :], axis=1) + rank
    n_active = end[-1] // tm
    tile_start = jnp.arange(n_tiles, dtype=jnp.int32) * tm
    tile_expert = jnp.sum((tile_start[:, None] >= end[None, :]).astype(jnp.int32), axis=1)
    tile_expert = jnp.minimum(tile_expert, N_EXPERTS - 1)
    last = tile_expert[jnp.maximum(n_active - 1, 0)]
    tile_expert = jnp.where(jnp.arange(n_tiles) < n_active, tile_expert, last)
    return slot.astype(jnp.int32), tile_expert.astype(jnp.int32), n_active.astype(jnp.int32).reshape(1)


def _token_copy(src, src_tok, dst, dst_tok, sem):
    rows = lambda tok: pl.ds(pl.multiple_of(tok * SUBLANES, SUBLANES), SUBLANES)
    return pltpu.make_async_copy(src.at[rows(src_tok)], dst.at[rows(dst_tok)], sem)


def _invert_kernel(slot_ref, plan_ref, *, tm, n_tok, src_bits):
    def clear(s, carry):
        plan_ref[s] = (2 * n_tok + (s & (2 * tm - 1))) << src_bits
        return carry

    lax.fori_loop(0, plan_ref.shape[0], clear, 0, unroll=8)

    def place(a, carry):
        tok = a >> 1
        plan_ref[slot_ref[a]] = tok + (((a & 1) * n_tok + tok) << src_bits)
        return carry

    lax.fori_loop(0, slot_ref.shape[0], place, 0, unroll=8)


def _plan_bits(n_tok, tm):
    src_bits = (n_tok - 1).bit_length()
    assert (2 * n_tok + 2 * tm) << src_bits < 2 ** 31 and tm & (tm - 1) == 0
    return src_bits


def _invert(slot, n_slots, tm, n_tok):
    smem = pl.BlockSpec(memory_space=pltpu.SMEM)
    return pl.pallas_call(
        functools.partial(_invert_kernel, tm=tm, n_tok=n_tok, src_bits=_plan_bits(n_tok, tm)),
        in_specs=[smem],
        out_specs=smem,
        out_shape=jax.ShapeDtypeStruct((n_slots,), jnp.int32),
        name="moe_invert",
    )(slot)


def _tokens_done(ref, n_tok, sem):
    rows = pl.ds(0, n_tok * SUBLANES)
    pltpu.make_async_copy(ref.at[rows], ref.at[rows], sem).wait()


def _moe_ffn_kernel(te_ref, nact_ref, plan_ref, x_hbm, g_ref, wg_ref, wu_ref, wd_ref, yk_hbm,
                    xbuf, xb16, ybuf, acc_ref, sem_in, sem_out, *, tm, n_tok, n_f, pad_tile, src_bits):
    del te_ref
    i = pl.program_id(0)
    f = pl.program_id(1)
    par = i % 2
    nact = nact_ref[0]
    per_step = tm // n_f

    def fetch(tile, p, j):
        src = plan_ref[tile * tm + j] & ((1 << src_bits) - 1)
        _token_copy(x_hbm, src, xbuf.at[p], j, sem_in.at[p]).start()

    def send(tile, p, j):
        dst = plan_ref[tile * tm + j] >> src_bits
        _token_copy(ybuf.at[p], j, yk_hbm, dst, sem_out.at[p]).start()

    @pl.when((i == 0) & (f == 0))
    def _():
        def first(j, carry):
            fetch(0, 0, j)
            return carry
        lax.fori_loop(0, tm, first, 0)
        ybuf[...] = jnp.zeros_like(ybuf)
        for p in range(2):
            spare_rows = pl.ds((2 * n_tok + p * tm) * SUBLANES, tm * SUBLANES)
            spare = pltpu.make_async_copy(ybuf.at[p], yk_hbm.at[spare_rows], sem_out.at[p])
            spare.start()
            spare.wait()
        _tokens_done(xbuf.at[0], tm, sem_in.at[0])

    @pl.when((i >= 1) & (i <= nact) & (f == 0))
    def _():
        _tokens_done(xbuf.at[par], tm, sem_in.at[par])

    @pl.when(i < nact)
    def _():
        @pl.when(f == 0)
        def _():
            xb16[...] = _rms(_load_token_tiles(xbuf.at[par], tm), g_ref[...]).astype(BF16)
            acc_ref[...] = jnp.zeros_like(acc_ref)

        prev = jnp.where(i > 0, i - 1, pad_tile)
        for jj in range(per_step):
            j = f * per_step + jj
            fetch(i + 1, 1 - par, j)
            send(prev, 1 - par, j)

        x = xb16[...]
        gate = jnp.dot(x, wg_ref[...], preferred_element_type=F32)
        up = jnp.dot(x, wu_ref[...], preferred_element_type=F32)
        act = (gate * jax.nn.sigmoid(gate) * up).astype(BF16)
        acc_ref[...] += jnp.dot(act, wd_ref[...], preferred_element_type=F32)

        @pl.when(f == n_f - 1)
        def _():
            @pl.when(i >= 1)
            def _():
                _tokens_done(ybuf.at[par], tm, sem_out.at[par])

            _store_token_tiles(ybuf.at[par], acc_ref[...])

    @pl.when((i == nact) & (f == 0))
    def _():
        def last(j, carry):
            send(i - 1, 1 - par, j)
            return carry
        lax.fori_loop(0, tm, last, 0)
        _tokens_done(ybuf.at[0], tm, sem_out.at[0])
        _tokens_done(ybuf.at[1], tm, sem_out.at[1])


def _moe_ffn(xt, g, plan, tile_expert, n_active, wg, wu, wd, tm):
    n_tok = xt.shape[0] // SUBLANES
    d = SUBLANES * LANES
    n_tiles = tile_expert.shape[0]
    pad_tile = n_tiles - 1 if (n_tiles - 1) % 2 else n_tiles
    assert plan.shape[0] >= (pad_tile + 1) * tm
    dff = wg.shape[-1]
    n_f = 4
    tf = dff // n_f
    fidx = lambda i, f, na: jnp.where(i < na[0], f, n_f - 1)
    grid_spec = pltpu.PrefetchScalarGridSpec(
        num_scalar_prefetch=3,
        grid=(n_tiles, n_f),
        in_specs=[
            pl.BlockSpec(memory_space=pl.ANY),
            pl.BlockSpec((1, d), lambda i, f, te, na, plan: (0, 0)),
            pl.BlockSpec((None, d, tf), lambda i, f, te, na, plan: (te[i], 0, fidx(i, f, na))),
            pl.BlockSpec((None, d, tf), lambda i, f, te, na, plan: (te[i], 0, fidx(i, f, na))),
            pl.BlockSpec((None, tf, d), lambda i, f, te, na, plan: (te[i], fidx(i, f, na), 0)),
        ],
        out_specs=pl.BlockSpec(memory_space=pl.ANY),
        scratch_shapes=[
            pltpu.VMEM((2, tm * SUBLANES, LANES), F32),
            pltpu.VMEM((tm, d), BF16),
            pltpu.VMEM((2, tm * SUBLANES, LANES), F32),
            pltpu.VMEM((tm, d), F32),
            pltpu.SemaphoreType.DMA((2,)),
            pltpu.SemaphoreType.DMA((2,)),
        ],
    )
    return pl.pallas_call(
        functools.partial(_moe_ffn_kernel, tm=tm, n_tok=n_tok, n_f=n_f, pad_tile=pad_tile,
                          src_bits=_plan_bits(n_tok, tm)),
        grid_spec=grid_spec,
        out_shape=jax.ShapeDtypeStruct(((2 * n_tok + 2 * tm) * SUBLANES, LANES), F32),
        compiler_params=_cparams(("arbitrary", "arbitrary")),
        name="moe_ffn",
    )(tile_expert, n_active, plan, xt, g.reshape(1, d), wg, wu, wd)


def _combine_kernel(route_ref, x_ref, y0_ref, y1_ref, fg_ref, o_ref, *, final_norm):
    r = route_ref[...]
    tm = r.shape[0]
    y = (_load_token_tiles(x_ref, tm) + r[:, 2:3] * _load_token_tiles(y0_ref, tm)
         + r[:, 3:4] * _load_token_tiles(y1_ref, tm))
    o_ref[...] = _rms(y, fg_ref[...]) if final_norm else y


def _combine(yk, route, xt, final_g, final_norm, tm=512):
    t = xt.shape[0] // SUBLANES
    d = SUBLANES * LANES
    n_blk = t // tm
    tiles = lambda f: pl.BlockSpec((tm * SUBLANES, LANES), f)
    return pl.pallas_call(
        functools.partial(_combine_kernel, final_norm=final_norm),
        grid=(n_blk,),
        in_specs=[
            pl.BlockSpec((tm, ROUTE_COLS), lambda i: (i, 0)),
            tiles(lambda i: (i, 0)),
            tiles(lambda i: (i, 0)),
            tiles(lambda i: (n_blk + i, 0)),
            pl.BlockSpec((1, d), lambda i: (0, 0)),
        ],
        out_specs=pl.BlockSpec((tm, d), lambda i: (i, 0)),
        out_shape=jax.ShapeDtypeStruct((t, d), F32),
        compiler_params=_cparams(("arbitrary",)),
        name="moe_combine",
    )(route, xt, yk, yk, final_g.reshape(1, d))


def _final_norm_kernel(x_ref, g_ref, o_ref):
    o_ref[...] = _rms(x_ref[...], g_ref[...])


def _final_norm(x2, g, tm=512):
    t, d = x2.shape
    return pl.pallas_call(
        _final_norm_kernel,
        grid=(t // tm,),
        in_specs=[pl.BlockSpec((tm, d), lambda i: (i, 0)), pl.BlockSpec((1, d), lambda i: (0, 0))],
        out_specs=pl.BlockSpec((tm, d), lambda i: (i, 0)),
        out_shape=jax.ShapeDtypeStruct((t, d), F32),
        compiler_params=_cparams(("arbitrary",)),
        name="final_norm",
    )(x2, g.reshape(1, d))


MOE_TM = 512


def kernel(x, norm1_g, w_in, na_rpb, sc_conv_w, cf_conv_w, cf_conv_b, cf_ln_g, cf_ln_b, ssm_a_re, ssm_a_im,
           ssm_log_dt, ssm_b_re, ssm_b_im, ssm_c_re, ssm_c_im, ssm_d, ssm_w_glu, ssm_b_glu, grp_norm_g, w_out,
           norm2_g, ffn_w_gate, ffn_w_up, ffn_w_down, moe_w_router, moe_w_gate, moe_w_up, moe_w_down,
           final_norm_g):
    bsz, seq, d = x.shape
    depth = w_in.shape[0]
    t = bsz * seq
    rows = seq // GRID_W
    x2 = x.reshape(t, d).astype(F32)
    final_done = False
    for i in range(depth):
        qkv, rest, ssm_lb = _in_proj(x2, norm1_g[i], w_in[i].astype(BF16), bsz, seq)
        y_na = _na_attention(qkv, _na_bias_table(na_rpb[i], rows), bsz, seq)
        y_sc, y_cf = _conv_mixers(rest, sc_conv_w[i], cf_conv_w[i], cf_conv_b[i], cf_ln_g[i], cf_ln_b[i],
                                  bsz, seq)
        bmat, cmat, amat = _s5_discretise(ssm_a_re[i], ssm_a_im[i], ssm_log_dt[i], ssm_b_re[i], ssm_b_im[i],
                                          ssm_c_re[i], ssm_c_im[i], bsz)
        y_dirs = _s5_scan(ssm_lb.reshape(seq * bsz, GROUP_W), bmat, cmat, amat, bsz, seq)
        y_ssm = _s5_post(ssm_lb, y_dirs.reshape(2, seq, bsz * GROUP_W), ssm_d[i],
                         ssm_w_glu[i].astype(BF16), ssm_b_glu[i], bsz, seq)
        x2 = _out_proj((y_na, y_sc, y_cf, y_ssm), grp_norm_g[i], w_out[i].astype(BF16), x2,
                       token_tiles=(i % 2 == 1))
        j = i // 2
        if i % 2 == 0:
            x2 = _dense_ffn(x2, norm2_g[i], ffn_w_gate[j].astype(BF16), ffn_w_up[j].astype(BF16),
                            ffn_w_down[j].astype(BF16))
        else:
            route = _router(x2, norm2_g[i], moe_w_router[j])
            n_tiles = (2 * t) // MOE_TM + N_EXPERTS + 1
            slot, tile_expert, n_active = _moe_plan(route, MOE_TM, n_tiles)
            plan = _invert(slot, (n_tiles + 1) * MOE_TM, MOE_TM, t)
            yk = _moe_ffn(x2, norm2_g[i], plan, tile_expert, n_active, moe_w_gate[j].astype(BF16),
                          moe_w_up[j].astype(BF16), moe_w_down[j].astype(BF16), MOE_TM)
            final_done = i == depth - 1
            x2 = _combine(yk, route, x2, final_norm_g, final_done)
    if not final_done:
        x2 = _final_norm(x2, final_norm_g)
    return x2.reshape(bsz, seq, d)
```

```python
import functools
import math

import jax
import jax.numpy as jnp
import numpy as np
from jax import lax
from jax.experimental import pallas as pl
from jax.experimental.pallas import tpu as pltpu

F32 = jnp.float32
BF16 = jnp.bfloat16
EPS = 1e-6
NEG_INF = -1e30

GRID_W = 64
NA_HEAD_DIM = 64
NA_WIN_H = 8
NA_WIN_W = 16
GROUP_W = 256
SSM_GROUP_CH = 16
SSM_STATE = 64
N_EXPERTS = 8
VMEM_LIMIT = 56 * 1024 * 1024


def _cparams(sem, vmem=VMEM_LIMIT):
    return pltpu.CompilerParams(dimension_semantics=sem, vmem_limit_bytes=vmem)


def _rms(x, g):
    return x * lax.rsqrt(jnp.mean(x * x, axis=-1, keepdims=True) + EPS) * g


SUBLANES = 8
LANES = 128


def _load_token_tiles(ref, n):
    return jnp.concatenate([ref[pl.ds(s, n, stride=SUBLANES), :] for s in range(SUBLANES)], axis=1)


def _store_token_tiles(ref, val):
    for s in range(SUBLANES):
        ref[pl.ds(s, val.shape[0], stride=SUBLANES), :] = val[:, s * LANES:(s + 1) * LANES]


N_QKV_COLS = 3 * GROUP_W
SSM_COL = 5


def _in_proj_kernel(x_ref, g_ref, w_ref, qkv_ref, rest_ref):
    h = _rms(x_ref[...], g_ref[...]).astype(BF16)
    n_qkv = qkv_ref.shape[1] // GROUP_W
    n_rest = rest_ref.shape[1] // GROUP_W
    for j in range(n_qkv + n_rest):
        z = jnp.dot(h, w_ref[:, j * GROUP_W:(j + 1) * GROUP_W], preferred_element_type=F32)
        if j < n_qkv:
            qkv_ref[:, j * GROUP_W:(j + 1) * GROUP_W] = z.astype(BF16)
        else:
            k = j - n_qkv
            rest_ref[:, k * GROUP_W:(k + 1) * GROUP_W] = z


def _in_proj(x2, g, w, tm=512):
    t, d = x2.shape
    n_cols = w.shape[1]
    n_rest = n_cols - N_QKV_COLS
    return pl.pallas_call(
        _in_proj_kernel,
        grid=(t // tm,),
        in_specs=[
            pl.BlockSpec((tm, d), lambda i: (i, 0)),
            pl.BlockSpec((1, d), lambda i: (0, 0)),
            pl.BlockSpec((d, n_cols), lambda i: (0, 0)),
        ],
        out_specs=[
            pl.BlockSpec((tm, N_QKV_COLS), lambda i: (i, 0)),
            pl.BlockSpec((tm, n_rest), lambda i: (i, 0)),
        ],
        out_shape=[
            jax.ShapeDtypeStruct((t, N_QKV_COLS), BF16),
            jax.ShapeDtypeStruct((t, n_rest), F32),
        ],
        compiler_params=_cparams(("arbitrary",)),
        name="in_proj",
    )(x2, g.reshape(1, d), w)


def _na_bias_table(rpb, rows):
    n_heads = rpb.shape[0]
    kh = NA_WIN_H
    c = np.arange(GRID_W)
    qcs = np.clip(c - NA_WIN_W // 2, 0, GRID_W - NA_WIN_W)
    kc = np.arange(GRID_W)
    in_win = (kc[None, :] >= qcs[:, None]) & (kc[None, :] < qcs[:, None] + NA_WIN_W)
    dc_idx = np.clip(kc[None, :] - c[:, None] + NA_WIN_W - 1, 0, 2 * NA_WIN_W - 2)
    onehot_dc = (dc_idx[:, :, None] == np.arange(2 * NA_WIN_W - 1)).astype(np.float32)
    tab = jnp.einsum("hab,ckb->hcak", rpb.astype(F32), jnp.asarray(onehot_dc),
                     precision=lax.Precision.HIGHEST)
    tab = jnp.where(jnp.asarray(in_win)[None, :, None, :], tab, NEG_INF)
    rep_rows = [0, 1, 2, 3, 4, rows - 3, rows - 2, rows - 1]
    tabs = []
    for r in rep_rows:
        rs = int(np.clip(r - kh // 2, 0, rows - kh))
        dr0 = rs - r + NA_WIN_H - 1
        tabs.append(tab[:, :, dr0:dr0 + kh, :].reshape(n_heads * GRID_W, kh * GRID_W))
    return jnp.stack(tabs)


def _na_kernel(q_ref, k_ref, v_ref, bias_ref, o_ref, *, rows, n_heads):
    kh = NA_WIN_H
    lane = lax.broadcasted_iota(jnp.int32, (GRID_W, n_heads * NA_HEAD_DIM), 1)
    masks = [(lane >= h * NA_HEAD_DIM) & (lane < (h + 1) * NA_HEAD_DIM) for h in range(n_heads)]
    scale = NA_HEAD_DIM ** -0.5

    def body(r, carry):
        q0 = pl.multiple_of(r * GRID_W, GRID_W)
        q = q_ref[pl.ds(q0, GRID_W), :].astype(F32) * scale
        qs = jnp.concatenate([jnp.where(m, q, 0.0) for m in masks], axis=0).astype(BF16)
        rs = jnp.clip(r - kh // 2, 0, rows - kh)
        k0 = pl.multiple_of(rs * GRID_W, GRID_W)
        ks = k_ref[pl.ds(k0, kh * GRID_W), :]
        vs = v_ref[pl.ds(k0, kh * GRID_W), :]
        s = lax.dot_general(qs, ks, (((1,), (1,)), ((), ())), preferred_element_type=F32)
        var = jnp.where(r < kh // 2, r, jnp.where(r > rows - kh // 2, r - (rows - 8), kh // 2))
        s = s + bias_ref[var]
        m = jnp.max(s, axis=-1, keepdims=True)
        p = jnp.exp(s - m)
        den = jnp.sum(p, axis=-1, keepdims=True)
        o = jnp.dot(p.astype(BF16), vs, preferred_element_type=F32) / den
        out = jnp.where(masks[0], o[0:GRID_W], 0.0)
        for h in range(1, n_heads):
            out = out + jnp.where(masks[h], o[h * GRID_W:(h + 1) * GRID_W], 0.0)
        o_ref[pl.ds(q0, GRID_W), :] = out
        return carry

    lax.fori_loop(0, rows, body, 0, unroll=2)


def _na_attention(qkv, bias, bsz, seq):
    rows = seq // GRID_W
    n_heads = GROUP_W // NA_HEAD_DIM
    t = qkv.shape[0]
    blk = lambda j: pl.BlockSpec((seq, GROUP_W), lambda b, j=j: (b, j))
    return pl.pallas_call(
        functools.partial(_na_kernel, rows=rows, n_heads=n_heads),
        grid=(bsz,),
        in_specs=[blk(0), blk(1), blk(2),
                  pl.BlockSpec(bias.shape, lambda b: (0, 0, 0))],
        out_specs=pl.BlockSpec((seq, GROUP_W), lambda b: (b, 0)),
        out_shape=jax.ShapeDtypeStruct((t, GROUP_W), F32),
        compiler_params=_cparams(("arbitrary",)),
        name="na_attention",
    )(qkv, qkv, qkv, bias)


CONV_PAD = 16
CONV_CHUNK = 128


def _conv_kernel(b_ref, c_ref, x_ref, a_ref, g_ref, scw_ref, cfw_ref, cfb_ref, lng_ref, lnb_ref,
                 ysc_ref, ycf_ref, psc, pcf):
    seq = b_ref.shape[0]
    n_chunks = seq // CONV_CHUNK
    zeros = jnp.zeros((CONV_PAD, GROUP_W), F32)
    psc[0:CONV_PAD, :] = zeros
    pcf[0:CONV_PAD, :] = zeros
    psc[CONV_PAD + seq:2 * CONV_PAD + seq, :] = zeros
    pcf[CONV_PAD + seq:2 * CONV_PAD + seq, :] = zeros

    def fill(i, carry):
        r0 = pl.multiple_of(i * CONV_CHUNK, CONV_CHUNK)
        rows = pl.ds(r0, CONV_CHUNK)
        dst = pl.ds(r0 + CONV_PAD, CONV_CHUNK)
        psc[dst, :] = c_ref[rows, :] * x_ref[rows, :]
        pcf[dst, :] = a_ref[rows, :] * jax.nn.sigmoid(g_ref[rows, :])
        return carry

    lax.fori_loop(0, n_chunks, fill, 0)

    def taps(win, w_ref):
        n_taps = w_ref.shape[0]
        offs = [CONV_PAD + k - n_taps // 2 for k in range(n_taps)]
        n_win = win.shape[0]
        acc = None
        for s in range(SUBLANES):
            ks = [k for k in range(n_taps) if offs[k] % SUBLANES == s]
            if not ks:
                continue
            shifted = pltpu.roll(win, n_win - s, axis=0) if s else win
            for k in ks:
                q = offs[k] - s
                term = shifted[q:q + CONV_CHUNK, :] * w_ref[k:k + 1, :]
                acc = term if acc is None else acc + term
        return acc

    def conv(i, carry):
        r0 = pl.multiple_of(i * CONV_CHUNK, CONV_CHUNK)
        rows = pl.ds(r0, CONV_CHUNK)
        window = pl.ds(r0, CONV_CHUNK + 2 * CONV_PAD)
        ysc_ref[rows, :] = b_ref[rows, :] * taps(psc[window, :], scw_ref)
        cf = taps(pcf[window, :], cfw_ref) + cfb_ref[...]
        mu = jnp.mean(cf, axis=-1, keepdims=True)
        xc = cf - mu
        var = jnp.mean(xc * xc, axis=-1, keepdims=True)
        ln = xc * lax.rsqrt(var + EPS) * lng_ref[...] + lnb_ref[...]
        ycf_ref[rows, :] = ln * jax.nn.sigmoid(ln)
        return carry

    lax.fori_loop(0, n_chunks, conv, 0)


def _conv_mixers(rest, sc_w, cf_w, cf_b, ln_g, ln_b, bsz, seq):
    t = rest.shape[0]
    blk = lambda j: pl.BlockSpec((seq, GROUP_W), lambda b, j=j: (b, j))
    full = lambda a: pl.BlockSpec(a.shape, lambda b: (0,) * a.ndim)
    cf_b, ln_g, ln_b = (a.reshape(1, GROUP_W) for a in (cf_b, ln_g, ln_b))
    out_spec = pl.BlockSpec((seq, GROUP_W), lambda b: (b, 0))
    return pl.pallas_call(
        _conv_kernel,
        grid=(bsz,),
        in_specs=[blk(0), blk(1), blk(2), blk(3), blk(4),
                  full(sc_w), full(cf_w), full(cf_b), full(ln_g), full(ln_b)],
        out_specs=[out_spec, out_spec],
        out_shape=[jax.ShapeDtypeStruct((t, GROUP_W), F32)] * 2,
        scratch_shapes=[pltpu.VMEM((seq + 2 * CONV_PAD, GROUP_W), F32)] * 2,
        compiler_params=_cparams(("arbitrary",)),
        name="conv_mixers",
    )(rest, rest, rest, rest, rest, sc_w, cf_w, cf_b, ln_g, ln_b)


S5_TL = 128
S5_LANE_SPLIT = 2


def _s5_discretise(a_re, a_im, log_dt, b_re, b_im, c_re, c_im, bsz):
    f32 = F32
    a_re, a_im, log_dt = a_re.astype(f32), a_im.astype(f32), log_dt.astype(f32)
    n_dir, n_grp, n_state = a_re.shape
    n_ch = b_re.shape[-1]
    dt = jnp.exp(log_dt)[..., None]
    mag = jnp.exp(a_re * dt)
    abr = mag * jnp.cos(a_im * dt)
    abi = mag * jnp.sin(a_im * dt)
    den = a_re * a_re + a_im * a_im
    fr = ((abr - 1.0) * a_re + abi * a_im) / den
    fi = (abi * a_re - (abr - 1.0) * a_im) / den
    bbr = fr[..., None] * b_re - fi[..., None] * b_im
    bbi = fr[..., None] * b_im + fi[..., None] * b_re
    eye = jnp.eye(n_grp, dtype=f32)
    blk_b = lambda m: jnp.einsum("dgps,gh->dgshp", m, eye).reshape(n_dir, n_grp * n_ch, n_grp * n_state)
    blk_c = lambda m: jnp.einsum("dgsp,gh->dgphs", m, eye).reshape(n_dir, n_grp * n_state, n_grp * n_ch)
    bmat = jnp.concatenate([blk_b(bbr), blk_b(bbi)], axis=-1)
    cmat = jnp.concatenate([blk_c(c_re.astype(f32)), blk_c(-c_im.astype(f32))], axis=1)
    amat = jnp.stack([abr.reshape(n_dir, -1), abi.reshape(n_dir, -1)], axis=1)
    amat = jnp.broadcast_to(amat[:, :, None, :], (n_dir, 2, bsz, n_grp * n_state))
    return bmat.astype(BF16), cmat.astype(BF16), amat


def _s5_kernel(*refs, bsz, backward):
    if backward:
        (u_ref, b_ref, c_ref, a_ref, yf_ref, d_ref, w_ref, bg_ref, o_ref, xs, st, tb) = refs
    else:
        (u_ref, b_ref, c_ref, a_ref, o_ref, xs, st, tb) = refs
    n_state = a_ref.shape[-1]

    @pl.when(pl.program_id(0) == 0)
    def _():
        st[...] = jnp.zeros_like(st)

    n_half = GROUP_W // LANES
    for b in range(bsz):
        for h in range(n_half):
            tb[h, pl.ds(b, S5_TL, stride=bsz), :] = u_ref[b, :, h * LANES:(h + 1) * LANES]
    u_tb = jnp.concatenate([tb[h] for h in range(n_half)], axis=1)
    xs[...] = jnp.dot(u_tb.astype(BF16), b_ref[...], preferred_element_type=F32)

    cw = n_state // S5_LANE_SPLIT
    for c in range(S5_LANE_SPLIT):
        re = slice(c * cw, (c + 1) * cw)
        im = slice(n_state + c * cw, n_state + (c + 1) * cw)
        a_r = a_ref[0, :, re]
        a_i = a_ref[1, :, re]

        def step(t, carry, re=re, im=im, a_r=a_r, a_i=a_i):
            s_r, s_i = carry
            tt = S5_TL - 1 - t if backward else t
            rows = pl.ds(pl.multiple_of(tt * bsz, bsz), bsz)
            n_r = a_r * s_r - a_i * s_i + xs[rows, re]
            n_i = a_r * s_i + a_i * s_r + xs[rows, im]
            xs[rows, re] = n_r
            xs[rows, im] = n_i
            return n_r, n_i

        s_r, s_i = lax.fori_loop(0, S5_TL, step, (st[:, re], st[:, im]), unroll=4)
        st[:, re] = s_r
        st[:, im] = s_i

    y_tb = jnp.dot(xs[...].astype(BF16), c_ref[...], preferred_element_type=F32)
    if not backward:
        o_ref[...] = y_tb
        return
    y_sum = y_tb + yf_ref[...]
    for h in range(n_half):
        tb[h] = y_sum[:, h * LANES:(h + 1) * LANES]
    y_bt = jnp.concatenate(
        [jnp.concatenate([tb[h, pl.ds(b, S5_TL, stride=bsz), :] for h in range(n_half)], axis=1)
         for b in range(bsz)], axis=0)
    y = jax.nn.gelu(d_ref[...] * u_ref[...].reshape(S5_TL * bsz, GROUP_W) + y_bt)
    gate = jnp.dot(y.astype(BF16), w_ref[...], preferred_element_type=F32) + bg_ref[...]
    o_ref[...] = (y * jax.nn.sigmoid(gate)).reshape(bsz, S5_TL, GROUP_W)


def _s5_mixer(rest, ssm_col, bmat, cmat, amat, d_skip, w_glu, b_glu, bsz, seq):
    n_chunks = seq // S5_TL
    n_rows = S5_TL * bsz
    n_st2 = bmat.shape[-1]
    rest3 = rest.reshape(bsz, seq, rest.shape[1])
    full = lambda a: pl.BlockSpec(a.shape, lambda i: (0,) * a.ndim)
    scratch = [pltpu.VMEM((n_rows, n_st2), F32),
               pltpu.VMEM((bsz, n_st2), F32),
               pltpu.VMEM((GROUP_W // LANES, n_rows, LANES), F32)]

    def direction(d, chunk_of):
        return [
            pl.BlockSpec((bsz, S5_TL, GROUP_W), lambda i: (0, chunk_of(i), ssm_col)),
            pl.BlockSpec((None, GROUP_W, n_st2), lambda i: (d, 0, 0)),
            pl.BlockSpec((None, n_st2, GROUP_W), lambda i: (d, 0, 0)),
            pl.BlockSpec((None, 2, bsz, n_st2 // 2), lambda i: (d, 0, 0, 0)),
        ]

    y_fwd = pl.pallas_call(
        functools.partial(_s5_kernel, bsz=bsz, backward=False),
        grid=(n_chunks,),
        in_specs=direction(0, lambda i: i),
        out_specs=pl.BlockSpec((n_rows, GROUP_W), lambda i: (i, 0)),
        out_shape=jax.ShapeDtypeStruct((seq * bsz, GROUP_W), F32),
        scratch_shapes=scratch,
        compiler_params=_cparams(("arbitrary",)),
        name="s5_forward",
    )(rest3, bmat, cmat, amat)

    rev = lambda i: n_chunks - 1 - i
    d_skip, b_glu = d_skip.reshape(1, GROUP_W), b_glu.reshape(1, GROUP_W)
    out = pl.pallas_call(
        functools.partial(_s5_kernel, bsz=bsz, backward=True),
        grid=(n_chunks,),
        in_specs=direction(1, rev) + [
            pl.BlockSpec((n_rows, GROUP_W), lambda i: (rev(i), 0)),
            full(d_skip), full(w_glu), full(b_glu),
        ],
        out_specs=pl.BlockSpec((bsz, S5_TL, GROUP_W), lambda i: (0, rev(i), 0)),
        out_shape=jax.ShapeDtypeStruct((bsz, seq, GROUP_W), F32),
        scratch_shapes=scratch,
        compiler_params=_cparams(("arbitrary",)),
        name="s5_backward",
    )(rest3, bmat, cmat, amat, y_fwd, d_skip, w_glu, b_glu)
    return out.reshape(bsz * seq, GROUP_W)


def _out_proj_kernel(y0_ref, y1_ref, y2_ref, y3_ref, gn_ref, w_ref, x_ref, o_ref, *, token_tiles):
    acc = x_ref[...]
    for j, y_ref in enumerate((y0_ref, y1_ref, y2_ref, y3_ref)):
        yn = _rms(y_ref[...], gn_ref[j:j + 1, :]).astype(BF16)
        acc = acc + jnp.dot(yn, w_ref[j * GROUP_W:(j + 1) * GROUP_W, :], preferred_element_type=F32)
    if token_tiles:
        _store_token_tiles(o_ref, acc)
    else:
        o_ref[...] = acc


def _out_proj(ys, gn, w, x2, token_tiles, tm=512):
    t, d = x2.shape
    yspec = pl.BlockSpec((tm, GROUP_W), lambda i: (i, 0))
    out_block, out_rows = ((tm * SUBLANES, LANES), t * SUBLANES) if token_tiles else ((tm, d), t)
    return pl.pallas_call(
        functools.partial(_out_proj_kernel, token_tiles=token_tiles),
        grid=(t // tm,),
        in_specs=[yspec] * 4 + [
            pl.BlockSpec((4, GROUP_W), lambda i: (0, 0)),
            pl.BlockSpec(w.shape, lambda i: (0, 0)),
            pl.BlockSpec((tm, d), lambda i: (i, 0)),
        ],
        out_specs=pl.BlockSpec(out_block, lambda i: (i, 0)),
        out_shape=jax.ShapeDtypeStruct((out_rows, out_block[1]), F32),
        compiler_params=_cparams(("arbitrary",)),
        name="out_proj",
    )(*ys, gn.reshape(4, GROUP_W), w, x2)


def _ffn_kernel(x_ref, g_ref, wg_ref, wu_ref, wd_ref, o_ref, h_ref, acc_ref):
    f = pl.program_id(1)

    @pl.when(f == 0)
    def _():
        h_ref[...] = _rms(x_ref[...], g_ref[...]).astype(BF16)
        acc_ref[...] = x_ref[...]

    h = h_ref[...]
    gate = jnp.dot(h, wg_ref[...], preferred_element_type=F32)
    up = jnp.dot(h, wu_ref[...], preferred_element_type=F32)
    act = (gate * jax.nn.sigmoid(gate) * up).astype(BF16)
    acc_ref[...] += jnp.dot(act, wd_ref[...], preferred_element_type=F32)

    @pl.when(f == pl.num_programs(1) - 1)
    def _():
        o_ref[...] = acc_ref[...]


def _dense_ffn(x2, g, wg, wu, wd, tm=512, tf=1408):
    t, d = x2.shape
    dff = wg.shape[1]
    return pl.pallas_call(
        _ffn_kernel,
        grid=(t // tm, dff // tf),
        in_specs=[
            pl.BlockSpec((tm, d), lambda i, f: (i, 0)),
            pl.BlockSpec((1, d), lambda i, f: (0, 0)),
            pl.BlockSpec((d, tf), lambda i, f: (0, f)),
            pl.BlockSpec((d, tf), lambda i, f: (0, f)),
            pl.BlockSpec((tf, d), lambda i, f: (f, 0)),
        ],
        out_specs=pl.BlockSpec((tm, d), lambda i, f: (i, 0)),
        out_shape=jax.ShapeDtypeStruct((t, d), F32),
        scratch_shapes=[pltpu.VMEM((tm, d), BF16), pltpu.VMEM((tm, d), F32)],
        compiler_params=_cparams(("arbitrary", "arbitrary")),
        name="dense_ffn",
    )(x2, g.reshape(1, d), wg, wu, wd)


ROUTE_COLS = 8
ROUTER_LANES = 128


def _router_kernel(x_ref, g_ref, wr_ref, route_ref):
    h = _rms(_load_token_tiles(x_ref, route_ref.shape[0]), g_ref[...])
    logits = jnp.dot(h, wr_ref[...], preferred_element_type=F32, precision=lax.Precision.HIGHEST)
    lane = lax.broadcasted_iota(jnp.int32, logits.shape, 1)
    minus_inf = -jnp.inf
    l1 = jnp.where(lane < N_EXPERTS, logits, minus_inf)
    m1 = jnp.max(l1, axis=-1, keepdims=True)
    i1 = jnp.min(jnp.where(l1 == m1, lane, ROUTER_LANES), axis=-1, keepdims=True)
    l2 = jnp.where(lane == i1, minus_inf, l1)
    m2 = jnp.max(l2, axis=-1, keepdims=True)
    i2 = jnp.min(jnp.where(l2 == m2, lane, ROUTER_LANES), axis=-1, keepdims=True)
    e2 = jnp.exp(m2 - m1)
    den = 1.0 + e2
    rec = jnp.where(lane == 0, i1.astype(F32),
                    jnp.where(lane == 1, i2.astype(F32),
                              jnp.where(lane == 2, 1.0 / den,
                                        jnp.where(lane == 3, e2 / den, 0.0))))
    route_ref[...] = rec[:, :ROUTE_COLS]


def _router(xt, g, w_router, tm=512):
    t = xt.shape[0] // SUBLANES
    d = SUBLANES * LANES
    wr = jnp.zeros((d, ROUTER_LANES), F32).at[:, :N_EXPERTS].set(w_router.astype(F32))
    return pl.pallas_call(
        _router_kernel,
        grid=(t // tm,),
        in_specs=[
            pl.BlockSpec((tm * SUBLANES, LANES), lambda i: (i, 0)),
            pl.BlockSpec((1, d), lambda i: (0, 0)),
            pl.BlockSpec((d, ROUTER_LANES), lambda i: (0, 0)),
        ],
        out_specs=pl.BlockSpec((tm, ROUTE_COLS), lambda i: (i, 0)),
        out_shape=jax.ShapeDtypeStruct((t, ROUTE_COLS), F32),
        compiler_params=_cparams(("arbitrary",)),
        name="moe_router",
    )(xt, g.reshape(1, d), wr)


def _moe_plan(route, tm, n_tiles):
    experts = route[:, :2].astype(jnp.int32).reshape(-1)
    onehot = (experts[:, None] == jnp.arange(N_EXPERTS, dtype=jnp.int32)[None, :]).astype(jnp.int32)
    csum = jnp.cumsum(onehot, axis=0)
    rank = jnp.sum((csum - onehot) * onehot, axis=1)
    count = csum[-1]
    padded = ((count + tm - 1) // tm) * tm
    end = jnp.cumsum(padded)
    start = end - padded
    slot = jnp.sum(onehot * start[None, :], axis=1) + rank
    n_active = end[-1] // tm
    tile_start = jnp.arange(n_tiles, dtype=jnp.int32) * tm
    tile_expert = jnp.sum((tile_start[:, None] >= end[None, :]).astype(jnp.int32), axis=1)
    tile_expert = jnp.minimum(tile_expert, N_EXPERTS - 1)
    last = tile_expert[jnp.maximum(n_active - 1, 0)]
    tile_expert = jnp.where(jnp.arange(n_tiles) < n_active, tile_expert, last)
    return slot.astype(jnp.int32), tile_expert.astype(jnp.int32), n_active.astype(jnp.int32).reshape(1)


def _token_copy(src, src_tok, dst, dst_tok, sem):
    rows = lambda tok: pl.ds(pl.multiple_of(tok * SUBLANES, SUBLANES), SUBLANES)
    return pltpu.make_async_copy(src.at[rows(src_tok)], dst.at[rows(dst_tok)], sem)


def _invert_kernel(slot_ref, word_ref, pad_hbm, plan_ref, sem):
    @pl.when(pl.program_id(0) == 0)
    def _():
        init = pltpu.make_async_copy(pad_hbm, plan_ref, sem)
        init.start()
        init.wait()

    def place(a, carry):
        plan_ref[slot_ref[0, 0, a]] = word_ref[0, 0, a]
        return carry

    lax.fori_loop(0, slot_ref.shape[-1], place, 0, unroll=8)


def _plan_bits(n_tok, tm):
    src_bits = (n_tok - 1).bit_length()
    assert (2 * n_tok + 2 * tm) << src_bits < 2 ** 31 and tm & (tm - 1) == 0
    return src_bits


def _invert(slot, n_slots, tm, n_tok, chunk=4096):
    src_bits = _plan_bits(n_tok, tm)
    a = jnp.arange(slot.shape[0], dtype=jnp.int32)
    tok = a >> 1
    word = tok + (((a & 1) * n_tok + tok) << src_bits)
    s = jnp.arange(n_slots, dtype=jnp.int32)
    pad_plan = (2 * n_tok + (s & (2 * tm - 1))) << src_bits
    n_chunks = slot.shape[0] // chunk
    chunked = pl.BlockSpec((1, 1, chunk), lambda c: (c, 0, 0), memory_space=pltpu.SMEM)
    return pl.pallas_call(
        _invert_kernel,
        grid=(n_chunks,),
        in_specs=[chunked, chunked, pl.BlockSpec(memory_space=pl.ANY)],
        out_specs=pl.BlockSpec((n_slots,), lambda c: (0,), memory_space=pltpu.SMEM),
        out_shape=jax.ShapeDtypeStruct((n_slots,), jnp.int32),
        scratch_shapes=[pltpu.SemaphoreType.DMA(())],
        compiler_params=pltpu.CompilerParams(dimension_semantics=("arbitrary",)),
        name="moe_invert",
    )(slot.reshape(n_chunks, 1, chunk), word.reshape(n_chunks, 1, chunk), pad_plan)


def _tokens_done(ref, n_tok, sem):
    rows = pl.ds(0, n_tok * SUBLANES)
    pltpu.make_async_copy(ref.at[rows], ref.at[rows], sem).wait()


def _moe_ffn_kernel(te_ref, nact_ref, plan_ref, x_hbm, g_ref, wg_ref, wu_ref, wd_ref, yk_hbm,
                    xbuf, xb16, ybuf, acc_ref, sem_in, sem_out, *, tm, n_tok, n_f, pad_tile, src_bits):
    del te_ref
    i = pl.program_id(0)
    f = pl.program_id(1)
    par = i % 2
    nact = nact_ref[0]
    per_step = tm // n_f

    def fetch(tile, p, j):
        src = plan_ref[tile * tm + j] & ((1 << src_bits) - 1)
        _token_copy(x_hbm, src, xbuf.at[p], j, sem_in.at[p]).start()

    def send(tile, p, j):
        dst = plan_ref[tile * tm + j] >> src_bits
        _token_copy(ybuf.at[p], j, yk_hbm, dst, sem_out.at[p]).start()

    @pl.when((i == 0) & (f == 0))
    def _():
        def first(j, carry):
            fetch(0, 0, j)
            return carry
        lax.fori_loop(0, tm, first, 0)
        ybuf[...] = jnp.zeros_like(ybuf)
        for p in range(2):
            spare_rows = pl.ds((2 * n_tok + p * tm) * SUBLANES, tm * SUBLANES)
            spare = pltpu.make_async_copy(ybuf.at[p], yk_hbm.at[spare_rows], sem_out.at[p])
            spare.start()
            spare.wait()
        _tokens_done(xbuf.at[0], tm, sem_in.at[0])

    @pl.when((i >= 1) & (i <= nact) & (f == 0))
    def _():
        _tokens_done(xbuf.at[par], tm, sem_in.at[par])

    @pl.when(i < nact)
    def _():
        @pl.when(f == 0)
        def _():
            xb16[...] = _rms(_load_token_tiles(xbuf.at[par], tm), g_ref[...]).astype(BF16)
            acc_ref[...] = jnp.zeros_like(acc_ref)

        prev = jnp.where(i > 0, i - 1, pad_tile)
        for jj in range(per_step):
            j = f * per_step + jj
            fetch(i + 1, 1 - par, j)
            send(prev, 1 - par, j)

        x = xb16[...]
        gate = jnp.dot(x, wg_ref[...], preferred_element_type=F32)
        up = jnp.dot(x, wu_ref[...], preferred_element_type=F32)
        act = (gate * jax.nn.sigmoid(gate) * up).astype(BF16)
        acc_ref[...] += jnp.dot(act, wd_ref[...], preferred_element_type=F32)

        @pl.when(f == n_f - 1)
        def _():
            @pl.when(i >= 1)
            def _():
                _tokens_done(ybuf.at[par], tm, sem_out.at[par])

            _store_token_tiles(ybuf.at[par], acc_ref[...])

    @pl.when((i == nact) & (f == 0))
    def _():
        def last(j, carry):
            send(i - 1, 1 - par, j)
            return carry
        lax.fori_loop(0, tm, last, 0)
        _tokens_done(ybuf.at[0], tm, sem_out.at[0])
        _tokens_done(ybuf.at[1], tm, sem_out.at[1])


def _moe_ffn(xt, g, plan, tile_expert, n_active, wg, wu, wd, tm):
    n_tok = xt.shape[0] // SUBLANES
    d = SUBLANES * LANES
    n_tiles = tile_expert.shape[0]
    pad_tile = n_tiles - 1 if (n_tiles - 1) % 2 else n_tiles
    assert plan.shape[0] >= (pad_tile + 1) * tm
    dff = wg.shape[-1]
    n_f = 4
    tf = dff // n_f
    fidx = lambda i, f, na: jnp.where(i < na[0], f, n_f - 1)
    grid_spec = pltpu.PrefetchScalarGridSpec(
        num_scalar_prefetch=3,
        grid=(n_tiles, n_f),
        in_specs=[
            pl.BlockSpec(memory_space=pl.ANY),
            pl.BlockSpec((1, d), lambda i, f, te, na, plan: (0, 0)),
            pl.BlockSpec((None, d, tf), lambda i, f, te, na, plan: (te[i], 0, fidx(i, f, na))),
            pl.BlockSpec((None, d, tf), lambda i, f, te, na, plan: (te[i], 0, fidx(i, f, na))),
            pl.BlockSpec((None, tf, d), lambda i, f, te, na, plan: (te[i], fidx(i, f, na), 0)),
        ],
        out_specs=pl.BlockSpec(memory_space=pl.ANY),
        scratch_shapes=[
            pltpu.VMEM((2, tm * SUBLANES, LANES), F32),
            pltpu.VMEM((tm, d), BF16),
            pltpu.VMEM((2, tm * SUBLANES, LANES), F32),
            pltpu.VMEM((tm, d), F32),
            pltpu.SemaphoreType.DMA((2,)),
            pltpu.SemaphoreType.DMA((2,)),
        ],
    )
    return pl.pallas_call(
        functools.partial(_moe_ffn_kernel, tm=tm, n_tok=n_tok, n_f=n_f, pad_tile=pad_tile,
                          src_bits=_plan_bits(n_tok, tm)),
        grid_spec=grid_spec,
        out_shape=jax.ShapeDtypeStruct(((2 * n_tok + 2 * tm) * SUBLANES, LANES), F32),
        compiler_params=_cparams(("arbitrary", "arbitrary")),
        name="moe_ffn",
    )(tile_expert, n_active, plan, xt, g.reshape(1, d), wg, wu, wd)


def _combine_kernel(route_ref, x_ref, y0_ref, y1_ref, fg_ref, o_ref, *, final_norm):
    r = route_ref[...]
    tm = r.shape[0]
    y = (_load_token_tiles(x_ref, tm) + r[:, 2:3] * _load_token_tiles(y0_ref, tm)
         + r[:, 3:4] * _load_token_tiles(y1_ref, tm))
    o_ref[...] = _rms(y, fg_ref[...]) if final_norm else y


def _combine(yk, route, xt, final_g, final_norm, tm=512):
    t = xt.shape[0] // SUBLANES
    d = SUBLANES * LANES
    n_blk = t // tm
    tiles = lambda f: pl.BlockSpec((tm * SUBLANES, LANES), f)
    return pl.pallas_call(
        functools.partial(_combine_kernel, final_norm=final_norm),
        grid=(n_blk,),
        in_specs=[
            pl.BlockSpec((tm, ROUTE_COLS), lambda i: (i, 0)),
            tiles(lambda i: (i, 0)),
            tiles(lambda i: (i, 0)),
            tiles(lambda i: (n_blk + i, 0)),
            pl.BlockSpec((1, d), lambda i: (0, 0)),
        ],
        out_specs=pl.BlockSpec((tm, d), lambda i: (i, 0)),
        out_shape=jax.ShapeDtypeStruct((t, d), F32),
        compiler_params=_cparams(("arbitrary",)),
        name="moe_combine",
    )(route, xt, yk, yk, final_g.reshape(1, d))


def _final_norm_kernel(x_ref, g_ref, o_ref):
    o_ref[...] = _rms(x_ref[...], g_ref[...])


def _final_norm(x2, g, tm=512):
    t, d = x2.shape
    return pl.pallas_call(
        _final_norm_kernel,
        grid=(t // tm,),
        in_specs=[pl.BlockSpec((tm, d), lambda i: (i, 0)), pl.BlockSpec((1, d), lambda i: (0, 0))],
        out_specs=pl.BlockSpec((tm, d), lambda i: (i, 0)),
        out_shape=jax.ShapeDtypeStruct((t, d), F32),
        compiler_params=_cparams(("arbitrary",)),
        name="final_norm",
    )(x2, g.reshape(1, d))


MOE_TM = 512


def kernel(x, norm1_g, w_in, na_rpb, sc_conv_w, cf_conv_w, cf_conv_b, cf_ln_g, cf_ln_b, ssm_a_re, ssm_a_im,
           ssm_log_dt, ssm_b_re, ssm_b_im, ssm_c_re, ssm_c_im, ssm_d, ssm_w_glu, ssm_b_glu, grp_norm_g, w_out,
           norm2_g, ffn_w_gate, ffn_w_up, ffn_w_down, moe_w_router, moe_w_gate, moe_w_up, moe_w_down,
           final_norm_g):
    bsz, seq, d = x.shape
    depth = w_in.shape[0]
    t = bsz * seq
    rows = seq // GRID_W
    x2 = x.reshape(t, d).astype(F32)
    final_done = False
    for i in range(depth):
        qkv, rest = _in_proj(x2, norm1_g[i], w_in[i].astype(BF16))
        y_na = _na_attention(qkv, _na_bias_table(na_rpb[i], rows), bsz, seq)
        y_sc, y_cf = _conv_mixers(rest, sc_conv_w[i], cf_conv_w[i], cf_conv_b[i], cf_ln_g[i], cf_ln_b[i],
                                  bsz, seq)
        bmat, cmat, amat = _s5_discretise(ssm_a_re[i], ssm_a_im[i], ssm_log_dt[i], ssm_b_re[i], ssm_b_im[i],
                                          ssm_c_re[i], ssm_c_im[i], bsz)
        y_ssm = _s5_mixer(rest, SSM_COL, bmat, cmat, amat, ssm_d[i], ssm_w_glu[i].astype(BF16), ssm_b_glu[i],
                          bsz, seq)
        x2 = _out_proj((y_na, y_sc, y_cf, y_ssm), grp_norm_g[i], w_out[i].astype(BF16), x2,
                       token_tiles=(i % 2 == 1))
        j = i // 2
        if i % 2 == 0:
            x2 = _dense_ffn(x2, norm2_g[i], ffn_w_gate[j].astype(BF16), ffn_w_up[j].astype(BF16),
                            ffn_w_down[j].astype(BF16))
        else:
            route = _router(x2, norm2_g[i], moe_w_router[j])
            n_tiles = (2 * t) // MOE_TM + N_EXPERTS + 1
            slot, tile_expert, n_active = _moe_plan(route, MOE_TM, n_tiles)
            plan = _invert(slot, (n_tiles + 1) * MOE_TM, MOE_TM, t)
            yk = _moe_ffn(x2, norm2_g[i], plan, tile_expert, n_active, moe_w_gate[j].astype(BF16),
                          moe_w_up[j].astype(BF16), moe_w_down[j].astype(BF16), MOE_TM)
            final_done = i == depth - 1
            x2 = _combine(yk, route, x2, final_norm_g, final_done)
    if not final_done:
        x2 = _final_norm(x2, final_norm_g)
    return x2.reshape(bsz, seq, d)
```

```python
import functools
import math

import jax
import jax.numpy as jnp
import numpy as np
from jax import lax
from jax.experimental import pallas as pl
from jax.experimental.pallas import tpu as pltpu

F32 = jnp.float32
BF16 = jnp.bfloat16
EPS = 1e-6
NEG_INF = -1e30

GRID_W = 64
NA_HEAD_DIM = 64
NA_WIN_H = 8
NA_WIN_W = 16
GROUP_W = 256
SSM_GROUP_CH = 16
SSM_STATE = 64
N_EXPERTS = 8
VMEM_LIMIT = 56 * 1024 * 1024


def _cparams(sem, vmem=VMEM_LIMIT):
    return pltpu.CompilerParams(dimension_semantics=sem, vmem_limit_bytes=vmem)


def _rms(x, g):
    return x * lax.rsqrt(jnp.mean(x * x, axis=-1, keepdims=True) + EPS) * g


SUBLANES = 8
LANES = 128


def _load_token_tiles(ref, n):
    return jnp.concatenate([ref[pl.ds(s, n, stride=SUBLANES), :] for s in range(SUBLANES)], axis=1)


def _store_token_tiles(ref, val):
    for s in range(SUBLANES):
        ref[pl.ds(s, val.shape[0], stride=SUBLANES), :] = val[:, s * LANES:(s + 1) * LANES]


N_QKV_COLS = 3 * GROUP_W
SSM_COL = 5


def _in_proj_kernel(x_ref, g_ref, w_ref, qkv_ref, rest_ref):
    h = _rms(x_ref[...], g_ref[...]).astype(BF16)
    n_qkv = qkv_ref.shape[1] // GROUP_W
    n_rest = rest_ref.shape[1] // GROUP_W
    for j in range(n_qkv + n_rest):
        z = jnp.dot(h, w_ref[:, j * GROUP_W:(j + 1) * GROUP_W], preferred_element_type=F32)
        if j < n_qkv:
            qkv_ref[:, j * GROUP_W:(j + 1) * GROUP_W] = z.astype(BF16)
        else:
            k = j - n_qkv
            rest_ref[:, k * GROUP_W:(k + 1) * GROUP_W] = z


def _in_proj(x2, g, w, tm=512):
    t, d = x2.shape
    n_cols = w.shape[1]
    n_rest = n_cols - N_QKV_COLS
    return pl.pallas_call(
        _in_proj_kernel,
        grid=(t // tm,),
        in_specs=[
            pl.BlockSpec((tm, d), lambda i: (i, 0)),
            pl.BlockSpec((1, d), lambda i: (0, 0)),
            pl.BlockSpec((d, n_cols), lambda i: (0, 0)),
        ],
        out_specs=[
            pl.BlockSpec((tm, N_QKV_COLS), lambda i: (i, 0)),
            pl.BlockSpec((tm, n_rest), lambda i: (i, 0)),
        ],
        out_shape=[
            jax.ShapeDtypeStruct((t, N_QKV_COLS), BF16),
            jax.ShapeDtypeStruct((t, n_rest), F32),
        ],
        compiler_params=_cparams(("arbitrary",)),
        name="in_proj",
    )(x2, g.reshape(1, d), w)


def _na_bias_table(rpb, rows):
    n_heads = rpb.shape[0]
    kh = NA_WIN_H
    c = np.arange(GRID_W)
    qcs = np.clip(c - NA_WIN_W // 2, 0, GRID_W - NA_WIN_W)
    kc = np.arange(GRID_W)
    in_win = (kc[None, :] >= qcs[:, None]) & (kc[None, :] < qcs[:, None] + NA_WIN_W)
    dc_idx = np.clip(kc[None, :] - c[:, None] + NA_WIN_W - 1, 0, 2 * NA_WIN_W - 2)
    onehot_dc = (dc_idx[:, :, None] == np.arange(2 * NA_WIN_W - 1)).astype(np.float32)
    tab = jnp.einsum("hab,ckb->hcak", rpb.astype(F32), jnp.asarray(onehot_dc),
                     precision=lax.Precision.HIGHEST)
    tab = jnp.where(jnp.asarray(in_win)[None, :, None, :], tab, NEG_INF)
    rep_rows = [0, 1, 2, 3, 4, rows - 3, rows - 2, rows - 1]
    tabs = []
    for r in rep_rows:
        rs = int(np.clip(r - kh // 2, 0, rows - kh))
        dr0 = rs - r + NA_WIN_H - 1
        tabs.append(tab[:, :, dr0:dr0 + kh, :].reshape(n_heads * GRID_W, kh * GRID_W))
    return jnp.stack(tabs)


def _na_kernel(q_ref, k_ref, v_ref, bias_ref, o_ref, *, rows, n_heads):
    kh = NA_WIN_H
    lane = lax.broadcasted_iota(jnp.int32, (GRID_W, n_heads * NA_HEAD_DIM), 1)
    masks = [(lane >= h * NA_HEAD_DIM) & (lane < (h + 1) * NA_HEAD_DIM) for h in range(n_heads)]
    scale = NA_HEAD_DIM ** -0.5

    def body(r, carry):
        q0 = pl.multiple_of(r * GRID_W, GRID_W)
        q = q_ref[pl.ds(q0, GRID_W), :].astype(F32) * scale
        qs = jnp.concatenate([jnp.where(m, q, 0.0) for m in masks], axis=0).astype(BF16)
        rs = jnp.clip(r - kh // 2, 0, rows - kh)
        k0 = pl.multiple_of(rs * GRID_W, GRID_W)
        ks = k_ref[pl.ds(k0, kh * GRID_W), :]
        vs = v_ref[pl.ds(k0, kh * GRID_W), :]
        s = lax.dot_general(qs, ks, (((1,), (1,)), ((), ())), preferred_element_type=F32)
        var = jnp.where(r < kh // 2, r, jnp.where(r > rows - kh // 2, r - (rows - 8), kh // 2))
        s = s + bias_ref[var]
        m = jnp.max(s, axis=-1, keepdims=True)
        p = jnp.exp(s - m)
        den = jnp.sum(p, axis=-1, keepdims=True)
        o = jnp.dot(p.astype(BF16), vs, preferred_element_type=F32) / den
        out = jnp.where(masks[0], o[0:GRID_W], 0.0)
        for h in range(1, n_heads):
            out = out + jnp.where(masks[h], o[h * GRID_W:(h + 1) * GRID_W], 0.0)
        o_ref[pl.ds(q0, GRID_W), :] = out
        return carry

    lax.fori_loop(0, rows, body, 0, unroll=2)


def _na_attention(qkv, bias, bsz, seq):
    rows = seq // GRID_W
    n_heads = GROUP_W // NA_HEAD_DIM
    t = qkv.shape[0]
    blk = lambda j: pl.BlockSpec((seq, GROUP_W), lambda b, j=j: (b, j))
    return pl.pallas_call(
        functools.partial(_na_kernel, rows=rows, n_heads=n_heads),
        grid=(bsz,),
        in_specs=[blk(0), blk(1), blk(2),
                  pl.BlockSpec(bias.shape, lambda b: (0, 0, 0))],
        out_specs=pl.BlockSpec((seq, GROUP_W), lambda b: (b, 0)),
        out_shape=jax.ShapeDtypeStruct((t, GROUP_W), F32),
        compiler_params=_cparams(("arbitrary",)),
        name="na_attention",
    )(qkv, qkv, qkv, bias)


CONV_PAD = 16
CONV_CHUNK = 128


def _conv_kernel(b_ref, c_ref, x_ref, a_ref, g_ref, scw_ref, cfw_ref, cfb_ref, lng_ref, lnb_ref,
                 ysc_ref, ycf_ref, psc, pcf):
    seq = b_ref.shape[0]
    n_chunks = seq // CONV_CHUNK
    zeros = jnp.zeros((CONV_PAD, GROUP_W), F32)
    psc[0:CONV_PAD, :] = zeros
    pcf[0:CONV_PAD, :] = zeros
    psc[CONV_PAD + seq:2 * CONV_PAD + seq, :] = zeros
    pcf[CONV_PAD + seq:2 * CONV_PAD + seq, :] = zeros

    def fill(i, carry):
        r0 = pl.multiple_of(i * CONV_CHUNK, CONV_CHUNK)
        rows = pl.ds(r0, CONV_CHUNK)
        dst = pl.ds(r0 + CONV_PAD, CONV_CHUNK)
        psc[dst, :] = c_ref[rows, :] * x_ref[rows, :]
        pcf[dst, :] = a_ref[rows, :] * jax.nn.sigmoid(g_ref[rows, :])
        return carry

    lax.fori_loop(0, n_chunks, fill, 0)

    def taps(win, w_ref):
        n_taps = w_ref.shape[0]
        offs = [CONV_PAD + k - n_taps // 2 for k in range(n_taps)]
        n_win = win.shape[0]
        acc = None
        for s in range(SUBLANES):
            ks = [k for k in range(n_taps) if offs[k] % SUBLANES == s]
            if not ks:
                continue
            shifted = pltpu.roll(win, n_win - s, axis=0) if s else win
            for k in ks:
                q = offs[k] - s
                term = shifted[q:q + CONV_CHUNK, :] * w_ref[k:k + 1, :]
                acc = term if acc is None else acc + term
        return acc

    def conv(i, carry):
        r0 = pl.multiple_of(i * CONV_CHUNK, CONV_CHUNK)
        rows = pl.ds(r0, CONV_CHUNK)
        window = pl.ds(r0, CONV_CHUNK + 2 * CONV_PAD)
        ysc_ref[rows, :] = b_ref[rows, :] * taps(psc[window, :], scw_ref)
        cf = taps(pcf[window, :], cfw_ref) + cfb_ref[...]
        mu = jnp.mean(cf, axis=-1, keepdims=True)
        xc = cf - mu
        var = jnp.mean(xc * xc, axis=-1, keepdims=True)
        ln = xc * lax.rsqrt(var + EPS) * lng_ref[...] + lnb_ref[...]
        ycf_ref[rows, :] = ln * jax.nn.sigmoid(ln)
        return carry

    lax.fori_loop(0, n_chunks, conv, 0)


def _conv_mixers(rest, sc_w, cf_w, cf_b, ln_g, ln_b, bsz, seq):
    t = rest.shape[0]
    blk = lambda j: pl.BlockSpec((seq, GROUP_W), lambda b, j=j: (b, j))
    full = lambda a: pl.BlockSpec(a.shape, lambda b: (0,) * a.ndim)
    cf_b, ln_g, ln_b = (a.reshape(1, GROUP_W) for a in (cf_b, ln_g, ln_b))
    out_spec = pl.BlockSpec((seq, GROUP_W), lambda b: (b, 0))
    return pl.pallas_call(
        _conv_kernel,
        grid=(bsz,),
        in_specs=[blk(0), blk(1), blk(2), blk(3), blk(4),
                  full(sc_w), full(cf_w), full(cf_b), full(ln_g), full(ln_b)],
        out_specs=[out_spec, out_spec],
        out_shape=[jax.ShapeDtypeStruct((t, GROUP_W), F32)] * 2,
        scratch_shapes=[pltpu.VMEM((seq + 2 * CONV_PAD, GROUP_W), F32)] * 2,
        compiler_params=_cparams(("arbitrary",)),
        name="conv_mixers",
    )(rest, rest, rest, rest, rest, sc_w, cf_w, cf_b, ln_g, ln_b)


S5_TL = 128
S5_LANE_SPLIT = 2


def _s5_discretise(a_re, a_im, log_dt, b_re, b_im, c_re, c_im, bsz):
    f32 = F32
    a_re, a_im, log_dt = a_re.astype(f32), a_im.astype(f32), log_dt.astype(f32)
    n_dir, n_grp, n_state = a_re.shape
    n_ch = b_re.shape[-1]
    dt = jnp.exp(log_dt)[..., None]
    mag = jnp.exp(a_re * dt)
    abr = mag * jnp.cos(a_im * dt)
    abi = mag * jnp.sin(a_im * dt)
    den = a_re * a_re + a_im * a_im
    fr = ((abr - 1.0) * a_re + abi * a_im) / den
    fi = (abi * a_re - (abr - 1.0) * a_im) / den
    bbr = fr[..., None] * b_re - fi[..., None] * b_im
    bbi = fr[..., None] * b_im + fi[..., None] * b_re
    eye = jnp.eye(n_grp, dtype=f32)
    blk_b = lambda m: jnp.einsum("dgps,gh->dgshp", m, eye).reshape(n_dir, n_grp * n_ch, n_grp * n_state)
    blk_c = lambda m: jnp.einsum("dgsp,gh->dgphs", m, eye).reshape(n_dir, n_grp * n_state, n_grp * n_ch)
    bmat = jnp.concatenate([blk_b(bbr), blk_b(bbi)], axis=-1)
    cmat = jnp.concatenate([blk_c(c_re.astype(f32)), blk_c(-c_im.astype(f32))], axis=1)
    amat = jnp.stack([abr.reshape(n_dir, -1), abi.reshape(n_dir, -1)], axis=1)
    amat = jnp.broadcast_to(amat[:, :, None, :], (n_dir, 2, bsz, n_grp * n_state))
    return bmat.astype(BF16), cmat.astype(BF16), amat


def _s5_kernel(*refs, bsz, backward):
    if backward:
        (u_ref, b_ref, c_ref, a_ref, yf_ref, d_ref, w_ref, bg_ref, o_ref, xs, st, tb) = refs
    else:
        (u_ref, b_ref, c_ref, a_ref, o_ref, xs, st, tb) = refs
    n_state = a_ref.shape[-1]

    @pl.when(pl.program_id(0) == 0)
    def _():
        st[...] = jnp.zeros_like(st)

    n_half = GROUP_W // LANES
    for b in range(bsz):
        for h in range(n_half):
            tb[h, pl.ds(b, S5_TL, stride=bsz), :] = u_ref[b, :, h * LANES:(h + 1) * LANES]
    u_tb = jnp.concatenate([tb[h] for h in range(n_half)], axis=1)
    xs[...] = jnp.dot(u_tb.astype(BF16), b_ref[...], preferred_element_type=F32)

    cw = n_state // S5_LANE_SPLIT
    for c in range(S5_LANE_SPLIT):
        re = slice(c * cw, (c + 1) * cw)
        im = slice(n_state + c * cw, n_state + (c + 1) * cw)
        a_r = a_ref[0, :, re]
        a_i = a_ref[1, :, re]

        def step(t, carry, re=re, im=im, a_r=a_r, a_i=a_i):
            s_r, s_i = carry
            tt = S5_TL - 1 - t if backward else t
            rows = pl.ds(pl.multiple_of(tt * bsz, bsz), bsz)
            n_r = a_r * s_r - a_i * s_i + xs[rows, re]
            n_i = a_r * s_i + a_i * s_r + xs[rows, im]
            xs[rows, re] = n_r
            xs[rows, im] = n_i
            return n_r, n_i

        s_r, s_i = lax.fori_loop(0, S5_TL, step, (st[:, re], st[:, im]), unroll=4)
        st[:, re] = s_r
        st[:, im] = s_i

    y_tb = jnp.dot(xs[...].astype(BF16), c_ref[...], preferred_element_type=F32)
    if not backward:
        o_ref[...] = y_tb
        return
    y_sum = y_tb + yf_ref[...]
    for h in range(n_half):
        tb[h] = y_sum[:, h * LANES:(h + 1) * LANES]
    y_bt = jnp.concatenate(
        [jnp.concatenate([tb[h, pl.ds(b, S5_TL, stride=bsz), :] for h in range(n_half)], axis=1)
         for b in range(bsz)], axis=0)
    y = jax.nn.gelu(d_ref[...] * u_ref[...].reshape(S5_TL * bsz, GROUP_W) + y_bt)
    gate = jnp.dot(y.astype(BF16), w_ref[...], preferred_element_type=F32) + bg_ref[...]
    o_ref[...] = (y * jax.nn.sigmoid(gate)).reshape(bsz, S5_TL, GROUP_W)


def _s5_mixer(rest, ssm_col, bmat, cmat, amat, d_skip, w_glu, b_glu, bsz, seq):
    n_chunks = seq // S5_TL
    n_rows = S5_TL * bsz
    n_st2 = bmat.shape[-1]
    rest3 = rest.reshape(bsz, seq, rest.shape[1])
    full = lambda a: pl.BlockSpec(a.shape, lambda i: (0,) * a.ndim)
    scratch = [pltpu.VMEM((n_rows, n_st2), F32),
               pltpu.VMEM((bsz, n_st2), F32),
               pltpu.VMEM((GROUP_W // LANES, n_rows, LANES), F32)]

    def direction(d, chunk_of):
        return [
            pl.BlockSpec((bsz, S5_TL, GROUP_W), lambda i: (0, chunk_of(i), ssm_col)),
            pl.BlockSpec((None, GROUP_W, n_st2), lambda i: (d, 0, 0)),
            pl.BlockSpec((None, n_st2, GROUP_W), lambda i: (d, 0, 0)),
            pl.BlockSpec((None, 2, bsz, n_st2 // 2), lambda i: (d, 0, 0, 0)),
        ]

    y_fwd = pl.pallas_call(
        functools.partial(_s5_kernel, bsz=bsz, backward=False),
        grid=(n_chunks,),
        in_specs=direction(0, lambda i: i),
        out_specs=pl.BlockSpec((n_rows, GROUP_W), lambda i: (i, 0)),
        out_shape=jax.ShapeDtypeStruct((seq * bsz, GROUP_W), F32),
        scratch_shapes=scratch,
        compiler_params=_cparams(("arbitrary",)),
        name="s5_forward",
    )(rest3, bmat, cmat, amat)

    rev = lambda i: n_chunks - 1 - i
    d_skip, b_glu = d_skip.reshape(1, GROUP_W), b_glu.reshape(1, GROUP_W)
    out = pl.pallas_call(
        functools.partial(_s5_kernel, bsz=bsz, backward=True),
        grid=(n_chunks,),
        in_specs=direction(1, rev) + [
            pl.BlockSpec((n_rows, GROUP_W), lambda i: (rev(i), 0)),
            full(d_skip), full(w_glu), full(b_glu),
        ],
        out_specs=pl.BlockSpec((bsz, S5_TL, GROUP_W), lambda i: (0, rev(i), 0)),
        out_shape=jax.ShapeDtypeStruct((bsz, seq, GROUP_W), F32),
        scratch_shapes=scratch,
        compiler_params=_cparams(("arbitrary",)),
        name="s5_backward",
    )(rest3, bmat, cmat, amat, y_fwd, d_skip, w_glu, b_glu)
    return out.reshape(bsz * seq, GROUP_W)


def _out_proj_kernel(y0_ref, y1_ref, y2_ref, y3_ref, gn_ref, w_ref, x_ref, o_ref, *, token_tiles):
    acc = x_ref[...]
    for j, y_ref in enumerate((y0_ref, y1_ref, y2_ref, y3_ref)):
        yn = _rms(y_ref[...], gn_ref[j:j + 1, :]).astype(BF16)
        acc = acc + jnp.dot(yn, w_ref[j * GROUP_W:(j + 1) * GROUP_W, :], preferred_element_type=F32)
    if token_tiles:
        _store_token_tiles(o_ref, acc)
    else:
        o_ref[...] = acc


def _out_proj(ys, gn, w, x2, token_tiles, tm=512):
    t, d = x2.shape
    yspec = pl.BlockSpec((tm, GROUP_W), lambda i: (i, 0))
    out_block, out_rows = ((tm * SUBLANES, LANES), t * SUBLANES) if token_tiles else ((tm, d), t)
    return pl.pallas_call(
        functools.partial(_out_proj_kernel, token_tiles=token_tiles),
        grid=(t // tm,),
        in_specs=[yspec] * 4 + [
            pl.BlockSpec((4, GROUP_W), lambda i: (0, 0)),
            pl.BlockSpec(w.shape, lambda i: (0, 0)),
            pl.BlockSpec((tm, d), lambda i: (i, 0)),
        ],
        out_specs=pl.BlockSpec(out_block, lambda i: (i, 0)),
        out_shape=jax.ShapeDtypeStruct((out_rows, out_block[1]), F32),
        compiler_params=_cparams(("arbitrary",)),
        name="out_proj",
    )(*ys, gn.reshape(4, GROUP_W), w, x2)


def _ffn_kernel(x_ref, g_ref, wg_ref, wu_ref, wd_ref, o_ref, h_ref, acc_ref):
    f = pl.program_id(1)

    @pl.when(f == 0)
    def _():
        h_ref[...] = _rms(x_ref[...], g_ref[...]).astype(BF16)
        acc_ref[...] = x_ref[...]

    h = h_ref[...]
    gate = jnp.dot(h, wg_ref[...], preferred_element_type=F32)
    up = jnp.dot(h, wu_ref[...], preferred_element_type=F32)
    act = (gate * jax.nn.sigmoid(gate) * up).astype(BF16)
    acc_ref[...] += jnp.dot(act, wd_ref[...], preferred_element_type=F32)

    @pl.when(f == pl.num_programs(1) - 1)
    def _():
        o_ref[...] = acc_ref[...]


def _dense_ffn(x2, g, wg, wu, wd, tm=512, tf=1408):
    t, d = x2.shape
    dff = wg.shape[1]
    return pl.pallas_call(
        _ffn_kernel,
        grid=(t // tm, dff // tf),
        in_specs=[
            pl.BlockSpec((tm, d), lambda i, f: (i, 0)),
            pl.BlockSpec((1, d), lambda i, f: (0, 0)),
            pl.BlockSpec((d, tf), lambda i, f: (0, f)),
            pl.BlockSpec((d, tf), lambda i, f: (0, f)),
            pl.BlockSpec((tf, d), lambda i, f: (f, 0)),
        ],
        out_specs=pl.BlockSpec((tm, d), lambda i, f: (i, 0)),
        out_shape=jax.ShapeDtypeStruct((t, d), F32),
        scratch_shapes=[pltpu.VMEM((tm, d), BF16), pltpu.VMEM((tm, d), F32)],
        compiler_params=_cparams(("arbitrary", "arbitrary")),
        name="dense_ffn",
    )(x2, g.reshape(1, d), wg, wu, wd)


ROUTE_COLS = 8
ROUTER_LANES = 128


def _router_kernel(x_ref, g_ref, wr_ref, route_ref):
    h = _rms(_load_token_tiles(x_ref, route_ref.shape[0]), g_ref[...])
    logits = jnp.dot(h, wr_ref[...], preferred_element_type=F32, precision=lax.Precision.HIGHEST)
    lane = lax.broadcasted_iota(jnp.int32, logits.shape, 1)
    minus_inf = -jnp.inf
    l1 = jnp.where(lane < N_EXPERTS, logits, minus_inf)
    m1 = jnp.max(l1, axis=-1, keepdims=True)
    i1 = jnp.min(jnp.where(l1 == m1, lane, ROUTER_LANES), axis=-1, keepdims=True)
    l2 = jnp.where(lane == i1, minus_inf, l1)
    m2 = jnp.max(l2, axis=-1, keepdims=True)
    i2 = jnp.min(jnp.where(l2 == m2, lane, ROUTER_LANES), axis=-1, keepdims=True)
    e2 = jnp.exp(m2 - m1)
    den = 1.0 + e2
    rec = jnp.where(lane == 0, i1.astype(F32),
                    jnp.where(lane == 1, i2.astype(F32),
                              jnp.where(lane == 2, 1.0 / den,
                                        jnp.where(lane == 3, e2 / den, 0.0))))
    route_ref[...] = rec[:, :ROUTE_COLS]


def _router(xt, g, w_router, tm=512):
    t = xt.shape[0] // SUBLANES
    d = SUBLANES * LANES
    wr = jnp.zeros((d, ROUTER_LANES), F32).at[:, :N_EXPERTS].set(w_router.astype(F32))
    return pl.pallas_call(
        _router_kernel,
        grid=(t // tm,),
        in_specs=[
            pl.BlockSpec((tm * SUBLANES, LANES), lambda i: (i, 0)),
            pl.BlockSpec((1, d), lambda i: (0, 0)),
            pl.BlockSpec((d, ROUTER_LANES), lambda i: (0, 0)),
        ],
        out_specs=pl.BlockSpec((tm, ROUTE_COLS), lambda i: (i, 0)),
        out_shape=jax.ShapeDtypeStruct((t, ROUTE_COLS), F32),
        compiler_params=_cparams(("arbitrary",)),
        name="moe_router",
    )(xt, g.reshape(1, d), wr)


def _moe_plan(route, tm, n_tiles):
    experts = route[:, :2].astype(jnp.int32).reshape(-1)
    onehot = (experts[:, None] == jnp.arange(N_EXPERTS, dtype=jnp.int32)[None, :]).astype(jnp.int32)
    csum = jnp.cumsum(onehot, axis=0)
    rank = jnp.sum((csum - onehot) * onehot, axis=1)
    count = csum[-1]
    padded = ((count + tm - 1) // tm) * tm
    end = jnp.cumsum(padded)
    start = end - padded
    slot = jnp.sum(onehot * start[None, :], axis=1) + rank
    n_active = end[-1] // tm
    tile_start = jnp.arange(n_tiles, dtype=jnp.int32) * tm
    tile_expert = jnp.sum((tile_start[:, None] >= end[None, :]).astype(jnp.int32), axis=1)
    tile_expert = jnp.minimum(tile_expert, N_EXPERTS - 1)
    last = tile_expert[jnp.maximum(n_active - 1, 0)]
    tile_expert = jnp.where(jnp.arange(n_tiles) < n_active, tile_expert, last)
    return slot.astype(jnp.int32), tile_expert.astype(jnp.int32), n_active.astype(jnp.int32).reshape(1)


def _token_copy(src, src_tok, dst, dst_tok, sem):
    rows = lambda tok: pl.ds(pl.multiple_of(tok * SUBLANES, SUBLANES), SUBLANES)
    return pltpu.make_async_copy(src.at[rows(src_tok)], dst.at[rows(dst_tok)], sem)


def _invert_kernel(slot_ref, word_ref, pad_hbm, plan_ref, sem):
    @pl.when(pl.program_id(0) == 0)
    def _():
        init = pltpu.make_async_copy(pad_hbm, plan_ref, sem)
        init.start()
        init.wait()

    def place(a, carry):
        plan_ref[slot_ref[0, 0, a]] = word_ref[0, 0, a]
        return carry

    lax.fori_loop(0, slot_ref.shape[-1], place, 0, unroll=8)


def _plan_bits(n_tok, tm):
    src_bits = (n_tok - 1).bit_length()
    assert (2 * n_tok + 2 * tm) << src_bits < 2 ** 31 and tm & (tm - 1) == 0
    return src_bits


def _invert(slot, n_slots, tm, n_tok, chunk=4096):
    src_bits = _plan_bits(n_tok, tm)
    a = jnp.arange(slot.shape[0], dtype=jnp.int32)
    tok = a >> 1
    word = tok + (((a & 1) * n_tok + tok) << src_bits)
    s = jnp.arange(n_slots, dtype=jnp.int32)
    pad_plan = (2 * n_tok + (s & (2 * tm - 1))) << src_bits
    n_chunks = slot.shape[0] // chunk
    chunked = pl.BlockSpec((1, 1, chunk), lambda c: (c, 0, 0), memory_space=pltpu.SMEM)
    return pl.pallas_call(
        _invert_kernel,
        grid=(n_chunks,),
        in_specs=[chunked, chunked, pl.BlockSpec(memory_space=pl.ANY)],
        out_specs=pl.BlockSpec((n_slots,), lambda c: (0,), memory_space=pltpu.SMEM),
        out_shape=jax.ShapeDtypeStruct((n_slots,), jnp.int32),
        scratch_shapes=[pltpu.SemaphoreType.DMA(())],
        compiler_params=pltpu.CompilerParams(dimension_semantics=("arbitrary",)),
        name="moe_invert",
    )(slot.reshape(n_chunks, 1, chunk), word.reshape(n_chunks, 1, chunk), pad_plan)


def _tokens_done(ref, n_tok, sem):
    rows = pl.ds(0, n_tok * SUBLANES)
    pltpu.make_async_copy(ref.at[rows], ref.at[rows], sem).wait()


def _moe_ffn_kernel(te_ref, nact_ref, plan_ref, x_hbm, g_ref, wg_ref, wu_ref, wd_ref, yk_hbm,
                    xbuf, xb16, ybuf, acc_ref, sem_in, sem_out, *, tm, n_tok, n_f, pad_tile, src_bits):
    del te_ref
    i = pl.program_id(0)
    f = pl.program_id(1)
    par = i % 2
    nact = nact_ref[0]
    per_step = tm // n_f

    def fetch(tile, p, j):
        src = plan_ref[tile * tm + j] & ((1 << src_bits) - 1)
        _token_copy(x_hbm, src, xbuf.at[p], j, sem_in.at[p]).start(priority=1)

    def send(tile, p, j):
        dst = plan_ref[tile * tm + j] >> src_bits
        _token_copy(ybuf.at[p], j, yk_hbm, dst, sem_out.at[p]).start()

    @pl.when((i == 0) & (f == 0))
    def _():
        def first(j, carry):
            fetch(0, 0, j)
            return carry
        lax.fori_loop(0, tm, first, 0)
        ybuf[...] = jnp.zeros_like(ybuf)
        for p in range(2):
            spare_rows = pl.ds((2 * n_tok + p * tm) * SUBLANES, tm * SUBLANES)
            spare = pltpu.make_async_copy(ybuf.at[p], yk_hbm.at[spare_rows], sem_out.at[p])
            spare.start()
            spare.wait()
        _tokens_done(xbuf.at[0], tm, sem_in.at[0])

    @pl.when((i >= 1) & (i <= nact) & (f == 0))
    def _():
        _tokens_done(xbuf.at[par], tm, sem_in.at[par])

    @pl.when(i < nact)
    def _():
        @pl.when(f == 0)
        def _():
            xb16[...] = _rms(_load_token_tiles(xbuf.at[par], tm), g_ref[...]).astype(BF16)
            acc_ref[...] = jnp.zeros_like(acc_ref)

        prev = jnp.where(i > 0, i - 1, pad_tile)
        for jj in range(per_step):
            j = f * per_step + jj
            fetch(i + 1, 1 - par, j)
            send(prev, 1 - par, j)

        x = xb16[...]
        gate = jnp.dot(x, wg_ref[...], preferred_element_type=F32)
        up = jnp.dot(x, wu_ref[...], preferred_element_type=F32)
        act = (gate * jax.nn.sigmoid(gate) * up).astype(BF16)
        acc_ref[...] += jnp.dot(act, wd_ref[...], preferred_element_type=F32)

        @pl.when(f == n_f - 1)
        def _():
            @pl.when(i >= 1)
            def _():
                _tokens_done(ybuf.at[par], tm, sem_out.at[par])

            _store_token_tiles(ybuf.at[par], acc_ref[...])

    @pl.when((i == nact) & (f == 0))
    def _():
        def last(j, carry):
            send(i - 1, 1 - par, j)
            return carry
        lax.fori_loop(0, tm, last, 0)
        _tokens_done(ybuf.at[0], tm, sem_out.at[0])
        _tokens_done(ybuf.at[1], tm, sem_out.at[1])


def _moe_ffn(xt, g, plan, tile_expert, n_active, wg, wu, wd, tm):
    n_tok = xt.shape[0] // SUBLANES
    d = SUBLANES * LANES
    n_tiles = tile_expert.shape[0]
    pad_tile = n_tiles - 1 if (n_tiles - 1) % 2 else n_tiles
    assert plan.shape[0] >= (pad_tile + 1) * tm
    dff = wg.shape[-1]
    n_f = 2
    tf = dff // n_f
    fidx = lambda i, f, na: jnp.where(i < na[0], f, n_f - 1)
    grid_spec = pltpu.PrefetchScalarGridSpec(
        num_scalar_prefetch=3,
        grid=(n_tiles, n_f),
        in_specs=[
            pl.BlockSpec(memory_space=pl.ANY),
            pl.BlockSpec((1, d), lambda i, f, te, na, plan: (0, 0)),
            pl.BlockSpec((None, d, tf), lambda i, f, te, na, plan: (te[i], 0, fidx(i, f, na))),
            pl.BlockSpec((None, d, tf), lambda i, f, te, na, plan: (te[i], 0, fidx(i, f, na))),
            pl.BlockSpec((None, tf, d), lambda i, f, te, na, plan: (te[i], fidx(i, f, na), 0)),
        ],
        out_specs=pl.BlockSpec(memory_space=pl.ANY),
        scratch_shapes=[
            pltpu.VMEM((2, tm * SUBLANES, LANES), F32),
            pltpu.VMEM((tm, d), BF16),
            pltpu.VMEM((2, tm * SUBLANES, LANES), F32),
            pltpu.VMEM((tm, d), F32),
            pltpu.SemaphoreType.DMA((2,)),
            pltpu.SemaphoreType.DMA((2,)),
        ],
    )
    return pl.pallas_call(
        functools.partial(_moe_ffn_kernel, tm=tm, n_tok=n_tok, n_f=n_f, pad_tile=pad_tile,
                          src_bits=_plan_bits(n_tok, tm)),
        grid_spec=grid_spec,
        out_shape=jax.ShapeDtypeStruct(((2 * n_tok + 2 * tm) * SUBLANES, LANES), F32),
        compiler_params=_cparams(("arbitrary", "arbitrary")),
        name="moe_ffn",
    )(tile_expert, n_active, plan, xt, g.reshape(1, d), wg, wu, wd)


def _combine_kernel(route_ref, x_ref, y0_ref, y1_ref, fg_ref, o_ref, *, final_norm):
    r = route_ref[...]
    tm = r.shape[0]
    y = (_load_token_tiles(x_ref, tm) + r[:, 2:3] * _load_token_tiles(y0_ref, tm)
         + r[:, 3:4] * _load_token_tiles(y1_ref, tm))
    o_ref[...] = _rms(y, fg_ref[...]) if final_norm else y


def _combine(yk, route, xt, final_g, final_norm, tm=512):
    t = xt.shape[0] // SUBLANES
    d = SUBLANES * LANES
    n_blk = t // tm
    tiles = lambda f: pl.BlockSpec((tm * SUBLANES, LANES), f)
    return pl.pallas_call(
        functools.partial(_combine_kernel, final_norm=final_norm),
        grid=(n_blk,),
        in_specs=[
            pl.BlockSpec((tm, ROUTE_COLS), lambda i: (i, 0)),
            tiles(lambda i: (i, 0)),
            tiles(lambda i: (i, 0)),
            tiles(lambda i: (n_blk + i, 0)),
            pl.BlockSpec((1, d), lambda i: (0, 0)),
        ],
        out_specs=pl.BlockSpec((tm, d), lambda i: (i, 0)),
        out_shape=jax.ShapeDtypeStruct((t, d), F32),
        compiler_params=_cparams(("arbitrary",)),
        name="moe_combine",
    )(route, xt, yk, yk, final_g.reshape(1, d))


def _final_norm_kernel(x_ref, g_ref, o_ref):
    o_ref[...] = _rms(x_ref[...], g_ref[...])


def _final_norm(x2, g, tm=512):
    t, d = x2.shape
    return pl.pallas_call(
        _final_norm_kernel,
        grid=(t // tm,),
        in_specs=[pl.BlockSpec((tm, d), lambda i: (i, 0)), pl.BlockSpec((1, d), lambda i: (0, 0))],
        out_specs=pl.BlockSpec((tm, d), lambda i: (i, 0)),
        out_shape=jax.ShapeDtypeStruct((t, d), F32),
        compiler_params=_cparams(("arbitrary",)),
        name="final_norm",
    )(x2, g.reshape(1, d))


MOE_TM = 512


def kernel(x, norm1_g, w_in, na_rpb, sc_conv_w, cf_conv_w, cf_conv_b, cf_ln_g, cf_ln_b, ssm_a_re, ssm_a_im,
           ssm_log_dt, ssm_b_re, ssm_b_im, ssm_c_re, ssm_c_im, ssm_d, ssm_w_glu, ssm_b_glu, grp_norm_g, w_out,
           norm2_g, ffn_w_gate, ffn_w_up, ffn_w_down, moe_w_router, moe_w_gate, moe_w_up, moe_w_down,
           final_norm_g):
    bsz, seq, d = x.shape
    depth = w_in.shape[0]
    t = bsz * seq
    rows = seq // GRID_W
    x2 = x.reshape(t, d).astype(F32)
    final_done = False
    for i in range(depth):
        qkv, rest = _in_proj(x2, norm1_g[i], w_in[i].astype(BF16))
        y_na = _na_attention(qkv, _na_bias_table(na_rpb[i], rows), bsz, seq)
        y_sc, y_cf = _conv_mixers(rest, sc_conv_w[i], cf_conv_w[i], cf_conv_b[i], cf_ln_g[i], cf_ln_b[i],
                                  bsz, seq)
        bmat, cmat, amat = _s5_discretise(ssm_a_re[i], ssm_a_im[i], ssm_log_dt[i], ssm_b_re[i], ssm_b_im[i],
                                          ssm_c_re[i], ssm_c_im[i], bsz)
        y_ssm = _s5_mixer(rest, SSM_COL, bmat, cmat, amat, ssm_d[i], ssm_w_glu[i].astype(BF16), ssm_b_glu[i],
                          bsz, seq)
        x2 = _out_proj((y_na, y_sc, y_cf, y_ssm), grp_norm_g[i], w_out[i].astype(BF16), x2,
                       token_tiles=(i % 2 == 1))
        j = i // 2
        if i % 2 == 0:
            x2 = _dense_ffn(x2, norm2_g[i], ffn_w_gate[j].astype(BF16), ffn_w_up[j].astype(BF16),
                            ffn_w_down[j].astype(BF16))
        else:
            route = _router(x2, norm2_g[i], moe_w_router[j])
            n_tiles = (2 * t) // MOE_TM + N_EXPERTS + 1
            slot, tile_expert, n_active = _moe_plan(route, MOE_TM, n_tiles)
            plan = _invert(slot, (n_tiles + 1) * MOE_TM, MOE_TM, t)
            yk = _moe_ffn(x2, norm2_g[i], plan, tile_expert, n_active, moe_w_gate[j].astype(BF16),
                          moe_w_up[j].astype(BF16), moe_w_down[j].astype(BF16), MOE_TM)
            final_done = i == depth - 1
            x2 = _combine(yk, route, x2, final_norm_g, final_done)
    if not final_done:
        x2 = _final_norm(x2, final_norm_g)
    return x2.reshape(bsz, seq, d)
```

```python
import functools
import math

import jax
import jax.numpy as jnp
import numpy as np
from jax import lax
from jax.experimental import pallas as pl
from jax.experimental.pallas import tpu as pltpu

F32 = jnp.float32
BF16 = jnp.bfloat16
EPS = 1e-6
NEG_INF = -1e30

GRID_W = 64
NA_HEAD_DIM = 64
NA_WIN_H = 8
NA_WIN_W = 16
GROUP_W = 256
SSM_GROUP_CH = 16
SSM_STATE = 64
N_EXPERTS = 8
VMEM_LIMIT = 56 * 1024 * 1024


def _cparams(sem, vmem=VMEM_LIMIT):
    return pltpu.CompilerParams(dimension_semantics=sem, vmem_limit_bytes=vmem)


def _rms(x, g):
    return x * lax.rsqrt(jnp.mean(x * x, axis=-1, keepdims=True) + EPS) * g


SUBLANES = 8
LANES = 128


def _load_token_tiles(ref, n):
    return jnp.concatenate([ref[pl.ds(s, n, stride=SUBLANES), :] for s in range(SUBLANES)], axis=1)


def _store_token_tiles(ref, val):
    for s in range(SUBLANES):
        ref[pl.ds(s, val.shape[0], stride=SUBLANES), :] = val[:, s * LANES:(s + 1) * LANES]


N_QKV_COLS = 3 * GROUP_W
SSM_COL = 5


def _in_proj_kernel(x_ref, g_ref, w_ref, qkv_ref, rest_ref):
    h = _rms(x_ref[...], g_ref[...]).astype(BF16)
    n_qkv = qkv_ref.shape[1] // GROUP_W
    n_rest = rest_ref.shape[1] // GROUP_W
    for j in range(n_qkv + n_rest):
        z = jnp.dot(h, w_ref[:, j * GROUP_W:(j + 1) * GROUP_W], preferred_element_type=F32)
        if j < n_qkv:
            qkv_ref[:, j * GROUP_W:(j + 1) * GROUP_W] = z.astype(BF16)
        else:
            k = j - n_qkv
            rest_ref[:, k * GROUP_W:(k + 1) * GROUP_W] = z


def _in_proj(x2, g, w, tm=512):
    t, d = x2.shape
    n_cols = w.shape[1]
    n_rest = n_cols - N_QKV_COLS
    return pl.pallas_call(
        _in_proj_kernel,
        grid=(t // tm,),
        in_specs=[
            pl.BlockSpec((tm, d), lambda i: (i, 0)),
            pl.BlockSpec((1, d), lambda i: (0, 0)),
            pl.BlockSpec((d, n_cols), lambda i: (0, 0)),
        ],
        out_specs=[
            pl.BlockSpec((tm, N_QKV_COLS), lambda i: (i, 0)),
            pl.BlockSpec((tm, n_rest), lambda i: (i, 0)),
        ],
        out_shape=[
            jax.ShapeDtypeStruct((t, N_QKV_COLS), BF16),
            jax.ShapeDtypeStruct((t, n_rest), F32),
        ],
        compiler_params=_cparams(("arbitrary",)),
        name="in_proj",
    )(x2, g.reshape(1, d), w)


def _na_bias_table(rpb, rows):
    n_heads = rpb.shape[0]
    kh = NA_WIN_H
    c = np.arange(GRID_W)
    qcs = np.clip(c - NA_WIN_W // 2, 0, GRID_W - NA_WIN_W)
    kc = np.arange(GRID_W)
    in_win = (kc[None, :] >= qcs[:, None]) & (kc[None, :] < qcs[:, None] + NA_WIN_W)
    dc_idx = np.clip(kc[None, :] - c[:, None] + NA_WIN_W - 1, 0, 2 * NA_WIN_W - 2)
    onehot_dc = (dc_idx[:, :, None] == np.arange(2 * NA_WIN_W - 1)).astype(np.float32)
    rep_rows = np.array([0, 1, 2, 3, 4, rows - 3, rows - 2, rows - 1])
    row_start = np.clip(rep_rows - kh // 2, 0, rows - kh)
    dr_idx = (row_start - rep_rows + NA_WIN_H - 1)[:, None] + np.arange(kh)[None, :]
    onehot_dr = (dr_idx[:, :, None] == np.arange(2 * NA_WIN_H - 1)).astype(np.float32)
    tab = jnp.einsum("via,hab,ckb->vhcik", jnp.asarray(onehot_dr), rpb.astype(F32), jnp.asarray(onehot_dc),
                     precision=lax.Precision.HIGHEST)
    tab = jnp.where(jnp.asarray(in_win)[None, None, :, None, :], tab, NEG_INF)
    return tab.reshape(len(rep_rows), n_heads * GRID_W, kh * GRID_W)


def _na_kernel(q_ref, k_ref, v_ref, bias_ref, o_ref, *, rows, n_heads):
    kh = NA_WIN_H
    lane = lax.broadcasted_iota(jnp.int32, (GRID_W, n_heads * NA_HEAD_DIM), 1)
    masks = [(lane >= h * NA_HEAD_DIM) & (lane < (h + 1) * NA_HEAD_DIM) for h in range(n_heads)]
    scale = NA_HEAD_DIM ** -0.5

    def body(r, carry):
        q0 = pl.multiple_of(r * GRID_W, GRID_W)
        q = q_ref[pl.ds(q0, GRID_W), :].astype(F32) * scale
        qs = jnp.concatenate([jnp.where(m, q, 0.0) for m in masks], axis=0).astype(BF16)
        rs = jnp.clip(r - kh // 2, 0, rows - kh)
        k0 = pl.multiple_of(rs * GRID_W, GRID_W)
        ks = k_ref[pl.ds(k0, kh * GRID_W), :]
        vs = v_ref[pl.ds(k0, kh * GRID_W), :]
        s = lax.dot_general(qs, ks, (((1,), (1,)), ((), ())), preferred_element_type=F32)
        var = jnp.where(r < kh // 2, r, jnp.where(r > rows - kh // 2, r - (rows - 8), kh // 2))
        s = s + bias_ref[var]
        m = jnp.max(s, axis=-1, keepdims=True)
        p = jnp.exp(s - m)
        den = jnp.sum(p, axis=-1, keepdims=True)
        o = jnp.dot(p.astype(BF16), vs, preferred_element_type=F32) / den
        out = jnp.where(masks[0], o[0:GRID_W], 0.0)
        for h in range(1, n_heads):
            out = out + jnp.where(masks[h], o[h * GRID_W:(h + 1) * GRID_W], 0.0)
        o_ref[pl.ds(q0, GRID_W), :] = out
        return carry

    lax.fori_loop(0, rows, body, 0, unroll=2)


def _na_attention(qkv, bias, bsz, seq):
    rows = seq // GRID_W
    n_heads = GROUP_W // NA_HEAD_DIM
    t = qkv.shape[0]
    blk = lambda j: pl.BlockSpec((seq, GROUP_W), lambda b, j=j: (b, j))
    return pl.pallas_call(
        functools.partial(_na_kernel, rows=rows, n_heads=n_heads),
        grid=(bsz,),
        in_specs=[blk(0), blk(1), blk(2),
                  pl.BlockSpec(bias.shape, lambda b: (0, 0, 0))],
        out_specs=pl.BlockSpec((seq, GROUP_W), lambda b: (b, 0)),
        out_shape=jax.ShapeDtypeStruct((t, GROUP_W), F32),
        compiler_params=_cparams(("arbitrary",)),
        name="na_attention",
    )(qkv, qkv, qkv, bias)


CONV_PAD = 16
CONV_CHUNK = 128


def _conv_kernel(b_ref, c_ref, x_ref, a_ref, g_ref, scw_ref, cfw_ref, cfb_ref, lng_ref, lnb_ref,
                 ysc_ref, ycf_ref, psc, pcf):
    seq = b_ref.shape[0]
    n_chunks = seq // CONV_CHUNK
    zeros = jnp.zeros((CONV_PAD, GROUP_W), F32)
    psc[0:CONV_PAD, :] = zeros
    pcf[0:CONV_PAD, :] = zeros
    psc[CONV_PAD + seq:2 * CONV_PAD + seq, :] = zeros
    pcf[CONV_PAD + seq:2 * CONV_PAD + seq, :] = zeros

    def fill(i, carry):
        r0 = pl.multiple_of(i * CONV_CHUNK, CONV_CHUNK)
        rows = pl.ds(r0, CONV_CHUNK)
        dst = pl.ds(r0 + CONV_PAD, CONV_CHUNK)
        psc[dst, :] = c_ref[rows, :] * x_ref[rows, :]
        pcf[dst, :] = a_ref[rows, :] * jax.nn.sigmoid(g_ref[rows, :])
        return carry

    lax.fori_loop(0, n_chunks, fill, 0)

    def taps(win, w_ref):
        n_taps = w_ref.shape[0]
        offs = [CONV_PAD + k - n_taps // 2 for k in range(n_taps)]
        n_win = win.shape[0]
        acc = None
        for s in range(SUBLANES):
            ks = [k for k in range(n_taps) if offs[k] % SUBLANES == s]
            if not ks:
                continue
            shifted = pltpu.roll(win, n_win - s, axis=0) if s else win
            for k in ks:
                q = offs[k] - s
                term = shifted[q:q + CONV_CHUNK, :] * w_ref[k:k + 1, :]
                acc = term if acc is None else acc + term
        return acc

    def conv(i, carry):
        r0 = pl.multiple_of(i * CONV_CHUNK, CONV_CHUNK)
        rows = pl.ds(r0, CONV_CHUNK)
        window = pl.ds(r0, CONV_CHUNK + 2 * CONV_PAD)
        ysc_ref[rows, :] = b_ref[rows, :] * taps(psc[window, :], scw_ref)
        cf = taps(pcf[window, :], cfw_ref) + cfb_ref[...]
        mu = jnp.mean(cf, axis=-1, keepdims=True)
        xc = cf - mu
        var = jnp.mean(xc * xc, axis=-1, keepdims=True)
        ln = xc * lax.rsqrt(var + EPS) * lng_ref[...] + lnb_ref[...]
        ycf_ref[rows, :] = ln * jax.nn.sigmoid(ln)
        return carry

    lax.fori_loop(0, n_chunks, conv, 0)


def _conv_mixers(rest, sc_w, cf_w, cf_b, ln_g, ln_b, bsz, seq):
    t = rest.shape[0]
    blk = lambda j: pl.BlockSpec((seq, GROUP_W), lambda b, j=j: (b, j))
    full = lambda a: pl.BlockSpec(a.shape, lambda b: (0,) * a.ndim)
    cf_b, ln_g, ln_b = (a.reshape(1, GROUP_W) for a in (cf_b, ln_g, ln_b))
    out_spec = pl.BlockSpec((seq, GROUP_W), lambda b: (b, 0))
    return pl.pallas_call(
        _conv_kernel,
        grid=(bsz,),
        in_specs=[blk(0), blk(1), blk(2), blk(3), blk(4),
                  full(sc_w), full(cf_w), full(cf_b), full(ln_g), full(ln_b)],
        out_specs=[out_spec, out_spec],
        out_shape=[jax.ShapeDtypeStruct((t, GROUP_W), F32)] * 2,
        scratch_shapes=[pltpu.VMEM((seq + 2 * CONV_PAD, GROUP_W), F32)] * 2,
        compiler_params=_cparams(("arbitrary",)),
        name="conv_mixers",
    )(rest, rest, rest, rest, rest, sc_w, cf_w, cf_b, ln_g, ln_b)


S5_TL = 128
S5_LANE_SPLIT = 2


def _s5_discretise(a_re, a_im, log_dt, b_re, b_im, c_re, c_im, bsz):
    f32 = F32
    a_re, a_im, log_dt = a_re.astype(f32), a_im.astype(f32), log_dt.astype(f32)
    n_dir, n_grp, n_state = a_re.shape
    n_ch = b_re.shape[-1]
    dt = jnp.exp(log_dt)[..., None]
    mag = jnp.exp(a_re * dt)
    abr = mag * jnp.cos(a_im * dt)
    abi = mag * jnp.sin(a_im * dt)
    den = a_re * a_re + a_im * a_im
    fr = ((abr - 1.0) * a_re + abi * a_im) / den
    fi = (abi * a_re - (abr - 1.0) * a_im) / den
    bbr = fr[..., None] * b_re - fi[..., None] * b_im
    bbi = fr[..., None] * b_im + fi[..., None] * b_re
    eye = jnp.eye(n_grp, dtype=f32)
    blk_b = lambda m: jnp.einsum("dgps,gh->dgshp", m, eye).reshape(n_dir, n_grp * n_ch, n_grp * n_state)
    blk_c = lambda m: jnp.einsum("dgsp,gh->dgphs", m, eye).reshape(n_dir, n_grp * n_state, n_grp * n_ch)
    bmat = jnp.concatenate([blk_b(bbr), blk_b(bbi)], axis=-1)
    cmat = jnp.concatenate([blk_c(c_re.astype(f32)), blk_c(-c_im.astype(f32))], axis=1)
    amat = jnp.stack([abr.reshape(n_dir, -1), abi.reshape(n_dir, -1)], axis=1)
    amat = jnp.broadcast_to(amat[:, :, None, :], (n_dir, 2, bsz, n_grp * n_state))
    return bmat.astype(BF16), cmat.astype(BF16), amat


def _s5_kernel(*refs, bsz, backward):
    if backward:
        (u_ref, b_ref, c_ref, a_ref, yf_ref, d_ref, w_ref, bg_ref, o_ref, xs, st, tb) = refs
    else:
        (u_ref, b_ref, c_ref, a_ref, o_ref, xs, st, tb) = refs
    n_state = a_ref.shape[-1]

    @pl.when(pl.program_id(0) == 0)
    def _():
        st[...] = jnp.zeros_like(st)

    n_half = GROUP_W // LANES
    for b in range(bsz):
        for h in range(n_half):
            tb[h, pl.ds(b, S5_TL, stride=bsz), :] = u_ref[b, :, h * LANES:(h + 1) * LANES]
    u_tb = jnp.concatenate([tb[h] for h in range(n_half)], axis=1)
    xs[...] = jnp.dot(u_tb.astype(BF16), b_ref[...], preferred_element_type=F32)

    cw = n_state // S5_LANE_SPLIT
    for c in range(S5_LANE_SPLIT):
        re = slice(c * cw, (c + 1) * cw)
        im = slice(n_state + c * cw, n_state + (c + 1) * cw)
        a_r = a_ref[0, :, re]
        a_i = a_ref[1, :, re]

        def step(t, carry, re=re, im=im, a_r=a_r, a_i=a_i):
            s_r, s_i = carry
            tt = S5_TL - 1 - t if backward else t
            rows = pl.ds(pl.multiple_of(tt * bsz, bsz), bsz)
            n_r = a_r * s_r - a_i * s_i + xs[rows, re]
            n_i = a_r * s_i + a_i * s_r + xs[rows, im]
            xs[rows, re] = n_r
            xs[rows, im] = n_i
            return n_r, n_i

        s_r, s_i = lax.fori_loop(0, S5_TL, step, (st[:, re], st[:, im]), unroll=4)
        st[:, re] = s_r
        st[:, im] = s_i

    y_tb = jnp.dot(xs[...].astype(BF16), c_ref[...], preferred_element_type=F32)
    if not backward:
        o_ref[...] = y_tb
        return
    y_sum = y_tb + yf_ref[...]
    for h in range(n_half):
        tb[h] = y_sum[:, h * LANES:(h + 1) * LANES]
    y_bt = jnp.concatenate(
        [jnp.concatenate([tb[h, pl.ds(b, S5_TL, stride=bsz), :] for h in range(n_half)], axis=1)
         for b in range(bsz)], axis=0)
    y = jax.nn.gelu(d_ref[...] * u_ref[...].reshape(S5_TL * bsz, GROUP_W) + y_bt)
    gate = jnp.dot(y.astype(BF16), w_ref[...], preferred_element_type=F32) + bg_ref[...]
    o_ref[...] = (y * jax.nn.sigmoid(gate)).reshape(bsz, S5_TL, GROUP_W)


def _s5_mixer(rest, ssm_col, bmat, cmat, amat, d_skip, w_glu, b_glu, bsz, seq):
    n_chunks = seq // S5_TL
    n_rows = S5_TL * bsz
    n_st2 = bmat.shape[-1]
    rest3 = rest.reshape(bsz, seq, rest.shape[1])
    full = lambda a: pl.BlockSpec(a.shape, lambda i: (0,) * a.ndim)
    scratch = [pltpu.VMEM((n_rows, n_st2), F32),
               pltpu.VMEM((bsz, n_st2), F32),
               pltpu.VMEM((GROUP_W // LANES, n_rows, LANES), F32)]

    def direction(d, chunk_of):
        return [
            pl.BlockSpec((bsz, S5_TL, GROUP_W), lambda i: (0, chunk_of(i), ssm_col)),
            pl.BlockSpec((None, GROUP_W, n_st2), lambda i: (d, 0, 0)),
            pl.BlockSpec((None, n_st2, GROUP_W), lambda i: (d, 0, 0)),
            pl.BlockSpec((None, 2, bsz, n_st2 // 2), lambda i: (d, 0, 0, 0)),
        ]

    y_fwd = pl.pallas_call(
        functools.partial(_s5_kernel, bsz=bsz, backward=False),
        grid=(n_chunks,),
        in_specs=direction(0, lambda i: i),
        out_specs=pl.BlockSpec((n_rows, GROUP_W), lambda i: (i, 0)),
        out_shape=jax.ShapeDtypeStruct((seq * bsz, GROUP_W), F32),
        scratch_shapes=scratch,
        compiler_params=_cparams(("arbitrary",)),
        name="s5_forward",
    )(rest3, bmat, cmat, amat)

    rev = lambda i: n_chunks - 1 - i
    d_skip, b_glu = d_skip.reshape(1, GROUP_W), b_glu.reshape(1, GROUP_W)
    out = pl.pallas_call(
        functools.partial(_s5_kernel, bsz=bsz, backward=True),
        grid=(n_chunks,),
        in_specs=direction(1, rev) + [
            pl.BlockSpec((n_rows, GROUP_W), lambda i: (rev(i), 0)),
            full(d_skip), full(w_glu), full(b_glu),
        ],
        out_specs=pl.BlockSpec((bsz, S5_TL, GROUP_W), lambda i: (0, rev(i), 0)),
        out_shape=jax.ShapeDtypeStruct((bsz, seq, GROUP_W), F32),
        scratch_shapes=scratch,
        compiler_params=_cparams(("arbitrary",)),
        name="s5_backward",
    )(rest3, bmat, cmat, amat, y_fwd, d_skip, w_glu, b_glu)
    return out.reshape(bsz * seq, GROUP_W)


def _out_proj_kernel(y0_ref, y1_ref, y2_ref, y3_ref, gn_ref, w_ref, x_ref, o_ref, *, token_tiles):
    acc = x_ref[...]
    for j, y_ref in enumerate((y0_ref, y1_ref, y2_ref, y3_ref)):
        yn = _rms(y_ref[...], gn_ref[j:j + 1, :]).astype(BF16)
        acc = acc + jnp.dot(yn, w_ref[j * GROUP_W:(j + 1) * GROUP_W, :], preferred_element_type=F32)
    if token_tiles:
        _store_token_tiles(o_ref, acc)
    else:
        o_ref[...] = acc


def _out_proj(ys, gn, w, x2, token_tiles, tm=512):
    t, d = x2.shape
    yspec = pl.BlockSpec((tm, GROUP_W), lambda i: (i, 0))
    out_block, out_rows = ((tm * SUBLANES, LANES), t * SUBLANES) if token_tiles else ((tm, d), t)
    return pl.pallas_call(
        functools.partial(_out_proj_kernel, token_tiles=token_tiles),
        grid=(t // tm,),
        in_specs=[yspec] * 4 + [
            pl.BlockSpec((4, GROUP_W), lambda i: (0, 0)),
            pl.BlockSpec(w.shape, lambda i: (0, 0)),
            pl.BlockSpec((tm, d), lambda i: (i, 0)),
        ],
        out_specs=pl.BlockSpec(out_block, lambda i: (i, 0)),
        out_shape=jax.ShapeDtypeStruct((out_rows, out_block[1]), F32),
        compiler_params=_cparams(("arbitrary",)),
        name="out_proj",
    )(*ys, gn.reshape(4, GROUP_W), w, x2)


def _ffn_kernel(x_ref, g_ref, wg_ref, wu_ref, wd_ref, o_ref, h_ref, acc_ref):
    f = pl.program_id(1)

    @pl.when(f == 0)
    def _():
        h_ref[...] = _rms(x_ref[...], g_ref[...]).astype(BF16)
        acc_ref[...] = x_ref[...]

    h = h_ref[...]
    gate = jnp.dot(h, wg_ref[...], preferred_element_type=F32)
    up = jnp.dot(h, wu_ref[...], preferred_element_type=F32)
    act = (gate * jax.nn.sigmoid(gate) * up).astype(BF16)
    acc_ref[...] += jnp.dot(act, wd_ref[...], preferred_element_type=F32)

    @pl.when(f == pl.num_programs(1) - 1)
    def _():
        o_ref[...] = acc_ref[...]


def _dense_ffn(x2, g, wg, wu, wd, tm=512, tf=1408):
    t, d = x2.shape
    dff = wg.shape[1]
    return pl.pallas_call(
        _ffn_kernel,
        grid=(t // tm, dff // tf),
        in_specs=[
            pl.BlockSpec((tm, d), lambda i, f: (i, 0)),
            pl.BlockSpec((1, d), lambda i, f: (0, 0)),
            pl.BlockSpec((d, tf), lambda i, f: (0, f)),
            pl.BlockSpec((d, tf), lambda i, f: (0, f)),
            pl.BlockSpec((tf, d), lambda i, f: (f, 0)),
        ],
        out_specs=pl.BlockSpec((tm, d), lambda i, f: (i, 0)),
        out_shape=jax.ShapeDtypeStruct((t, d), F32),
        scratch_shapes=[pltpu.VMEM((tm, d), BF16), pltpu.VMEM((tm, d), F32)],
        compiler_params=_cparams(("arbitrary", "arbitrary")),
        name="dense_ffn",
    )(x2, g.reshape(1, d), wg, wu, wd)


ROUTE_COLS = 8
ROUTER_LANES = 128


def _router_kernel(x_ref, g_ref, wr_ref, route_ref):
    h = _rms(_load_token_tiles(x_ref, route_ref.shape[0]), g_ref[...])
    logits = jnp.dot(h, wr_ref[...], preferred_element_type=F32, precision=lax.Precision.HIGHEST)
    lane = lax.broadcasted_iota(jnp.int32, logits.shape, 1)
    minus_inf = -jnp.inf
    l1 = jnp.where(lane < N_EXPERTS, logits, minus_inf)
    m1 = jnp.max(l1, axis=-1, keepdims=True)
    i1 = jnp.min(jnp.where(l1 == m1, lane, ROUTER_LANES), axis=-1, keepdims=True)
    l2 = jnp.where(lane == i1, minus_inf, l1)
    m2 = jnp.max(l2, axis=-1, keepdims=True)
    i2 = jnp.min(jnp.where(l2 == m2, lane, ROUTER_LANES), axis=-1, keepdims=True)
    e2 = jnp.exp(m2 - m1)
    den = 1.0 + e2
    rec = jnp.where(lane == 0, i1.astype(F32),
                    jnp.where(lane == 1, i2.astype(F32),
                              jnp.where(lane == 2, 1.0 / den,
                                        jnp.where(lane == 3, e2 / den, 0.0))))
    route_ref[...] = rec[:, :ROUTE_COLS]


def _router(xt, g, w_router, tm=512):
    t = xt.shape[0] // SUBLANES
    d = SUBLANES * LANES
    wr = jnp.zeros((d, ROUTER_LANES), F32).at[:, :N_EXPERTS].set(w_router.astype(F32))
    return pl.pallas_call(
        _router_kernel,
        grid=(t // tm,),
        in_specs=[
            pl.BlockSpec((tm * SUBLANES, LANES), lambda i: (i, 0)),
            pl.BlockSpec((1, d), lambda i: (0, 0)),
            pl.BlockSpec((d, ROUTER_LANES), lambda i: (0, 0)),
        ],
        out_specs=pl.BlockSpec((tm, ROUTE_COLS), lambda i: (i, 0)),
        out_shape=jax.ShapeDtypeStruct((t, ROUTE_COLS), F32),
        compiler_params=_cparams(("arbitrary",)),
        name="moe_router",
    )(xt, g.reshape(1, d), wr)


def _moe_plan(route, tm, n_tiles):
    experts = route[:, :2].astype(jnp.int32).reshape(-1)
    onehot = (experts[:, None] == jnp.arange(N_EXPERTS, dtype=jnp.int32)[None, :]).astype(jnp.int32)
    csum = jnp.cumsum(onehot, axis=0)
    rank = jnp.sum((csum - onehot) * onehot, axis=1)
    count = csum[-1]
    padded = ((count + tm - 1) // tm) * tm
    end = jnp.cumsum(padded)
    start = end - padded
    slot = jnp.sum(onehot * start[None, :], axis=1) + rank
    n_active = end[-1] // tm
    tile_start = jnp.arange(n_tiles, dtype=jnp.int32) * tm
    tile_expert = jnp.sum((tile_start[:, None] >= end[None, :]).astype(jnp.int32), axis=1)
    tile_expert = jnp.minimum(tile_expert, N_EXPERTS - 1)
    last = tile_expert[jnp.maximum(n_active - 1, 0)]
    tile_expert = jnp.where(jnp.arange(n_tiles) < n_active, tile_expert, last)
    return slot.astype(jnp.int32), tile_expert.astype(jnp.int32), n_active.astype(jnp.int32).reshape(1)


def _token_copy(src, src_tok, dst, dst_tok, sem):
    rows = lambda tok: pl.ds(pl.multiple_of(tok * SUBLANES, SUBLANES), SUBLANES)
    return pltpu.make_async_copy(src.at[rows(src_tok)], dst.at[rows(dst_tok)], sem)


def _invert_kernel(slot_ref, word_ref, pad_hbm, plan_ref, sem):
    @pl.when(pl.program_id(0) == 0)
    def _():
        init = pltpu.make_async_copy(pad_hbm, plan_ref, sem)
        init.start()
        init.wait()

    def place(a, carry):
        plan_ref[slot_ref[0, 0, a]] = word_ref[0, 0, a]
        return carry

    lax.fori_loop(0, slot_ref.shape[-1], place, 0, unroll=8)


def _plan_bits(n_tok, tm):
    src_bits = (n_tok - 1).bit_length()
    assert (2 * n_tok + 2 * tm) << src_bits < 2 ** 31 and tm & (tm - 1) == 0
    return src_bits


def _invert(slot, n_slots, tm, n_tok, chunk=4096):
    src_bits = _plan_bits(n_tok, tm)
    a = jnp.arange(slot.shape[0], dtype=jnp.int32)
    tok = a >> 1
    word = tok + (((a & 1) * n_tok + tok) << src_bits)
    s = jnp.arange(n_slots, dtype=jnp.int32)
    pad_plan = (2 * n_tok + (s & (2 * tm - 1))) << src_bits
    n_chunks = slot.shape[0] // chunk
    chunked = pl.BlockSpec((1, 1, chunk), lambda c: (c, 0, 0), memory_space=pltpu.SMEM)
    return pl.pallas_call(
        _invert_kernel,
        grid=(n_chunks,),
        in_specs=[chunked, chunked, pl.BlockSpec(memory_space=pl.ANY)],
        out_specs=pl.BlockSpec((n_slots,), lambda c: (0,), memory_space=pltpu.SMEM),
        out_shape=jax.ShapeDtypeStruct((n_slots,), jnp.int32),
        scratch_shapes=[pltpu.SemaphoreType.DMA(())],
        compiler_params=pltpu.CompilerParams(dimension_semantics=("arbitrary",)),
        name="moe_invert",
    )(slot.reshape(n_chunks, 1, chunk), word.reshape(n_chunks, 1, chunk), pad_plan)


def _tokens_done(ref, n_tok, sem):
    rows = pl.ds(0, n_tok * SUBLANES)
    pltpu.make_async_copy(ref.at[rows], ref.at[rows], sem).wait()


def _moe_ffn_kernel(te_ref, nact_ref, plan_ref, x_hbm, g_ref, wg_ref, wu_ref, wd_ref, yk_hbm,
                    xbuf, xb16, ybuf, acc_ref, sem_in, sem_out, *, tm, n_tok, n_f, pad_tile, src_bits):
    del te_ref
    i = pl.program_id(0)
    f = pl.program_id(1)
    par = i % 2
    nact = nact_ref[0]

    def fetch(tile, p, j):
        src = plan_ref[tile * tm + j] & ((1 << src_bits) - 1)
        _token_copy(x_hbm, src, xbuf.at[p], j, sem_in.at[p]).start(priority=1)

    def send(tile, p, j):
        dst = plan_ref[tile * tm + j] >> src_bits
        _token_copy(ybuf.at[p], j, yk_hbm, dst, sem_out.at[p]).start()

    @pl.when((i == 0) & (f == 0))
    def _():
        def first(j, carry):
            fetch(0, 0, j)
            return carry
        lax.fori_loop(0, tm, first, 0)
        ybuf[...] = jnp.zeros_like(ybuf)
        for p in range(2):
            spare_rows = pl.ds((2 * n_tok + p * tm) * SUBLANES, tm * SUBLANES)
            spare = pltpu.make_async_copy(ybuf.at[p], yk_hbm.at[spare_rows], sem_out.at[p])
            spare.start()
            spare.wait()
        _tokens_done(xbuf.at[0], tm, sem_in.at[0])

    @pl.when((i >= 1) & (i <= nact) & (f == 0))
    def _():
        _tokens_done(xbuf.at[par], tm, sem_in.at[par])

    @pl.when(i < nact)
    def _():
        @pl.when(f == 0)
        def _():
            xb16[...] = _rms(_load_token_tiles(xbuf.at[par], tm), g_ref[...]).astype(BF16)
            acc_ref[...] = jnp.zeros_like(acc_ref)

            for j in range(tm):
                fetch(i + 1, 1 - par, j)

        @pl.when(f == n_f - 1)
        def _():
            prev = jnp.where(i > 0, i - 1, pad_tile)

            for j in range(tm):
                send(prev, 1 - par, j)

        x = xb16[...]
        gate = jnp.dot(x, wg_ref[...], preferred_element_type=F32)
        up = jnp.dot(x, wu_ref[...], preferred_element_type=F32)
        act = (gate * jax.nn.sigmoid(gate) * up).astype(BF16)
        acc_ref[...] += jnp.dot(act, wd_ref[...], preferred_element_type=F32)

        @pl.when(f == n_f - 1)
        def _():
            @pl.when(i >= 1)
            def _():
                _tokens_done(ybuf.at[par], tm, sem_out.at[par])

            _store_token_tiles(ybuf.at[par], acc_ref[...])

    @pl.when((i == nact) & (f == 0))
    def _():
        def last(j, carry):
            send(i - 1, 1 - par, j)
            return carry
        lax.fori_loop(0, tm, last, 0)
        _tokens_done(ybuf.at[0], tm, sem_out.at[0])
        _tokens_done(ybuf.at[1], tm, sem_out.at[1])


def _moe_ffn(xt, g, plan, tile_expert, n_active, wg, wu, wd, tm):
    n_tok = xt.shape[0] // SUBLANES
    d = SUBLANES * LANES
    n_tiles = tile_expert.shape[0]
    pad_tile = n_tiles - 1 if (n_tiles - 1) % 2 else n_tiles
    assert plan.shape[0] >= (pad_tile + 1) * tm
    dff = wg.shape[-1]
    n_f = 2
    tf = dff // n_f
    fidx = lambda i, f, na: jnp.where(i < na[0], f, n_f - 1)
    grid_spec = pltpu.PrefetchScalarGridSpec(
        num_scalar_prefetch=3,
        grid=(n_tiles, n_f),
        in_specs=[
            pl.BlockSpec(memory_space=pl.ANY),
            pl.BlockSpec((1, d), lambda i, f, te, na, plan: (0, 0)),
            pl.BlockSpec((None, d, tf), lambda i, f, te, na, plan: (te[i], 0, fidx(i, f, na))),
            pl.BlockSpec((None, d, tf), lambda i, f, te, na, plan: (te[i], 0, fidx(i, f, na))),
            pl.BlockSpec((None, tf, d), lambda i, f, te, na, plan: (te[i], fidx(i, f, na), 0)),
        ],
        out_specs=pl.BlockSpec(memory_space=pl.ANY),
        scratch_shapes=[
            pltpu.VMEM((2, tm * SUBLANES, LANES), F32),
            pltpu.VMEM((tm, d), BF16),
            pltpu.VMEM((2, tm * SUBLANES, LANES), F32),
            pltpu.VMEM((tm, d), F32),
            pltpu.SemaphoreType.DMA((2,)),
            pltpu.SemaphoreType.DMA((2,)),
        ],
    )
    return pl.pallas_call(
        functools.partial(_moe_ffn_kernel, tm=tm, n_tok=n_tok, n_f=n_f, pad_tile=pad_tile,
                          src_bits=_plan_bits(n_tok, tm)),
        grid_spec=grid_spec,
        out_shape=jax.ShapeDtypeStruct(((2 * n_tok + 2 * tm) * SUBLANES, LANES), F32),
        compiler_params=_cparams(("arbitrary", "arbitrary")),
        name="moe_ffn",
    )(tile_expert, n_active, plan, xt, g.reshape(1, d), wg, wu, wd)


def _combine_kernel(route_ref, x_ref, y0_ref, y1_ref, fg_ref, o_ref, *, final_norm):
    r = route_ref[...]
    tm = r.shape[0]
    y = (_load_token_tiles(x_ref, tm) + r[:, 2:3] * _load_token_tiles(y0_ref, tm)
         + r[:, 3:4] * _load_token_tiles(y1_ref, tm))
    o_ref[...] = _rms(y, fg_ref[...]) if final_norm else y


def _combine(yk, route, xt, final_g, final_norm, tm=512):
    t = xt.shape[0] // SUBLANES
    d = SUBLANES * LANES
    n_blk = t // tm
    tiles = lambda f: pl.BlockSpec((tm * SUBLANES, LANES), f)
    return pl.pallas_call(
        functools.partial(_combine_kernel, final_norm=final_norm),
        grid=(n_blk,),
        in_specs=[
            pl.BlockSpec((tm, ROUTE_COLS), lambda i: (i, 0)),
            tiles(lambda i: (i, 0)),
            tiles(lambda i: (i, 0)),
            tiles(lambda i: (n_blk + i, 0)),
            pl.BlockSpec((1, d), lambda i: (0, 0)),
        ],
        out_specs=pl.BlockSpec((tm, d), lambda i: (i, 0)),
        out_shape=jax.ShapeDtypeStruct((t, d), F32),
        compiler_params=_cparams(("arbitrary",)),
        name="moe_combine",
    )(route, xt, yk, yk, final_g.reshape(1, d))


def _final_norm_kernel(x_ref, g_ref, o_ref):
    o_ref[...] = _rms(x_ref[...], g_ref[...])


def _final_norm(x2, g, tm=512):
    t, d = x2.shape
    return pl.pallas_call(
        _final_norm_kernel,
        grid=(t // tm,),
        in_specs=[pl.BlockSpec((tm, d), lambda i: (i, 0)), pl.BlockSpec((1, d), lambda i: (0, 0))],
        out_specs=pl.BlockSpec((tm, d), lambda i: (i, 0)),
        out_shape=jax.ShapeDtypeStruct((t, d), F32),
        compiler_params=_cparams(("arbitrary",)),
        name="final_norm",
    )(x2, g.reshape(1, d))


MOE_TM = 512


def kernel(x, norm1_g, w_in, na_rpb, sc_conv_w, cf_conv_w, cf_conv_b, cf_ln_g, cf_ln_b, ssm_a_re, ssm_a_im,
           ssm_log_dt, ssm_b_re, ssm_b_im, ssm_c_re, ssm_c_im, ssm_d, ssm_w_glu, ssm_b_glu, grp_norm_g, w_out,
           norm2_g, ffn_w_gate, ffn_w_up, ffn_w_down, moe_w_router, moe_w_gate, moe_w_up, moe_w_down,
           final_norm_g):
    bsz, seq, d = x.shape
    depth = w_in.shape[0]
    t = bsz * seq
    rows = seq // GRID_W
    x2 = x.reshape(t, d).astype(F32)
    final_done = False
    for i in range(depth):
        qkv, rest = _in_proj(x2, norm1_g[i], w_in[i].astype(BF16))
        y_na = _na_attention(qkv, _na_bias_table(na_rpb[i], rows), bsz, seq)
        y_sc, y_cf = _conv_mixers(rest, sc_conv_w[i], cf_conv_w[i], cf_conv_b[i], cf_ln_g[i], cf_ln_b[i],
                                  bsz, seq)
        bmat, cmat, amat = _s5_discretise(ssm_a_re[i], ssm_a_im[i], ssm_log_dt[i], ssm_b_re[i], ssm_b_im[i],
                                          ssm_c_re[i], ssm_c_im[i], bsz)
        y_ssm = _s5_mixer(rest, SSM_COL, bmat, cmat, amat, ssm_d[i], ssm_w_glu[i].astype(BF16), ssm_b_glu[i],
                          bsz, seq)
        x2 = _out_proj((y_na, y_sc, y_cf, y_ssm), grp_norm_g[i], w_out[i].astype(BF16), x2,
                       token_tiles=(i % 2 == 1))
        j = i // 2
        if i % 2 == 0:
            x2 = _dense_ffn(x2, norm2_g[i], ffn_w_gate[j].astype(BF16), ffn_w_up[j].astype(BF16),
                            ffn_w_down[j].astype(BF16))
        else:
            route = _router(x2, norm2_g[i], moe_w_router[j])
            n_tiles = (2 * t) // MOE_TM + N_EXPERTS + 1
            slot, tile_expert, n_active = _moe_plan(route, MOE_TM, n_tiles)
            plan = _invert(slot, (n_tiles + 1) * MOE_TM, MOE_TM, t)
            yk = _moe_ffn(x2, norm2_g[i], plan, tile_expert, n_active, moe_w_gate[j].astype(BF16),
                          moe_w_up[j].astype(BF16), moe_w_down[j].astype(BF16), MOE_TM)
            final_done = i == depth - 1
            x2 = _combine(yk, route, x2, final_norm_g, final_done)
    if not final_done:
        x2 = _final_norm(x2, final_norm_g)
    return x2.reshape(bsz, seq, d)
```

```python
import functools
import math

import jax
import jax.numpy as jnp
import numpy as np
from jax import lax
from jax.experimental import pallas as pl
from jax.experimental.pallas import tpu as pltpu

F32 = jnp.float32
BF16 = jnp.bfloat16
EPS = 1e-6
NEG_INF = -1e30

GRID_W = 64
NA_HEAD_DIM = 64
NA_WIN_H = 8
NA_WIN_W = 16
GROUP_W = 256
SSM_GROUP_CH = 16
SSM_STATE = 64
N_EXPERTS = 8
VMEM_LIMIT = 56 * 1024 * 1024


def _cparams(sem, vmem=VMEM_LIMIT):
    return pltpu.CompilerParams(dimension_semantics=sem, vmem_limit_bytes=vmem)


def _rms(x, g):
    return x * lax.rsqrt(jnp.mean(x * x, axis=-1, keepdims=True) + EPS) * g


SUBLANES = 8
LANES = 128


def _load_token_tiles(ref, n):
    return jnp.concatenate([ref[pl.ds(s, n, stride=SUBLANES), :] for s in range(SUBLANES)], axis=1)


def _store_token_tiles(ref, val):
    for s in range(SUBLANES):
        ref[pl.ds(s, val.shape[0], stride=SUBLANES), :] = val[:, s * LANES:(s + 1) * LANES]


N_QKV_COLS = 3 * GROUP_W
SSM_COL = 5


def _in_proj_kernel(x_ref, g_ref, w_ref, qkv_ref, rest_ref):
    h = _rms(x_ref[...], g_ref[...]).astype(BF16)
    n_qkv = qkv_ref.shape[1] // GROUP_W
    n_rest = rest_ref.shape[1] // GROUP_W
    for j in range(n_qkv + n_rest):
        z = jnp.dot(h, w_ref[:, j * GROUP_W:(j + 1) * GROUP_W], preferred_element_type=F32)
        if j < n_qkv:
            qkv_ref[:, j * GROUP_W:(j + 1) * GROUP_W] = z.astype(BF16)
        else:
            k = j - n_qkv
            rest_ref[:, k * GROUP_W:(k + 1) * GROUP_W] = z


def _in_proj(x2, g, w, tm=512):
    t, d = x2.shape
    n_cols = w.shape[1]
    n_rest = n_cols - N_QKV_COLS
    return pl.pallas_call(
        _in_proj_kernel,
        grid=(t // tm,),
        in_specs=[
            pl.BlockSpec((tm, d), lambda i: (i, 0)),
            pl.BlockSpec((1, d), lambda i: (0, 0)),
            pl.BlockSpec((d, n_cols), lambda i: (0, 0)),
        ],
        out_specs=[
            pl.BlockSpec((tm, N_QKV_COLS), lambda i: (i, 0)),
            pl.BlockSpec((tm, n_rest), lambda i: (i, 0)),
        ],
        out_shape=[
            jax.ShapeDtypeStruct((t, N_QKV_COLS), BF16),
            jax.ShapeDtypeStruct((t, n_rest), F32),
        ],
        compiler_params=_cparams(("arbitrary",)),
        name="in_proj",
    )(x2, g.reshape(1, d), w)


def _na_bias_table(rpb, rows):
    n_heads = rpb.shape[0]
    kh = NA_WIN_H
    c = np.arange(GRID_W)
    qcs = np.clip(c - NA_WIN_W // 2, 0, GRID_W - NA_WIN_W)
    kc = np.arange(GRID_W)
    in_win = (kc[None, :] >= qcs[:, None]) & (kc[None, :] < qcs[:, None] + NA_WIN_W)
    dc_idx = np.clip(kc[None, :] - c[:, None] + NA_WIN_W - 1, 0, 2 * NA_WIN_W - 2)
    onehot_dc = (dc_idx[:, :, None] == np.arange(2 * NA_WIN_W - 1)).astype(np.float32)
    rep_rows = np.array([0, 1, 2, 3, 4, rows - 3, rows - 2, rows - 1])
    row_start = np.clip(rep_rows - kh // 2, 0, rows - kh)
    dr_idx = (row_start - rep_rows + NA_WIN_H - 1)[:, None] + np.arange(kh)[None, :]
    onehot_dr = (dr_idx[:, :, None] == np.arange(2 * NA_WIN_H - 1)).astype(np.float32)
    tab = jnp.einsum("via,hab,ckb->vhcik", jnp.asarray(onehot_dr), rpb.astype(F32), jnp.asarray(onehot_dc),
                     precision=lax.Precision.HIGHEST)
    tab = jnp.where(jnp.asarray(in_win)[None, None, :, None, :], tab, NEG_INF)
    return tab.reshape(len(rep_rows), n_heads * GRID_W, kh * GRID_W)


def _na_kernel(q_ref, k_ref, v_ref, bias_ref, o_ref, *, rows, n_heads):
    kh = NA_WIN_H
    lane = lax.broadcasted_iota(jnp.int32, (GRID_W, n_heads * NA_HEAD_DIM), 1)
    masks = [(lane >= h * NA_HEAD_DIM) & (lane < (h + 1) * NA_HEAD_DIM) for h in range(n_heads)]
    scale = NA_HEAD_DIM ** -0.5

    def body(r, carry):
        q0 = pl.multiple_of(r * GRID_W, GRID_W)
        q = q_ref[pl.ds(q0, GRID_W), :].astype(F32) * scale
        qs = jnp.concatenate([jnp.where(m, q, 0.0) for m in masks], axis=0).astype(BF16)
        rs = jnp.clip(r - kh // 2, 0, rows - kh)
        k0 = pl.multiple_of(rs * GRID_W, GRID_W)
        ks = k_ref[pl.ds(k0, kh * GRID_W), :]
        vs = v_ref[pl.ds(k0, kh * GRID_W), :]
        s = lax.dot_general(qs, ks, (((1,), (1,)), ((), ())), preferred_element_type=F32)
        var = jnp.where(r < kh // 2, r, jnp.where(r > rows - kh // 2, r - (rows - 8), kh // 2))
        s = s + bias_ref[var]
        m = jnp.max(s, axis=-1, keepdims=True)
        p = jnp.exp(s - m)
        den = jnp.sum(p, axis=-1, keepdims=True)
        o = jnp.dot(p.astype(BF16), vs, preferred_element_type=F32) / den
        out = jnp.where(masks[0], o[0:GRID_W], 0.0)
        for h in range(1, n_heads):
            out = out + jnp.where(masks[h], o[h * GRID_W:(h + 1) * GRID_W], 0.0)
        o_ref[pl.ds(q0, GRID_W), :] = out
        return carry

    lax.fori_loop(0, rows, body, 0, unroll=2)


def _na_attention(qkv, bias, bsz, seq):
    rows = seq // GRID_W
    n_heads = GROUP_W // NA_HEAD_DIM
    t = qkv.shape[0]
    blk = lambda j: pl.BlockSpec((seq, GROUP_W), lambda b, j=j: (b, j))
    return pl.pallas_call(
        functools.partial(_na_kernel, rows=rows, n_heads=n_heads),
        grid=(bsz,),
        in_specs=[blk(0), blk(1), blk(2),
                  pl.BlockSpec(bias.shape, lambda b: (0, 0, 0))],
        out_specs=pl.BlockSpec((seq, GROUP_W), lambda b: (b, 0)),
        out_shape=jax.ShapeDtypeStruct((t, GROUP_W), F32),
        compiler_params=_cparams(("arbitrary",)),
        name="na_attention",
    )(qkv, qkv, qkv, bias)


CONV_PAD = 16
CONV_CHUNK = 128


def _conv_kernel(b_ref, c_ref, x_ref, a_ref, g_ref, scw_ref, cfw_ref, cfb_ref, lng_ref, lnb_ref,
                 ysc_ref, ycf_ref, psc, pcf):
    seq = b_ref.shape[0]
    n_chunks = seq // CONV_CHUNK
    zeros = jnp.zeros((CONV_PAD, GROUP_W), F32)
    psc[0:CONV_PAD, :] = zeros
    pcf[0:CONV_PAD, :] = zeros
    psc[CONV_PAD + seq:2 * CONV_PAD + seq, :] = zeros
    pcf[CONV_PAD + seq:2 * CONV_PAD + seq, :] = zeros

    def fill(i, carry):
        r0 = pl.multiple_of(i * CONV_CHUNK, CONV_CHUNK)
        rows = pl.ds(r0, CONV_CHUNK)
        dst = pl.ds(r0 + CONV_PAD, CONV_CHUNK)
        psc[dst, :] = c_ref[rows, :] * x_ref[rows, :]
        pcf[dst, :] = a_ref[rows, :] * jax.nn.sigmoid(g_ref[rows, :])
        return carry

    lax.fori_loop(0, n_chunks, fill, 0)

    def taps(win, w_ref):
        n_taps = w_ref.shape[0]
        offs = [CONV_PAD + k - n_taps // 2 for k in range(n_taps)]
        n_win = win.shape[0]
        acc = None
        for s in range(SUBLANES):
            ks = [k for k in range(n_taps) if offs[k] % SUBLANES == s]
            if not ks:
                continue
            shifted = pltpu.roll(win, n_win - s, axis=0) if s else win
            for k in ks:
                q = offs[k] - s
                term = shifted[q:q + CONV_CHUNK, :] * w_ref[k:k + 1, :]
                acc = term if acc is None else acc + term
        return acc

    def conv(i, carry):
        r0 = pl.multiple_of(i * CONV_CHUNK, CONV_CHUNK)
        rows = pl.ds(r0, CONV_CHUNK)
        window = pl.ds(r0, CONV_CHUNK + 2 * CONV_PAD)
        ysc_ref[rows, :] = b_ref[rows, :] * taps(psc[window, :], scw_ref)
        cf = taps(pcf[window, :], cfw_ref) + cfb_ref[...]
        mu = jnp.mean(cf, axis=-1, keepdims=True)
        xc = cf - mu
        var = jnp.mean(xc * xc, axis=-1, keepdims=True)
        ln = xc * lax.rsqrt(var + EPS) * lng_ref[...] + lnb_ref[...]
        ycf_ref[rows, :] = ln * jax.nn.sigmoid(ln)
        return carry

    lax.fori_loop(0, n_chunks, conv, 0)


def _conv_mixers(rest, sc_w, cf_w, cf_b, ln_g, ln_b, bsz, seq):
    t = rest.shape[0]
    blk = lambda j: pl.BlockSpec((seq, GROUP_W), lambda b, j=j: (b, j))
    full = lambda a: pl.BlockSpec(a.shape, lambda b: (0,) * a.ndim)
    cf_b, ln_g, ln_b = (a.reshape(1, GROUP_W) for a in (cf_b, ln_g, ln_b))
    out_spec = pl.BlockSpec((seq, GROUP_W), lambda b: (b, 0))
    return pl.pallas_call(
        _conv_kernel,
        grid=(bsz,),
        in_specs=[blk(0), blk(1), blk(2), blk(3), blk(4),
                  full(sc_w), full(cf_w), full(cf_b), full(ln_g), full(ln_b)],
        out_specs=[out_spec, out_spec],
        out_shape=[jax.ShapeDtypeStruct((t, GROUP_W), F32)] * 2,
        scratch_shapes=[pltpu.VMEM((seq + 2 * CONV_PAD, GROUP_W), F32)] * 2,
        compiler_params=_cparams(("arbitrary",)),
        name="conv_mixers",
    )(rest, rest, rest, rest, rest, sc_w, cf_w, cf_b, ln_g, ln_b)


S5_TL = 128
S5_LANE_SPLIT = 2


def _s5_discretise(a_re, a_im, log_dt, b_re, b_im, c_re, c_im, bsz):
    f32 = F32
    a_re, a_im, log_dt = a_re.astype(f32), a_im.astype(f32), log_dt.astype(f32)
    n_dir, n_grp, n_state = a_re.shape
    n_ch = b_re.shape[-1]
    dt = jnp.exp(log_dt)[..., None]
    mag = jnp.exp(a_re * dt)
    abr = mag * jnp.cos(a_im * dt)
    abi = mag * jnp.sin(a_im * dt)
    den = a_re * a_re + a_im * a_im
    fr = ((abr - 1.0) * a_re + abi * a_im) / den
    fi = (abi * a_re - (abr - 1.0) * a_im) / den
    bbr = fr[..., None] * b_re - fi[..., None] * b_im
    bbi = fr[..., None] * b_im + fi[..., None] * b_re
    eye = jnp.eye(n_grp, dtype=f32)
    blk_b = lambda m: jnp.einsum("dgps,gh->dgshp", m, eye).reshape(n_dir, n_grp * n_ch, n_grp * n_state)
    blk_c = lambda m: jnp.einsum("dgsp,gh->dgphs", m, eye).reshape(n_dir, n_grp * n_state, n_grp * n_ch)
    bmat = jnp.concatenate([blk_b(bbr), blk_b(bbi)], axis=-1)
    cmat = jnp.concatenate([blk_c(c_re.astype(f32)), blk_c(-c_im.astype(f32))], axis=1)
    amat = jnp.stack([abr.reshape(n_dir, -1), abi.reshape(n_dir, -1)], axis=1)
    amat = jnp.broadcast_to(amat[:, :, None, :], (n_dir, 2, bsz, n_grp * n_state))
    return bmat.astype(BF16), cmat.astype(BF16), amat


def _s5_kernel(*refs, bsz, backward):
    if backward:
        (u_ref, b_ref, c_ref, a_ref, yf_ref, d_ref, w_ref, bg_ref, o_ref, xs, st, tb) = refs
    else:
        (u_ref, b_ref, c_ref, a_ref, o_ref, xs, st, tb) = refs
    n_state = a_ref.shape[-1]

    @pl.when(pl.program_id(0) == 0)
    def _():
        st[...] = jnp.zeros_like(st)

    n_half = GROUP_W // LANES
    for b in range(bsz):
        for h in range(n_half):
            tb[h, pl.ds(b, S5_TL, stride=bsz), :] = u_ref[b, :, h * LANES:(h + 1) * LANES]
    u_tb = jnp.concatenate([tb[h] for h in range(n_half)], axis=1)
    xs[...] = jnp.dot(u_tb.astype(BF16), b_ref[...], preferred_element_type=F32)

    cw = n_state // S5_LANE_SPLIT
    for c in range(S5_LANE_SPLIT):
        re = slice(c * cw, (c + 1) * cw)
        im = slice(n_state + c * cw, n_state + (c + 1) * cw)
        a_r = a_ref[0, :, re]
        a_i = a_ref[1, :, re]

        def step(t, carry, re=re, im=im, a_r=a_r, a_i=a_i):
            s_r, s_i = carry
            tt = S5_TL - 1 - t if backward else t
            rows = pl.ds(pl.multiple_of(tt * bsz, bsz), bsz)
            n_r = a_r * s_r - a_i * s_i + xs[rows, re]
            n_i = a_r * s_i + a_i * s_r + xs[rows, im]
            xs[rows, re] = n_r
            xs[rows, im] = n_i
            return n_r, n_i

        s_r, s_i = lax.fori_loop(0, S5_TL, step, (st[:, re], st[:, im]), unroll=4)
        st[:, re] = s_r
        st[:, im] = s_i

    y_tb = jnp.dot(xs[...].astype(BF16), c_ref[...], preferred_element_type=F32)
    if not backward:
        o_ref[...] = y_tb
        return
    y_sum = y_tb + yf_ref[...]
    for h in range(n_half):
        tb[h] = y_sum[:, h * LANES:(h + 1) * LANES]
    y_bt = jnp.concatenate(
        [jnp.concatenate([tb[h, pl.ds(b, S5_TL, stride=bsz), :] for h in range(n_half)], axis=1)
         for b in range(bsz)], axis=0)
    y = jax.nn.gelu(d_ref[...] * u_ref[...].reshape(S5_TL * bsz, GROUP_W) + y_bt)
    gate = jnp.dot(y.astype(BF16), w_ref[...], preferred_element_type=F32) + bg_ref[...]
    o_ref[...] = (y * jax.nn.sigmoid(gate)).reshape(bsz, S5_TL, GROUP_W)


def _s5_mixer(rest, ssm_col, bmat, cmat, amat, d_skip, w_glu, b_glu, bsz, seq):
    n_chunks = seq // S5_TL
    n_rows = S5_TL * bsz
    n_st2 = bmat.shape[-1]
    rest3 = rest.reshape(bsz, seq, rest.shape[1])
    full = lambda a: pl.BlockSpec(a.shape, lambda i: (0,) * a.ndim)
    scratch = [pltpu.VMEM((n_rows, n_st2), F32),
               pltpu.VMEM((bsz, n_st2), F32),
               pltpu.VMEM((GROUP_W // LANES, n_rows, LANES), F32)]

    def direction(d, chunk_of):
        return [
            pl.BlockSpec((bsz, S5_TL, GROUP_W), lambda i: (0, chunk_of(i), ssm_col)),
            pl.BlockSpec((None, GROUP_W, n_st2), lambda i: (d, 0, 0)),
            pl.BlockSpec((None, n_st2, GROUP_W), lambda i: (d, 0, 0)),
            pl.BlockSpec((None, 2, bsz, n_st2 // 2), lambda i: (d, 0, 0, 0)),
        ]

    y_fwd = pl.pallas_call(
        functools.partial(_s5_kernel, bsz=bsz, backward=False),
        grid=(n_chunks,),
        in_specs=direction(0, lambda i: i),
        out_specs=pl.BlockSpec((n_rows, GROUP_W), lambda i: (i, 0)),
        out_shape=jax.ShapeDtypeStruct((seq * bsz, GROUP_W), F32),
        scratch_shapes=scratch,
        compiler_params=_cparams(("arbitrary",)),
        name="s5_forward",
    )(rest3, bmat, cmat, amat)

    rev = lambda i: n_chunks - 1 - i
    d_skip, b_glu = d_skip.reshape(1, GROUP_W), b_glu.reshape(1, GROUP_W)
    out = pl.pallas_call(
        functools.partial(_s5_kernel, bsz=bsz, backward=True),
        grid=(n_chunks,),
        in_specs=direction(1, rev) + [
            pl.BlockSpec((n_rows, GROUP_W), lambda i: (rev(i), 0)),
            full(d_skip), full(w_glu), full(b_glu),
        ],
        out_specs=pl.BlockSpec((bsz, S5_TL, GROUP_W), lambda i: (0, rev(i), 0)),
        out_shape=jax.ShapeDtypeStruct((bsz, seq, GROUP_W), F32),
        scratch_shapes=scratch,
        compiler_params=_cparams(("arbitrary",)),
        name="s5_backward",
    )(rest3, bmat, cmat, amat, y_fwd, d_skip, w_glu, b_glu)
    return out.reshape(bsz * seq, GROUP_W)


FFN_CHUNK = 256
ROUTE_COLS = 8
ROUTER_LANES = 128


def _route_records(h, wr_ref):
    logits = jnp.dot(h, wr_ref[...], preferred_element_type=F32, precision=lax.Precision.HIGHEST)
    lane = lax.broadcasted_iota(jnp.int32, logits.shape, 1)
    minus_inf = -jnp.inf
    l1 = jnp.where(lane < N_EXPERTS, logits, minus_inf)
    m1 = jnp.max(l1, axis=-1, keepdims=True)
    i1 = jnp.min(jnp.where(l1 == m1, lane, ROUTER_LANES), axis=-1, keepdims=True)
    l2 = jnp.where(lane == i1, minus_inf, l1)
    m2 = jnp.max(l2, axis=-1, keepdims=True)
    i2 = jnp.min(jnp.where(l2 == m2, lane, ROUTER_LANES), axis=-1, keepdims=True)
    e2 = jnp.exp(m2 - m1)
    den = 1.0 + e2
    rec = jnp.where(lane == 0, i1.astype(F32),
                    jnp.where(lane == 1, i2.astype(F32),
                              jnp.where(lane == 2, 1.0 / den,
                                        jnp.where(lane == 3, e2 / den, 0.0))))
    return rec[:, :ROUTE_COLS]


def _mixer_out_kernel(*refs, tail):
    y_refs, (gn_ref, w_ref, x_ref, g2_ref) = refs[:4], refs[4:8]
    acc = x_ref[...]
    for j, y_ref in enumerate(y_refs):
        yn = _rms(y_ref[...], gn_ref[j:j + 1, :]).astype(BF16)
        acc = acc + jnp.dot(yn, w_ref[j * GROUP_W:(j + 1) * GROUP_W, :], preferred_element_type=F32)
    h = _rms(acc, g2_ref[...])
    if tail == "route":
        wr_ref, o_ref, route_ref = refs[8:]
        _store_token_tiles(o_ref, acc)
        route_ref[...] = _route_records(h, wr_ref)
        return
    wg_hbm, wu_hbm, wd_hbm, o_ref, wg_ref, wu_ref, wd_ref, sem = refs[8:]

    @pl.when(pl.program_id(0) == 0)
    def _():
        copies = [pltpu.make_async_copy(src, dst, sem.at[k])
                  for k, (src, dst) in enumerate(((wg_hbm, wg_ref), (wu_hbm, wu_ref), (wd_hbm, wd_ref)))]
        for c in copies:
            c.start()
        for c in copies:
            c.wait()

    h = h.astype(BF16)
    acts = []
    for c in range(wg_ref.shape[0]):
        gate = jnp.dot(h, wg_ref[c], preferred_element_type=F32)
        up = jnp.dot(h, wu_ref[c], preferred_element_type=F32)
        acts.append((gate * jax.nn.sigmoid(gate) * up).astype(BF16))
    o_ref[...] = acc + jnp.dot(jnp.concatenate(acts, axis=1), wd_ref[...], preferred_element_type=F32)


def _mixer_out(ys, gn, w_out, x2, g2, tail, tail_weights, tm=512):
    t, d = x2.shape
    row = lambda n: pl.BlockSpec((tm, n), lambda i: (i, 0))
    whole = lambda a: pl.BlockSpec(a.shape, lambda i: (0, 0))
    if tail == "route":
        wr = jnp.zeros((d, ROUTER_LANES), F32).at[:, :N_EXPERTS].set(tail_weights[0].astype(F32))
        tail_weights = (wr,)
        tail_specs = [whole(w) for w in tail_weights]
        out_specs = [pl.BlockSpec((tm * SUBLANES, LANES), lambda i: (i, 0)), row(ROUTE_COLS)]
        out_shape = [jax.ShapeDtypeStruct((t * SUBLANES, LANES), F32), jax.ShapeDtypeStruct((t, ROUTE_COLS), F32)]
        scratch = []
    else:
        w_gate, w_up, w_down = tail_weights
        n_chunks = w_gate.shape[1] // FFN_CHUNK
        by_cols = lambda w: jnp.transpose(w.reshape(d, n_chunks, FFN_CHUNK), (1, 0, 2))
        tail_weights = (by_cols(w_gate), by_cols(w_up), w_down)
        tail_specs = [pl.BlockSpec(memory_space=pl.ANY)] * 3
        out_specs, out_shape = row(d), jax.ShapeDtypeStruct((t, d), F32)
        scratch = [pltpu.VMEM(w.shape, w.dtype) for w in tail_weights] + [pltpu.SemaphoreType.DMA((3,))]
    return pl.pallas_call(
        functools.partial(_mixer_out_kernel, tail=tail),
        grid=(t // tm,),
        in_specs=[row(GROUP_W)] * 4 + [whole(gn), whole(w_out), row(d), whole(g2)] + tail_specs,
        out_specs=out_specs,
        out_shape=out_shape,
        scratch_shapes=scratch,
        compiler_params=_cparams(("arbitrary",)),
        name="mixer_out_" + tail,
    )(*ys, gn, w_out, x2, g2, *tail_weights)


def _moe_plan(route, tm, n_tiles):
    ids = route[:, :2]
    experts = jnp.where((ids >= 0) & (ids <= N_EXPERTS - 1), ids, 0).astype(jnp.int32).reshape(-1)
    onehot = (experts[:, None] == jnp.arange(N_EXPERTS, dtype=jnp.int32)[None, :]).astype(jnp.int32)
    csum = jnp.cumsum(onehot, axis=0)
    rank = jnp.sum((csum - onehot) * onehot, axis=1)
    count = csum[-1]
    padded = ((count + tm - 1) // tm) * tm
    end = jnp.cumsum(padded)
    start = end - padded
    slot = jnp.sum(onehot * start[None, :], axis=1) + rank
    n_active = end[-1] // tm
    tile_start = jnp.arange(n_tiles, dtype=jnp.int32) * tm
    tile_expert = jnp.sum((tile_start[:, None] >= end[None, :]).astype(jnp.int32), axis=1)
    tile_expert = jnp.minimum(tile_expert, N_EXPERTS - 1)
    last = tile_expert[jnp.maximum(n_active - 1, 0)]
    tile_expert = jnp.where(jnp.arange(n_tiles) < n_active, tile_expert, last)
    return slot.astype(jnp.int32), tile_expert.astype(jnp.int32), n_active.astype(jnp.int32).reshape(1)


def _token_copy(src, src_tok, dst, dst_tok, sem):
    rows = lambda tok: pl.ds(pl.multiple_of(tok * SUBLANES, SUBLANES), SUBLANES)
    return pltpu.make_async_copy(src.at[rows(src_tok)], dst.at[rows(dst_tok)], sem)


def _invert_kernel(slot_ref, word_ref, pad_hbm, plan_ref, sem):
    @pl.when(pl.program_id(0) == 0)
    def _():
        init = pltpu.make_async_copy(pad_hbm, plan_ref, sem)
        init.start()
        init.wait()

    def place(a, carry):
        plan_ref[slot_ref[0, 0, a]] = word_ref[0, 0, a]
        return carry

    lax.fori_loop(0, slot_ref.shape[-1], place, 0, unroll=8)


def _plan_bits(n_tok, tm):
    src_bits = (n_tok - 1).bit_length()
    assert (2 * n_tok + 2 * tm) << src_bits < 2 ** 31 and tm & (tm - 1) == 0
    return src_bits


def _invert(slot, n_slots, tm, n_tok, chunk=4096):
    src_bits = _plan_bits(n_tok, tm)
    a = jnp.arange(slot.shape[0], dtype=jnp.int32)
    tok = a >> 1
    word = tok + (((a & 1) * n_tok + tok) << src_bits)
    s = jnp.arange(n_slots, dtype=jnp.int32)
    pad_plan = (2 * n_tok + (s & (2 * tm - 1))) << src_bits
    n_chunks = slot.shape[0] // chunk
    chunked = pl.BlockSpec((1, 1, chunk), lambda c: (c, 0, 0), memory_space=pltpu.SMEM)
    return pl.pallas_call(
        _invert_kernel,
        grid=(n_chunks,),
        in_specs=[chunked, chunked, pl.BlockSpec(memory_space=pl.ANY)],
        out_specs=pl.BlockSpec((n_slots,), lambda c: (0,), memory_space=pltpu.SMEM),
        out_shape=jax.ShapeDtypeStruct((n_slots,), jnp.int32),
        scratch_shapes=[pltpu.SemaphoreType.DMA(())],
        compiler_params=pltpu.CompilerParams(dimension_semantics=("arbitrary",)),
        name="moe_invert",
    )(slot.reshape(n_chunks, 1, chunk), word.reshape(n_chunks, 1, chunk), pad_plan)


def _tokens_done(ref, n_tok, sem):
    rows = pl.ds(0, n_tok * SUBLANES)
    pltpu.make_async_copy(ref.at[rows], ref.at[rows], sem).wait()


def _moe_ffn_kernel(te_ref, nact_ref, plan_ref, x_hbm, g_ref, wg_ref, wu_ref, wd_ref, yk_hbm,
                    xbuf, xb16, ybuf, acc_ref, sem_in, sem_out, *, tm, n_tok, n_f, pad_tile, src_bits):
    del te_ref
    i = pl.program_id(0)
    f = pl.program_id(1)
    par = i % 2
    nact = nact_ref[0]

    def fetch(tile, p, j):
        src = plan_ref[tile * tm + j] & ((1 << src_bits) - 1)
        _token_copy(x_hbm, src, xbuf.at[p], j, sem_in.at[p]).start(priority=1)

    def send(tile, p, j):
        dst = plan_ref[tile * tm + j] >> src_bits
        _token_copy(ybuf.at[p], j, yk_hbm, dst, sem_out.at[p]).start()

    @pl.when((i == 0) & (f == 0))
    def _():
        def first(j, carry):
            fetch(0, 0, j)
            return carry
        lax.fori_loop(0, tm, first, 0)
        ybuf[...] = jnp.zeros_like(ybuf)
        for p in range(2):
            spare_rows = pl.ds((2 * n_tok + p * tm) * SUBLANES, tm * SUBLANES)
            spare = pltpu.make_async_copy(ybuf.at[p], yk_hbm.at[spare_rows], sem_out.at[p])
            spare.start()
            spare.wait()
        _tokens_done(xbuf.at[0], tm, sem_in.at[0])

    @pl.when((i >= 1) & (i <= nact) & (f == 0))
    def _():
        _tokens_done(xbuf.at[par], tm, sem_in.at[par])

    @pl.when(i < nact)
    def _():
        @pl.when(f == 0)
        def _():
            xb16[...] = _rms(_load_token_tiles(xbuf.at[par], tm), g_ref[...]).astype(BF16)
            acc_ref[...] = jnp.zeros_like(acc_ref)

            for j in range(tm):
                fetch(i + 1, 1 - par, j)

        @pl.when(f == n_f - 1)
        def _():
            prev = jnp.where(i > 0, i - 1, pad_tile)

            for j in range(tm):
                send(prev, 1 - par, j)

        x = xb16[...]
        gate = jnp.dot(x, wg_ref[...], preferred_element_type=F32)
        up = jnp.dot(x, wu_ref[...], preferred_element_type=F32)
        act = (gate * jax.nn.sigmoid(gate) * up).astype(BF16)
        acc_ref[...] += jnp.dot(act, wd_ref[...], preferred_element_type=F32)

        @pl.when(f == n_f - 1)
        def _():
            @pl.when(i >= 1)
            def _():
                _tokens_done(ybuf.at[par], tm, sem_out.at[par])

            _store_token_tiles(ybuf.at[par], acc_ref[...])

    @pl.when((i == nact) & (f == 0))
    def _():
        def last(j, carry):
            send(i - 1, 1 - par, j)
            return carry
        lax.fori_loop(0, tm, last, 0)
        _tokens_done(ybuf.at[0], tm, sem_out.at[0])
        _tokens_done(ybuf.at[1], tm, sem_out.at[1])


def _moe_ffn(xt, g, plan, tile_expert, n_active, wg, wu, wd, tm):
    n_tok = xt.shape[0] // SUBLANES
    d = SUBLANES * LANES
    n_tiles = tile_expert.shape[0]
    pad_tile = n_tiles - 1 if (n_tiles - 1) % 2 else n_tiles
    assert plan.shape[0] >= (pad_tile + 1) * tm
    dff = wg.shape[-1]
    n_f = 2
    tf = dff // n_f
    fidx = lambda i, f, na: jnp.where(i < na[0], f, n_f - 1)
    grid_spec = pltpu.PrefetchScalarGridSpec(
        num_scalar_prefetch=3,
        grid=(n_tiles, n_f),
        in_specs=[
            pl.BlockSpec(memory_space=pl.ANY),
            pl.BlockSpec((1, d), lambda i, f, te, na, plan: (0, 0)),
            pl.BlockSpec((None, d, tf), lambda i, f, te, na, plan: (te[i], 0, fidx(i, f, na))),
            pl.BlockSpec((None, d, tf), lambda i, f, te, na, plan: (te[i], 0, fidx(i, f, na))),
            pl.BlockSpec((None, tf, d), lambda i, f, te, na, plan: (te[i], fidx(i, f, na), 0)),
        ],
        out_specs=pl.BlockSpec(memory_space=pl.ANY),
        scratch_shapes=[
            pltpu.VMEM((2, tm * SUBLANES, LANES), F32),
            pltpu.VMEM((tm, d), BF16),
            pltpu.VMEM((2, tm * SUBLANES, LANES), F32),
            pltpu.VMEM((tm, d), F32),
            pltpu.SemaphoreType.DMA((2,)),
            pltpu.SemaphoreType.DMA((2,)),
        ],
    )
    return pl.pallas_call(
        functools.partial(_moe_ffn_kernel, tm=tm, n_tok=n_tok, n_f=n_f, pad_tile=pad_tile,
                          src_bits=_plan_bits(n_tok, tm)),
        grid_spec=grid_spec,
        out_shape=jax.ShapeDtypeStruct(((2 * n_tok + 2 * tm) * SUBLANES, LANES), F32),
        compiler_params=_cparams(("arbitrary", "arbitrary")),
        name="moe_ffn",
    )(tile_expert, n_active, plan, xt, g.reshape(1, d), wg, wu, wd)


def _combine_kernel(route_ref, x_ref, y0_ref, y1_ref, fg_ref, o_ref, *, final_norm):
    r = route_ref[...]
    tm = r.shape[0]
    y = (_load_token_tiles(x_ref, tm) + r[:, 2:3] * _load_token_tiles(y0_ref, tm)
         + r[:, 3:4] * _load_token_tiles(y1_ref, tm))
    o_ref[...] = _rms(y, fg_ref[...]) if final_norm else y


def _combine(yk, route, xt, final_g, final_norm, tm=512):
    t = xt.shape[0] // SUBLANES
    d = SUBLANES * LANES
    n_blk = t // tm
    tiles = lambda f: pl.BlockSpec((tm * SUBLANES, LANES), f)
    return pl.pallas_call(
        functools.partial(_combine_kernel, final_norm=final_norm),
        grid=(n_blk,),
        in_specs=[
            pl.BlockSpec((tm, ROUTE_COLS), lambda i: (i, 0)),
            tiles(lambda i: (i, 0)),
            tiles(lambda i: (i, 0)),
            tiles(lambda i: (n_blk + i, 0)),
            pl.BlockSpec((1, d), lambda i: (0, 0)),
        ],
        out_specs=pl.BlockSpec((tm, d), lambda i: (i, 0)),
        out_shape=jax.ShapeDtypeStruct((t, d), F32),
        compiler_params=_cparams(("arbitrary",)),
        name="moe_combine",
    )(route, xt, yk, yk, final_g.reshape(1, d))


def _final_norm_kernel(x_ref, g_ref, o_ref):
    o_ref[...] = _rms(x_ref[...], g_ref[...])


def _final_norm(x2, g, tm=512):
    t, d = x2.shape
    return pl.pallas_call(
        _final_norm_kernel,
        grid=(t // tm,),
        in_specs=[pl.BlockSpec((tm, d), lambda i: (i, 0)), pl.BlockSpec((1, d), lambda i: (0, 0))],
        out_specs=pl.BlockSpec((tm, d), lambda i: (i, 0)),
        out_shape=jax.ShapeDtypeStruct((t, d), F32),
        compiler_params=_cparams(("arbitrary",)),
        name="final_norm",
    )(x2, g.reshape(1, d))


MOE_TM = 512


def kernel(x, norm1_g, w_in, na_rpb, sc_conv_w, cf_conv_w, cf_conv_b, cf_ln_g, cf_ln_b, ssm_a_re, ssm_a_im,
           ssm_log_dt, ssm_b_re, ssm_b_im, ssm_c_re, ssm_c_im, ssm_d, ssm_w_glu, ssm_b_glu, grp_norm_g, w_out,
           norm2_g, ffn_w_gate, ffn_w_up, ffn_w_down, moe_w_router, moe_w_gate, moe_w_up, moe_w_down,
           final_norm_g):
    bsz, seq, d = x.shape
    depth = w_in.shape[0]
    t = bsz * seq
    rows = seq // GRID_W
    x2 = x.reshape(t, d).astype(F32)
    final_done = False
    for i in range(depth):
        qkv, rest = _in_proj(x2, norm1_g[i], w_in[i].astype(BF16))
        y_na = _na_attention(qkv, _na_bias_table(na_rpb[i], rows), bsz, seq)
        y_sc, y_cf = _conv_mixers(rest, sc_conv_w[i], cf_conv_w[i], cf_conv_b[i], cf_ln_g[i], cf_ln_b[i],
                                  bsz, seq)
        bmat, cmat, amat = _s5_discretise(ssm_a_re[i], ssm_a_im[i], ssm_log_dt[i], ssm_b_re[i], ssm_b_im[i],
                                          ssm_c_re[i], ssm_c_im[i], bsz)
        y_ssm = _s5_mixer(rest, SSM_COL, bmat, cmat, amat, ssm_d[i], ssm_w_glu[i].astype(BF16), ssm_b_glu[i],
                          bsz, seq)
        ys = (y_na, y_sc, y_cf, y_ssm)
        gn, g2 = grp_norm_g[i].reshape(4, GROUP_W), norm2_g[i].reshape(1, d)
        j = i // 2
        if i % 2 == 0:
            ffn_w = (ffn_w_gate[j].astype(BF16), ffn_w_up[j].astype(BF16), ffn_w_down[j].astype(BF16))
            x2 = _mixer_out(ys, gn, w_out[i].astype(BF16), x2, g2, "ffn", ffn_w)
        else:
            x2, route = _mixer_out(ys, gn, w_out[i].astype(BF16), x2, g2, "route", (moe_w_router[j],))
            n_tiles = (2 * t) // MOE_TM + N_EXPERTS + 1
            slot, tile_expert, n_active = _moe_plan(route, MOE_TM, n_tiles)
            plan = _invert(slot, (n_tiles + 1) * MOE_TM, MOE_TM, t)
            yk = _moe_ffn(x2, norm2_g[i], plan, tile_expert, n_active, moe_w_gate[j].astype(BF16),
                          moe_w_up[j].astype(BF16), moe_w_down[j].astype(BF16), MOE_TM)
            final_done = i == depth - 1
            x2 = _combine(yk, route, x2, final_norm_g, final_done)
    if not final_done:
        x2 = _final_norm(x2, final_norm_g)
    return x2.reshape(bsz, seq, d)
```

```python
import functools
import math

import jax
import jax.numpy as jnp
import numpy as np
from jax import lax
from jax.experimental import pallas as pl
from jax.experimental.pallas import tpu as pltpu

F32 = jnp.float32
BF16 = jnp.bfloat16
EPS = 1e-6
NEG_INF = -1e30

GRID_W = 64
NA_HEAD_DIM = 64
NA_WIN_H = 8
NA_WIN_W = 16
GROUP_W = 256
SSM_GROUP_CH = 16
SSM_STATE = 64
N_EXPERTS = 8
VMEM_LIMIT = 56 * 1024 * 1024


def _cparams(sem, vmem=VMEM_LIMIT):
    return pltpu.CompilerParams(dimension_semantics=sem, vmem_limit_bytes=vmem)


def _rms(x, g):
    return x * lax.rsqrt(jnp.mean(x * x, axis=-1, keepdims=True) + EPS) * g


SUBLANES = 8
LANES = 128


def _load_token_tiles(ref, n):
    return jnp.concatenate([ref[pl.ds(s, n, stride=SUBLANES), :] for s in range(SUBLANES)], axis=1)


def _store_token_tiles(ref, val):
    for s in range(SUBLANES):
        ref[pl.ds(s, val.shape[0], stride=SUBLANES), :] = val[:, s * LANES:(s + 1) * LANES]


N_QKV_COLS = 3 * GROUP_W
SSM_COL = 5


def _in_proj_kernel(x_ref, g_ref, w_ref, qkv_ref, rest_ref):
    h = _rms(x_ref[...], g_ref[...]).astype(BF16)
    n_qkv = qkv_ref.shape[1] // GROUP_W
    n_rest = rest_ref.shape[1] // GROUP_W
    for j in range(n_qkv + n_rest):
        z = jnp.dot(h, w_ref[:, j * GROUP_W:(j + 1) * GROUP_W], preferred_element_type=F32)
        if j < n_qkv:
            qkv_ref[:, j * GROUP_W:(j + 1) * GROUP_W] = z.astype(BF16)
        else:
            k = j - n_qkv
            rest_ref[:, k * GROUP_W:(k + 1) * GROUP_W] = z


def _in_proj(x2, g, w, tm=512):
    t, d = x2.shape
    n_cols = w.shape[1]
    n_rest = n_cols - N_QKV_COLS
    return pl.pallas_call(
        _in_proj_kernel,
        grid=(t // tm,),
        in_specs=[
            pl.BlockSpec((tm, d), lambda i: (i, 0)),
            pl.BlockSpec((1, d), lambda i: (0, 0)),
            pl.BlockSpec((d, n_cols), lambda i: (0, 0)),
        ],
        out_specs=[
            pl.BlockSpec((tm, N_QKV_COLS), lambda i: (i, 0)),
            pl.BlockSpec((tm, n_rest), lambda i: (i, 0)),
        ],
        out_shape=[
            jax.ShapeDtypeStruct((t, N_QKV_COLS), BF16),
            jax.ShapeDtypeStruct((t, n_rest), F32),
        ],
        compiler_params=_cparams(("arbitrary",)),
        name="in_proj",
    )(x2, g.reshape(1, d), w)


def _na_bias_table(rpb, rows):
    n_heads = rpb.shape[0]
    kh = NA_WIN_H
    c = np.arange(GRID_W)
    qcs = np.clip(c - NA_WIN_W // 2, 0, GRID_W - NA_WIN_W)
    kc = np.arange(GRID_W)
    in_win = (kc[None, :] >= qcs[:, None]) & (kc[None, :] < qcs[:, None] + NA_WIN_W)
    dc_idx = np.clip(kc[None, :] - c[:, None] + NA_WIN_W - 1, 0, 2 * NA_WIN_W - 2)
    onehot_dc = (dc_idx[:, :, None] == np.arange(2 * NA_WIN_W - 1)).astype(np.float32)
    rep_rows = np.array([0, 1, 2, 3, 4, rows - 3, rows - 2, rows - 1])
    row_start = np.clip(rep_rows - kh // 2, 0, rows - kh)
    dr_idx = (row_start - rep_rows + NA_WIN_H - 1)[:, None] + np.arange(kh)[None, :]
    onehot_dr = (dr_idx[:, :, None] == np.arange(2 * NA_WIN_H - 1)).astype(np.float32)
    tab = jnp.einsum("via,hab,ckb->vhcik", jnp.asarray(onehot_dr), rpb.astype(F32), jnp.asarray(onehot_dc),
                     precision=lax.Precision.HIGHEST)
    tab = jnp.where(jnp.asarray(in_win)[None, None, :, None, :], tab, NEG_INF)
    return tab.reshape(len(rep_rows), n_heads * GRID_W, kh * GRID_W)


def _na_kernel(q_ref, k_ref, v_ref, bias_ref, o_ref, *, rows, n_heads):
    kh = NA_WIN_H
    lane = lax.broadcasted_iota(jnp.int32, (GRID_W, n_heads * NA_HEAD_DIM), 1)
    masks = [(lane >= h * NA_HEAD_DIM) & (lane < (h + 1) * NA_HEAD_DIM) for h in range(n_heads)]
    scale = NA_HEAD_DIM ** -0.5

    def body(r, carry):
        q0 = pl.multiple_of(r * GRID_W, GRID_W)
        q = q_ref[pl.ds(q0, GRID_W), :].astype(F32) * scale
        qs = jnp.concatenate([jnp.where(m, q, 0.0) for m in masks], axis=0).astype(BF16)
        rs = jnp.clip(r - kh // 2, 0, rows - kh)
        k0 = pl.multiple_of(rs * GRID_W, GRID_W)
        ks = k_ref[pl.ds(k0, kh * GRID_W), :]
        vs = v_ref[pl.ds(k0, kh * GRID_W), :]
        s = lax.dot_general(qs, ks, (((1,), (1,)), ((), ())), preferred_element_type=F32)
        var = jnp.where(r < kh // 2, r, jnp.where(r > rows - kh // 2, r - (rows - 8), kh // 2))
        s = s + bias_ref[var]
        m = jnp.max(s, axis=-1, keepdims=True)
        p = jnp.exp(s - m)
        den = jnp.sum(p, axis=-1, keepdims=True)
        o = jnp.dot(p.astype(BF16), vs, preferred_element_type=F32) / den
        out = jnp.where(masks[0], o[0:GRID_W], 0.0)
        for h in range(1, n_heads):
            out = out + jnp.where(masks[h], o[h * GRID_W:(h + 1) * GRID_W], 0.0)
        o_ref[pl.ds(q0, GRID_W), :] = out
        return carry

    lax.fori_loop(0, rows, body, 0, unroll=2)


def _na_attention(qkv, bias, bsz, seq):
    rows = seq // GRID_W
    n_heads = GROUP_W // NA_HEAD_DIM
    t = qkv.shape[0]
    blk = lambda j: pl.BlockSpec((seq, GROUP_W), lambda b, j=j: (b, j))
    return pl.pallas_call(
        functools.partial(_na_kernel, rows=rows, n_heads=n_heads),
        grid=(bsz,),
        in_specs=[blk(0), blk(1), blk(2),
                  pl.BlockSpec(bias.shape, lambda b: (0, 0, 0))],
        out_specs=pl.BlockSpec((seq, GROUP_W), lambda b: (b, 0)),
        out_shape=jax.ShapeDtypeStruct((t, GROUP_W), F32),
        compiler_params=_cparams(("arbitrary",)),
        name="na_attention",
    )(qkv, qkv, qkv, bias)


CONV_PAD = 16
CONV_CHUNK = 128


def _conv_kernel(b_ref, c_ref, x_ref, a_ref, g_ref, scw_ref, cfw_ref, cfb_ref, lng_ref, lnb_ref,
                 ysc_ref, ycf_ref, psc, pcf):
    seq = b_ref.shape[0]
    n_chunks = seq // CONV_CHUNK
    zeros = jnp.zeros((CONV_PAD, GROUP_W), F32)
    psc[0:CONV_PAD, :] = zeros
    pcf[0:CONV_PAD, :] = zeros
    psc[CONV_PAD + seq:2 * CONV_PAD + seq, :] = zeros
    pcf[CONV_PAD + seq:2 * CONV_PAD + seq, :] = zeros

    def fill(i, carry):
        r0 = pl.multiple_of(i * CONV_CHUNK, CONV_CHUNK)
        rows = pl.ds(r0, CONV_CHUNK)
        dst = pl.ds(r0 + CONV_PAD, CONV_CHUNK)
        psc[dst, :] = c_ref[rows, :] * x_ref[rows, :]
        pcf[dst, :] = a_ref[rows, :] * jax.nn.sigmoid(g_ref[rows, :])
        return carry

    lax.fori_loop(0, n_chunks, fill, 0)

    def taps(win, w_ref):
        n_taps = w_ref.shape[0]
        offs = [CONV_PAD + k - n_taps // 2 for k in range(n_taps)]
        n_win = win.shape[0]
        acc = None
        for s in range(SUBLANES):
            ks = [k for k in range(n_taps) if offs[k] % SUBLANES == s]
            if not ks:
                continue
            shifted = pltpu.roll(win, n_win - s, axis=0) if s else win
            for k in ks:
                q = offs[k] - s
                term = shifted[q:q + CONV_CHUNK, :] * w_ref[k:k + 1, :]
                acc = term if acc is None else acc + term
        return acc

    def conv(i, carry):
        r0 = pl.multiple_of(i * CONV_CHUNK, CONV_CHUNK)
        rows = pl.ds(r0, CONV_CHUNK)
        window = pl.ds(r0, CONV_CHUNK + 2 * CONV_PAD)
        ysc_ref[rows, :] = b_ref[rows, :] * taps(psc[window, :], scw_ref)
        cf = taps(pcf[window, :], cfw_ref) + cfb_ref[...]
        mu = jnp.mean(cf, axis=-1, keepdims=True)
        xc = cf - mu
        var = jnp.mean(xc * xc, axis=-1, keepdims=True)
        ln = xc * lax.rsqrt(var + EPS) * lng_ref[...] + lnb_ref[...]
        ycf_ref[rows, :] = ln * jax.nn.sigmoid(ln)
        return carry

    lax.fori_loop(0, n_chunks, conv, 0)


def _conv_mixers(rest, sc_w, cf_w, cf_b, ln_g, ln_b, bsz, seq):
    t = rest.shape[0]
    blk = lambda j: pl.BlockSpec((seq, GROUP_W), lambda b, j=j: (b, j))
    full = lambda a: pl.BlockSpec(a.shape, lambda b: (0,) * a.ndim)
    cf_b, ln_g, ln_b = (a.reshape(1, GROUP_W) for a in (cf_b, ln_g, ln_b))
    out_spec = pl.BlockSpec((seq, GROUP_W), lambda b: (b, 0))
    return pl.pallas_call(
        _conv_kernel,
        grid=(bsz,),
        in_specs=[blk(0), blk(1), blk(2), blk(3), blk(4),
                  full(sc_w), full(cf_w), full(cf_b), full(ln_g), full(ln_b)],
        out_specs=[out_spec, out_spec],
        out_shape=[jax.ShapeDtypeStruct((t, GROUP_W), F32)] * 2,
        scratch_shapes=[pltpu.VMEM((seq + 2 * CONV_PAD, GROUP_W), F32)] * 2,
        compiler_params=_cparams(("arbitrary",)),
        name="conv_mixers",
    )(rest, rest, rest, rest, rest, sc_w, cf_w, cf_b, ln_g, ln_b)


S5_TL = 128
S5_LANE_SPLIT = 1


def _s5_discretise(a_re, a_im, log_dt, b_re, b_im, c_re, c_im, bsz):
    f32 = F32
    a_re, a_im, log_dt = a_re.astype(f32), a_im.astype(f32), log_dt.astype(f32)
    n_dir, n_grp, n_state = a_re.shape
    n_ch = b_re.shape[-1]
    dt = jnp.exp(log_dt)[..., None]
    mag = jnp.exp(a_re * dt)
    abr = mag * jnp.cos(a_im * dt)
    abi = mag * jnp.sin(a_im * dt)
    den = a_re * a_re + a_im * a_im
    fr = ((abr - 1.0) * a_re + abi * a_im) / den
    fi = (abi * a_re - (abr - 1.0) * a_im) / den
    bbr = fr[..., None] * b_re - fi[..., None] * b_im
    bbi = fr[..., None] * b_im + fi[..., None] * b_re
    eye = jnp.eye(n_grp, dtype=f32)
    blk_b = lambda m: jnp.einsum("dgps,gh->dgshp", m, eye).reshape(n_dir, n_grp * n_ch, n_grp * n_state)
    blk_c = lambda m: jnp.einsum("dgsp,gh->dgphs", m, eye).reshape(n_dir, n_grp * n_state, n_grp * n_ch)
    bmat = jnp.concatenate([blk_b(bbr), blk_b(bbi)], axis=-1)
    cmat = jnp.concatenate([blk_c(c_re.astype(f32)), blk_c(-c_im.astype(f32))], axis=1)
    amat = jnp.stack([abr.reshape(n_dir, -1), abi.reshape(n_dir, -1)], axis=1)
    amat = jnp.broadcast_to(amat[:, :, None, :], (n_dir, 2, bsz, n_grp * n_state))
    return bmat.astype(BF16), cmat.astype(BF16), amat


def _s5_kernel(*refs, bsz, backward):
    if backward:
        (u_ref, b_ref, c_ref, a_ref, yf_ref, d_ref, w_ref, bg_ref, o_ref, xs, st, tb) = refs
    else:
        (u_ref, b_ref, c_ref, a_ref, o_ref, xs, st, tb) = refs
    n_state = a_ref.shape[-1]

    @pl.when(pl.program_id(0) == 0)
    def _():
        st[...] = jnp.zeros_like(st)

    n_half = GROUP_W // LANES
    for b in range(bsz):
        for h in range(n_half):
            tb[h, pl.ds(b, S5_TL, stride=bsz), :] = u_ref[b, :, h * LANES:(h + 1) * LANES]
    u_tb = jnp.concatenate([tb[h] for h in range(n_half)], axis=1)
    xs[...] = jnp.dot(u_tb.astype(BF16), b_ref[...], preferred_element_type=F32)

    cw = n_state // S5_LANE_SPLIT
    for c in range(S5_LANE_SPLIT):
        re = slice(c * cw, (c + 1) * cw)
        im = slice(n_state + c * cw, n_state + (c + 1) * cw)
        a_r = a_ref[0, :, re]
        a_i = a_ref[1, :, re]

        def step(t, carry, re=re, im=im, a_r=a_r, a_i=a_i):
            s_r, s_i = carry
            tt = S5_TL - 1 - t if backward else t
            rows = pl.ds(pl.multiple_of(tt * bsz, bsz), bsz)
            n_r = a_r * s_r - a_i * s_i + xs[rows, re]
            n_i = a_r * s_i + a_i * s_r + xs[rows, im]
            xs[rows, re] = n_r
            xs[rows, im] = n_i
            return n_r, n_i

        s_r, s_i = lax.fori_loop(0, S5_TL, step, (st[:, re], st[:, im]), unroll=4)
        st[:, re] = s_r
        st[:, im] = s_i

    y_tb = jnp.dot(xs[...].astype(BF16), c_ref[...], preferred_element_type=F32)
    if not backward:
        o_ref[...] = y_tb
        return
    y_sum = y_tb + yf_ref[...]
    for h in range(n_half):
        tb[h] = y_sum[:, h * LANES:(h + 1) * LANES]
    y_bt = jnp.concatenate(
        [jnp.concatenate([tb[h, pl.ds(b, S5_TL, stride=bsz), :] for h in range(n_half)], axis=1)
         for b in range(bsz)], axis=0)
    y = jax.nn.gelu(d_ref[...] * u_ref[...].reshape(S5_TL * bsz, GROUP_W) + y_bt)
    gate = jnp.dot(y.astype(BF16), w_ref[...], preferred_element_type=F32) + bg_ref[...]
    o_ref[...] = (y * jax.nn.sigmoid(gate)).reshape(bsz, S5_TL, GROUP_W)


def _s5_mixer(rest, ssm_col, bmat, cmat, amat, d_skip, w_glu, b_glu, bsz, seq):
    n_chunks = seq // S5_TL
    n_rows = S5_TL * bsz
    n_st2 = bmat.shape[-1]
    rest3 = rest.reshape(bsz, seq, rest.shape[1])
    full = lambda a: pl.BlockSpec(a.shape, lambda i: (0,) * a.ndim)
    scratch = [pltpu.VMEM((n_rows, n_st2), F32),
               pltpu.VMEM((bsz, n_st2), F32),
               pltpu.VMEM((GROUP_W // LANES, n_rows, LANES), F32)]

    def direction(d, chunk_of):
        return [
            pl.BlockSpec((bsz, S5_TL, GROUP_W), lambda i: (0, chunk_of(i), ssm_col)),
            pl.BlockSpec((None, GROUP_W, n_st2), lambda i: (d, 0, 0)),
            pl.BlockSpec((None, n_st2, GROUP_W), lambda i: (d, 0, 0)),
            pl.BlockSpec((None, 2, bsz, n_st2 // 2), lambda i: (d, 0, 0, 0)),
        ]

    y_fwd = pl.pallas_call(
        functools.partial(_s5_kernel, bsz=bsz, backward=False),
        grid=(n_chunks,),
        in_specs=direction(0, lambda i: i),
        out_specs=pl.BlockSpec((n_rows, GROUP_W), lambda i: (i, 0)),
        out_shape=jax.ShapeDtypeStruct((seq * bsz, GROUP_W), F32),
        scratch_shapes=scratch,
        compiler_params=_cparams(("arbitrary",)),
        name="s5_forward",
    )(rest3, bmat, cmat, amat)

    rev = lambda i: n_chunks - 1 - i
    d_skip, b_glu = d_skip.reshape(1, GROUP_W), b_glu.reshape(1, GROUP_W)
    out = pl.pallas_call(
        functools.partial(_s5_kernel, bsz=bsz, backward=True),
        grid=(n_chunks,),
        in_specs=direction(1, rev) + [
            pl.BlockSpec((n_rows, GROUP_W), lambda i: (rev(i), 0)),
            full(d_skip), full(w_glu), full(b_glu),
        ],
        out_specs=pl.BlockSpec((bsz, S5_TL, GROUP_W), lambda i: (0, rev(i), 0)),
        out_shape=jax.ShapeDtypeStruct((bsz, seq, GROUP_W), F32),
        scratch_shapes=scratch,
        compiler_params=_cparams(("arbitrary",)),
        name="s5_backward",
    )(rest3, bmat, cmat, amat, y_fwd, d_skip, w_glu, b_glu)
    return out.reshape(bsz * seq, GROUP_W)


FFN_CHUNK = 256
ROUTE_COLS = 8
ROUTER_LANES = 128


def _route_records(h, wr_ref):
    h_hi = h.astype(BF16)
    h_lo = (h - h_hi.astype(F32)).astype(BF16)
    logits = jnp.dot(jnp.concatenate([h_hi, h_lo, h_hi], axis=1), wr_ref[...], preferred_element_type=F32)
    lane = lax.broadcasted_iota(jnp.int32, logits.shape, 1)
    minus_inf = -jnp.inf
    l1 = jnp.where(lane < N_EXPERTS, logits, minus_inf)
    m1 = jnp.max(l1, axis=-1, keepdims=True)
    i1 = jnp.min(jnp.where(l1 == m1, lane, ROUTER_LANES), axis=-1, keepdims=True)
    l2 = jnp.where(lane == i1, minus_inf, l1)
    m2 = jnp.max(l2, axis=-1, keepdims=True)
    i2 = jnp.min(jnp.where(l2 == m2, lane, ROUTER_LANES), axis=-1, keepdims=True)
    e2 = jnp.exp(m2 - m1)
    den = 1.0 + e2
    rec = jnp.where(lane == 0, i1.astype(F32),
                    jnp.where(lane == 1, i2.astype(F32),
                              jnp.where(lane == 2, 1.0 / den,
                                        jnp.where(lane == 3, e2 / den, 0.0))))
    return rec[:, :ROUTE_COLS]


def _mixer_out_kernel(*refs, tail):
    y_refs, (gn_ref, w_ref, x_ref, g2_ref) = refs[:4], refs[4:8]
    acc = x_ref[...]
    for j, y_ref in enumerate(y_refs):
        yn = _rms(y_ref[...], gn_ref[j:j + 1, :]).astype(BF16)
        acc = acc + jnp.dot(yn, w_ref[j * GROUP_W:(j + 1) * GROUP_W, :], preferred_element_type=F32)
    h = _rms(acc, g2_ref[...])
    if tail == "route":
        wr_ref, o_ref, route_ref = refs[8:]
        _store_token_tiles(o_ref, acc)
        route_ref[...] = _route_records(h, wr_ref)
        return
    wg_hbm, wu_hbm, wd_hbm, o_ref, wg_ref, wu_ref, wd_ref, sem = refs[8:]

    @pl.when(pl.program_id(0) == 0)
    def _():
        copies = [pltpu.make_async_copy(src, dst, sem.at[k])
                  for k, (src, dst) in enumerate(((wg_hbm, wg_ref), (wu_hbm, wu_ref), (wd_hbm, wd_ref)))]
        for c in copies:
            c.start()
        for c in copies:
            c.wait()

    h = h.astype(BF16)
    acts = []
    for c in range(wg_ref.shape[0]):
        gate = jnp.dot(h, wg_ref[c], preferred_element_type=F32)
        up = jnp.dot(h, wu_ref[c], preferred_element_type=F32)
        acts.append((gate * jax.nn.sigmoid(gate) * up).astype(BF16))
    o_ref[...] = acc + jnp.dot(jnp.concatenate(acts, axis=1), wd_ref[...], preferred_element_type=F32)


def _mixer_out(ys, gn, w_out, x2, g2, tail, tail_weights, tm=512):
    t, d = x2.shape
    row = lambda n: pl.BlockSpec((tm, n), lambda i: (i, 0))
    whole = lambda a: pl.BlockSpec(a.shape, lambda i: (0, 0))
    if tail == "route":
        wr = jnp.zeros((d, ROUTER_LANES), F32).at[:, :N_EXPERTS].set(tail_weights[0].astype(F32))
        wr_hi = wr.astype(BF16)
        wr_lo = (wr - wr_hi.astype(F32)).astype(BF16)
        tail_weights = (jnp.concatenate([wr_hi, wr_hi, wr_lo], axis=0),)
        tail_specs = [whole(w) for w in tail_weights]
        out_specs = [pl.BlockSpec((tm * SUBLANES, LANES), lambda i: (i, 0)), row(ROUTE_COLS)]
        out_shape = [jax.ShapeDtypeStruct((t * SUBLANES, LANES), F32), jax.ShapeDtypeStruct((t, ROUTE_COLS), F32)]
        scratch = []
    else:
        w_gate, w_up, w_down = tail_weights
        n_chunks = w_gate.shape[1] // FFN_CHUNK
        by_cols = lambda w: jnp.transpose(w.reshape(d, n_chunks, FFN_CHUNK), (1, 0, 2))
        tail_weights = (by_cols(w_gate), by_cols(w_up), w_down)
        tail_specs = [pl.BlockSpec(memory_space=pl.ANY)] * 3
        out_specs, out_shape = row(d), jax.ShapeDtypeStruct((t, d), F32)
        scratch = [pltpu.VMEM(w.shape, w.dtype) for w in tail_weights] + [pltpu.SemaphoreType.DMA((3,))]
    return pl.pallas_call(
        functools.partial(_mixer_out_kernel, tail=tail),
        grid=(t // tm,),
        in_specs=[row(GROUP_W)] * 4 + [whole(gn), whole(w_out), row(d), whole(g2)] + tail_specs,
        out_specs=out_specs,
        out_shape=out_shape,
        scratch_shapes=scratch,
        compiler_params=_cparams(("arbitrary",)),
        name="mixer_out_" + tail,
    )(*ys, gn, w_out, x2, g2, *tail_weights)


def _moe_plan(route, tm, n_tiles):
    ids = route[:, :2]
    experts = jnp.where((ids >= 0) & (ids <= N_EXPERTS - 1), ids, 0).astype(jnp.int32).reshape(-1)
    onehot = (experts[:, None] == jnp.arange(N_EXPERTS, dtype=jnp.int32)[None, :]).astype(jnp.int32)
    csum = jnp.cumsum(onehot, axis=0)
    rank = jnp.sum((csum - onehot) * onehot, axis=1)
    count = csum[-1]
    padded = ((count + tm - 1) // tm) * tm
    end = jnp.cumsum(padded)
    start = end - padded
    slot = jnp.sum(onehot * start[None, :], axis=1) + rank
    n_active = end[-1] // tm
    tile_start = jnp.arange(n_tiles, dtype=jnp.int32) * tm
    tile_expert = jnp.sum((tile_start[:, None] >= end[None, :]).astype(jnp.int32), axis=1)
    tile_expert = jnp.minimum(tile_expert, N_EXPERTS - 1)
    last = tile_expert[jnp.maximum(n_active - 1, 0)]
    tile_expert = jnp.where(jnp.arange(n_tiles) < n_active, tile_expert, last)
    return slot.astype(jnp.int32), tile_expert.astype(jnp.int32), n_active.astype(jnp.int32).reshape(1)


def _token_copy(src, src_tok, dst, dst_tok, sem):
    rows = lambda tok: pl.ds(pl.multiple_of(tok * SUBLANES, SUBLANES), SUBLANES)
    return pltpu.make_async_copy(src.at[rows(src_tok)], dst.at[rows(dst_tok)], sem)


def _invert_kernel(slot_ref, word_ref, pad_hbm, plan_ref, sem):
    @pl.when(pl.program_id(0) == 0)
    def _():
        init = pltpu.make_async_copy(pad_hbm, plan_ref, sem)
        init.start()
        init.wait()

    def place(a, carry):
        plan_ref[slot_ref[0, 0, a]] = word_ref[0, 0, a]
        return carry

    lax.fori_loop(0, slot_ref.shape[-1], place, 0, unroll=8)


def _plan_bits(n_tok, tm):
    src_bits = (n_tok - 1).bit_length()
    assert (2 * n_tok + 2 * tm) << src_bits < 2 ** 31 and tm & (tm - 1) == 0
    return src_bits


def _invert(slot, n_slots, tm, n_tok, chunk=4096):
    src_bits = _plan_bits(n_tok, tm)
    a = jnp.arange(slot.shape[0], dtype=jnp.int32)
    tok = a >> 1
    word = tok + (((a & 1) * n_tok + tok) << src_bits)
    s = jnp.arange(n_slots, dtype=jnp.int32)
    pad_plan = (2 * n_tok + (s & (2 * tm - 1))) << src_bits
    n_chunks = slot.shape[0] // chunk
    chunked = pl.BlockSpec((1, 1, chunk), lambda c: (c, 0, 0), memory_space=pltpu.SMEM)
    return pl.pallas_call(
        _invert_kernel,
        grid=(n_chunks,),
        in_specs=[chunked, chunked, pl.BlockSpec(memory_space=pl.ANY)],
        out_specs=pl.BlockSpec((n_slots,), lambda c: (0,), memory_space=pltpu.SMEM),
        out_shape=jax.ShapeDtypeStruct((n_slots,), jnp.int32),
        scratch_shapes=[pltpu.SemaphoreType.DMA(())],
        compiler_params=pltpu.CompilerParams(dimension_semantics=("arbitrary",)),
        name="moe_invert",
    )(slot.reshape(n_chunks, 1, chunk), word.reshape(n_chunks, 1, chunk), pad_plan)


def _tokens_done(ref, n_tok, sem):
    rows = pl.ds(0, n_tok * SUBLANES)
    pltpu.make_async_copy(ref.at[rows], ref.at[rows], sem).wait()


def _moe_ffn_kernel(te_ref, nact_ref, plan_ref, x_hbm, g_ref, wg_ref, wu_ref, wd_ref, yk_hbm,
                    xbuf, xb16, ybuf, acc_ref, sem_in, sem_out, *, tm, n_tok, n_f, pad_tile, src_bits):
    del te_ref
    i = pl.program_id(0)
    f = pl.program_id(1)
    par = i % 2
    nact = nact_ref[0]

    def fetch(tile, p, j):
        src = plan_ref[tile * tm + j] & ((1 << src_bits) - 1)
        _token_copy(x_hbm, src, xbuf.at[p], j, sem_in.at[p]).start(priority=1)

    def send(tile, p, j):
        dst = plan_ref[tile * tm + j] >> src_bits
        _token_copy(ybuf.at[p], j, yk_hbm, dst, sem_out.at[p]).start()

    @pl.when((i == 0) & (f == 0))
    def _():
        def first(j, carry):
            fetch(0, 0, j)
            return carry
        lax.fori_loop(0, tm, first, 0)
        ybuf[...] = jnp.zeros_like(ybuf)
        for p in range(2):
            spare_rows = pl.ds((2 * n_tok + p * tm) * SUBLANES, tm * SUBLANES)
            spare = pltpu.make_async_copy(ybuf.at[p], yk_hbm.at[spare_rows], sem_out.at[p])
            spare.start()
            spare.wait()
        _tokens_done(xbuf.at[0], tm, sem_in.at[0])

    @pl.when((i >= 1) & (i <= nact) & (f == 0))
    def _():
        _tokens_done(xbuf.at[par], tm, sem_in.at[par])

    @pl.when(i < nact)
    def _():
        @pl.when(f == 0)
        def _():
            xb16[...] = _rms(_load_token_tiles(xbuf.at[par], tm), g_ref[...]).astype(BF16)
            acc_ref[...] = jnp.zeros_like(acc_ref)

            for j in range(tm):
                fetch(i + 1, 1 - par, j)

        @pl.when(f == n_f - 1)
        def _():
            prev = jnp.where(i > 0, i - 1, pad_tile)

            for j in range(tm):
                send(prev, 1 - par, j)

        x = xb16[...]
        gate = jnp.dot(x, wg_ref[...], preferred_element_type=F32)
        up = jnp.dot(x, wu_ref[...], preferred_element_type=F32)
        act = (gate * jax.nn.sigmoid(gate) * up).astype(BF16)
        acc_ref[...] += jnp.dot(act, wd_ref[...], preferred_element_type=F32)

        @pl.when(f == n_f - 1)
        def _():
            @pl.when(i >= 1)
            def _():
                _tokens_done(ybuf.at[par], tm, sem_out.at[par])

            _store_token_tiles(ybuf.at[par], acc_ref[...])

    @pl.when((i == nact) & (f == 0))
    def _():
        def last(j, carry):
            send(i - 1, 1 - par, j)
            return carry
        lax.fori_loop(0, tm, last, 0)
        _tokens_done(ybuf.at[0], tm, sem_out.at[0])
        _tokens_done(ybuf.at[1], tm, sem_out.at[1])


def _moe_ffn(xt, g, plan, tile_expert, n_active, wg, wu, wd, tm):
    n_tok = xt.shape[0] // SUBLANES
    d = SUBLANES * LANES
    n_tiles = tile_expert.shape[0]
    pad_tile = n_tiles - 1 if (n_tiles - 1) % 2 else n_tiles
    assert plan.shape[0] >= (pad_tile + 1) * tm
    dff = wg.shape[-1]
    n_f = 2
    tf = dff // n_f
    fidx = lambda i, f, na: jnp.where(i < na[0], f, n_f - 1)
    grid_spec = pltpu.PrefetchScalarGridSpec(
        num_scalar_prefetch=3,
        grid=(n_tiles, n_f),
        in_specs=[
            pl.BlockSpec(memory_space=pl.ANY),
            pl.BlockSpec((1, d), lambda i, f, te, na, plan: (0, 0)),
            pl.BlockSpec((None, d, tf), lambda i, f, te, na, plan: (te[i], 0, fidx(i, f, na))),
            pl.BlockSpec((None, d, tf), lambda i, f, te, na, plan: (te[i], 0, fidx(i, f, na))),
            pl.BlockSpec((None, tf, d), lambda i, f, te, na, plan: (te[i], fidx(i, f, na), 0)),
        ],
        out_specs=pl.BlockSpec(memory_space=pl.ANY),
        scratch_shapes=[
            pltpu.VMEM((2, tm * SUBLANES, LANES), F32),
            pltpu.VMEM((tm, d), BF16),
            pltpu.VMEM((2, tm * SUBLANES, LANES), F32),
            pltpu.VMEM((tm, d), F32),
            pltpu.SemaphoreType.DMA((2,)),
            pltpu.SemaphoreType.DMA((2,)),
        ],
    )
    return pl.pallas_call(
        functools.partial(_moe_ffn_kernel, tm=tm, n_tok=n_tok, n_f=n_f, pad_tile=pad_tile,
                          src_bits=_plan_bits(n_tok, tm)),
        grid_spec=grid_spec,
        out_shape=jax.ShapeDtypeStruct(((2 * n_tok + 2 * tm) * SUBLANES, LANES), F32),
        compiler_params=_cparams(("arbitrary", "arbitrary")),
        name="moe_ffn",
    )(tile_expert, n_active, plan, xt, g.reshape(1, d), wg, wu, wd)


def _combine_kernel(route_ref, x_ref, y0_ref, y1_ref, fg_ref, o_ref, *, final_norm):
    r = route_ref[...]
    tm = r.shape[0]
    y = (_load_token_tiles(x_ref, tm) + r[:, 2:3] * _load_token_tiles(y0_ref, tm)
         + r[:, 3:4] * _load_token_tiles(y1_ref, tm))
    o_ref[...] = _rms(y, fg_ref[...]) if final_norm else y


def _combine(yk, route, xt, final_g, final_norm, tm=512):
    t = xt.shape[0] // SUBLANES
    d = SUBLANES * LANES
    n_blk = t // tm
    tiles = lambda f: pl.BlockSpec((tm * SUBLANES, LANES), f)
    return pl.pallas_call(
        functools.partial(_combine_kernel, final_norm=final_norm),
        grid=(n_blk,),
        in_specs=[
            pl.BlockSpec((tm, ROUTE_COLS), lambda i: (i, 0)),
            tiles(lambda i: (i, 0)),
            tiles(lambda i: (i, 0)),
            tiles(lambda i: (n_blk + i, 0)),
            pl.BlockSpec((1, d), lambda i: (0, 0)),
        ],
        out_specs=pl.BlockSpec((tm, d), lambda i: (i, 0)),
        out_shape=jax.ShapeDtypeStruct((t, d), F32),
        compiler_params=_cparams(("arbitrary",)),
        name="moe_combine",
    )(route, xt, yk, yk, final_g.reshape(1, d))


def _final_norm_kernel(x_ref, g_ref, o_ref):
    o_ref[...] = _rms(x_ref[...], g_ref[...])


def _final_norm(x2, g, tm=512):
    t, d = x2.shape
    return pl.pallas_call(
        _final_norm_kernel,
        grid=(t // tm,),
        in_specs=[pl.BlockSpec((tm, d), lambda i: (i, 0)), pl.BlockSpec((1, d), lambda i: (0, 0))],
        out_specs=pl.BlockSpec((tm, d), lambda i: (i, 0)),
        out_shape=jax.ShapeDtypeStruct((t, d), F32),
        compiler_params=_cparams(("arbitrary",)),
        name="final_norm",
    )(x2, g.reshape(1, d))


MOE_TM = 512


def kernel(x, norm1_g, w_in, na_rpb, sc_conv_w, cf_conv_w, cf_conv_b, cf_ln_g, cf_ln_b, ssm_a_re, ssm_a_im,
           ssm_log_dt, ssm_b_re, ssm_b_im, ssm_c_re, ssm_c_im, ssm_d, ssm_w_glu, ssm_b_glu, grp_norm_g, w_out,
           norm2_g, ffn_w_gate, ffn_w_up, ffn_w_down, moe_w_router, moe_w_gate, moe_w_up, moe_w_down,
           final_norm_g):
    bsz, seq, d = x.shape
    depth = w_in.shape[0]
    t = bsz * seq
    rows = seq // GRID_W
    x2 = x.reshape(t, d).astype(F32)
    final_done = False
    for i in range(depth):
        qkv, rest = _in_proj(x2, norm1_g[i], w_in[i].astype(BF16))
        y_na = _na_attention(qkv, _na_bias_table(na_rpb[i], rows), bsz, seq)
        y_sc, y_cf = _conv_mixers(rest, sc_conv_w[i], cf_conv_w[i], cf_conv_b[i], cf_ln_g[i], cf_ln_b[i],
                                  bsz, seq)
        bmat, cmat, amat = _s5_discretise(ssm_a_re[i], ssm_a_im[i], ssm_log_dt[i], ssm_b_re[i], ssm_b_im[i],
                                          ssm_c_re[i], ssm_c_im[i], bsz)
        y_ssm = _s5_mixer(rest, SSM_COL, bmat, cmat, amat, ssm_d[i], ssm_w_glu[i].astype(BF16), ssm_b_glu[i],
                          bsz, seq)
        ys = (y_na, y_sc, y_cf, y_ssm)
        gn, g2 = grp_norm_g[i].reshape(4, GROUP_W), norm2_g[i].reshape(1, d)
        j = i // 2
        if i % 2 == 0:
            ffn_w = (ffn_w_gate[j].astype(BF16), ffn_w_up[j].astype(BF16), ffn_w_down[j].astype(BF16))
            x2 = _mixer_out(ys, gn, w_out[i].astype(BF16), x2, g2, "ffn", ffn_w)
        else:
            x2, route = _mixer_out(ys, gn, w_out[i].astype(BF16), x2, g2, "route", (moe_w_router[j],))
            n_tiles = (2 * t) // MOE_TM + N_EXPERTS + 1
            slot, tile_expert, n_active = _moe_plan(route, MOE_TM, n_tiles)
            plan = _invert(slot, (n_tiles + 1) * MOE_TM, MOE_TM, t)
            yk = _moe_ffn(x2, norm2_g[i], plan, tile_expert, n_active, moe_w_gate[j].astype(BF16),
                          moe_w_up[j].astype(BF16), moe_w_down[j].astype(BF16), MOE_TM)
            final_done = i == depth - 1
            x2 = _combine(yk, route, x2, final_norm_g, final_done)
    if not final_done:
        x2 = _final_norm(x2, final_norm_g)
    return x2.reshape(bsz, seq, d)
```

```python
import functools
import math

import jax
import jax.numpy as jnp
import numpy as np
from jax import lax
from jax.experimental import pallas as pl
from jax.experimental.pallas import tpu as pltpu

F32 = jnp.float32
BF16 = jnp.bfloat16
EPS = 1e-6
NEG_INF = -1e30

GRID_W = 64
NA_HEAD_DIM = 64
NA_WIN_H = 8
NA_WIN_W = 16
GROUP_W = 256
SSM_GROUP_CH = 16
SSM_STATE = 64
N_EXPERTS = 8
VMEM_LIMIT = 56 * 1024 * 1024


def _cparams(sem, vmem=VMEM_LIMIT):
    return pltpu.CompilerParams(dimension_semantics=sem, vmem_limit_bytes=vmem)


def _rms(x, g):
    return x * lax.rsqrt(jnp.mean(x * x, axis=-1, keepdims=True) + EPS) * g


SUBLANES = 8
LANES = 128


def _load_token_tiles(ref, n):
    return jnp.concatenate([ref[pl.ds(s, n, stride=SUBLANES), :] for s in range(SUBLANES)], axis=1)


def _store_token_tiles(ref, val):
    for s in range(SUBLANES):
        ref[pl.ds(s, val.shape[0], stride=SUBLANES), :] = val[:, s * LANES:(s + 1) * LANES]


N_QKV_COLS = 3 * GROUP_W
SSM_COL = 5


def _in_proj_kernel(x_ref, g_ref, w_ref, qkv_ref, rest_ref):
    h = _rms(x_ref[...], g_ref[...]).astype(BF16)
    n_qkv = qkv_ref.shape[1] // GROUP_W
    n_rest = rest_ref.shape[1] // GROUP_W
    for j in range(n_qkv + n_rest):
        z = jnp.dot(h, w_ref[:, j * GROUP_W:(j + 1) * GROUP_W], preferred_element_type=F32)
        if j < n_qkv:
            qkv_ref[:, j * GROUP_W:(j + 1) * GROUP_W] = z.astype(BF16)
        else:
            k = j - n_qkv
            rest_ref[:, k * GROUP_W:(k + 1) * GROUP_W] = z


def _in_proj(x2, g, w, tm=512):
    t, d = x2.shape
    n_cols = w.shape[1]
    n_rest = n_cols - N_QKV_COLS
    return pl.pallas_call(
        _in_proj_kernel,
        grid=(t // tm,),
        in_specs=[
            pl.BlockSpec((tm, d), lambda i: (i, 0)),
            pl.BlockSpec((1, d), lambda i: (0, 0)),
            pl.BlockSpec((d, n_cols), lambda i: (0, 0)),
        ],
        out_specs=[
            pl.BlockSpec((tm, N_QKV_COLS), lambda i: (i, 0)),
            pl.BlockSpec((tm, n_rest), lambda i: (i, 0)),
        ],
        out_shape=[
            jax.ShapeDtypeStruct((t, N_QKV_COLS), BF16),
            jax.ShapeDtypeStruct((t, n_rest), F32),
        ],
        compiler_params=_cparams(("arbitrary",)),
        name="in_proj",
    )(x2, g.reshape(1, d), w)


def _na_bias_table(rpb, rows):
    n_heads = rpb.shape[0]
    kh = NA_WIN_H
    c = np.arange(GRID_W)
    qcs = np.clip(c - NA_WIN_W // 2, 0, GRID_W - NA_WIN_W)
    kc = np.arange(GRID_W)
    in_win = (kc[None, :] >= qcs[:, None]) & (kc[None, :] < qcs[:, None] + NA_WIN_W)
    dc_idx = np.clip(kc[None, :] - c[:, None] + NA_WIN_W - 1, 0, 2 * NA_WIN_W - 2)
    onehot_dc = (dc_idx[:, :, None] == np.arange(2 * NA_WIN_W - 1)).astype(np.float32)
    rep_rows = np.array([0, 1, 2, 3, 4, rows - 3, rows - 2, rows - 1])
    row_start = np.clip(rep_rows - kh // 2, 0, rows - kh)
    dr_idx = (row_start - rep_rows + NA_WIN_H - 1)[:, None] + np.arange(kh)[None, :]
    onehot_dr = (dr_idx[:, :, None] == np.arange(2 * NA_WIN_H - 1)).astype(np.float32)
    tab = jnp.einsum("via,hab,ckb->vhcik", jnp.asarray(onehot_dr), rpb.astype(F32), jnp.asarray(onehot_dc),
                     precision=lax.Precision.HIGHEST)
    tab = jnp.where(jnp.asarray(in_win)[None, None, :, None, :], tab, NEG_INF)
    return tab.reshape(len(rep_rows), n_heads * GRID_W, kh * GRID_W)


def _na_kernel(q_ref, k_ref, v_ref, bias_ref, o_ref, *, rows, n_heads):
    kh = NA_WIN_H
    lane = lax.broadcasted_iota(jnp.int32, (GRID_W, n_heads * NA_HEAD_DIM), 1)
    masks = [(lane >= h * NA_HEAD_DIM) & (lane < (h + 1) * NA_HEAD_DIM) for h in range(n_heads)]
    scale = NA_HEAD_DIM ** -0.5

    def body(r, carry):
        q0 = pl.multiple_of(r * GRID_W, GRID_W)
        q = q_ref[pl.ds(q0, GRID_W), :].astype(F32) * scale
        qs = jnp.concatenate([jnp.where(m, q, 0.0) for m in masks], axis=0).astype(BF16)
        rs = jnp.clip(r - kh // 2, 0, rows - kh)
        k0 = pl.multiple_of(rs * GRID_W, GRID_W)
        ks = k_ref[pl.ds(k0, kh * GRID_W), :]
        vs = v_ref[pl.ds(k0, kh * GRID_W), :]
        s = lax.dot_general(qs, ks, (((1,), (1,)), ((), ())), preferred_element_type=F32)
        var = jnp.where(r < kh // 2, r, jnp.where(r > rows - kh // 2, r - (rows - 8), kh // 2))
        s = s + bias_ref[var]
        m = jnp.max(s, axis=-1, keepdims=True)
        p = jnp.exp(s - m)
        den = jnp.sum(p, axis=-1, keepdims=True)
        o = jnp.dot(p.astype(BF16), vs, preferred_element_type=F32) / den
        out = jnp.where(masks[0], o[0:GRID_W], 0.0)
        for h in range(1, n_heads):
            out = out + jnp.where(masks[h], o[h * GRID_W:(h + 1) * GRID_W], 0.0)
        o_ref[pl.ds(q0, GRID_W), :] = out
        return carry

    lax.fori_loop(0, rows, body, 0, unroll=8)


def _na_attention(qkv, bias, bsz, seq):
    rows = seq // GRID_W
    n_heads = GROUP_W // NA_HEAD_DIM
    t = qkv.shape[0]
    blk = lambda j: pl.BlockSpec((seq, GROUP_W), lambda b, j=j: (b, j))
    return pl.pallas_call(
        functools.partial(_na_kernel, rows=rows, n_heads=n_heads),
        grid=(bsz,),
        in_specs=[blk(0), blk(1), blk(2),
                  pl.BlockSpec(bias.shape, lambda b: (0, 0, 0))],
        out_specs=pl.BlockSpec((seq, GROUP_W), lambda b: (b, 0)),
        out_shape=jax.ShapeDtypeStruct((t, GROUP_W), F32),
        compiler_params=_cparams(("arbitrary",)),
        name="na_attention",
    )(qkv, qkv, qkv, bias)


CONV_PAD = 16
CONV_CHUNK = 256


def _conv_kernel(b_ref, c_ref, x_ref, a_ref, g_ref, scw_ref, cfw_ref, cfb_ref, lng_ref, lnb_ref,
                 ysc_ref, ycf_ref, psc, pcf):
    seq = b_ref.shape[0]
    n_chunks = seq // CONV_CHUNK
    zeros = jnp.zeros((CONV_PAD, GROUP_W), F32)
    psc[0:CONV_PAD, :] = zeros
    pcf[0:CONV_PAD, :] = zeros
    psc[CONV_PAD + seq:2 * CONV_PAD + seq, :] = zeros
    pcf[CONV_PAD + seq:2 * CONV_PAD + seq, :] = zeros

    def fill(i, carry):
        r0 = pl.multiple_of(i * CONV_CHUNK, CONV_CHUNK)
        rows = pl.ds(r0, CONV_CHUNK)
        dst = pl.ds(r0 + CONV_PAD, CONV_CHUNK)
        psc[dst, :] = c_ref[rows, :] * x_ref[rows, :]
        pcf[dst, :] = a_ref[rows, :] * jax.nn.sigmoid(g_ref[rows, :])
        return carry

    lax.fori_loop(0, n_chunks, fill, 0)

    def taps(win, w_ref):
        n_taps = w_ref.shape[0]
        offs = [CONV_PAD + k - n_taps // 2 for k in range(n_taps)]
        n_win = win.shape[0]
        acc = None
        for s in range(SUBLANES):
            ks = [k for k in range(n_taps) if offs[k] % SUBLANES == s]
            if not ks:
                continue
            shifted = pltpu.roll(win, n_win - s, axis=0) if s else win
            for k in ks:
                q = offs[k] - s
                term = shifted[q:q + CONV_CHUNK, :] * w_ref[k:k + 1, :]
                acc = term if acc is None else acc + term
        return acc

    def conv(i, carry):
        r0 = pl.multiple_of(i * CONV_CHUNK, CONV_CHUNK)
        rows = pl.ds(r0, CONV_CHUNK)
        window = pl.ds(r0, CONV_CHUNK + 2 * CONV_PAD)
        ysc_ref[rows, :] = b_ref[rows, :] * taps(psc[window, :], scw_ref)
        cf = taps(pcf[window, :], cfw_ref) + cfb_ref[...]
        mu = jnp.mean(cf, axis=-1, keepdims=True)
        xc = cf - mu
        var = jnp.mean(xc * xc, axis=-1, keepdims=True)
        ln = xc * lax.rsqrt(var + EPS) * lng_ref[...] + lnb_ref[...]
        ycf_ref[rows, :] = ln * jax.nn.sigmoid(ln)
        return carry

    lax.fori_loop(0, n_chunks, conv, 0)


def _conv_mixers(rest, sc_w, cf_w, cf_b, ln_g, ln_b, bsz, seq):
    t = rest.shape[0]
    blk = lambda j: pl.BlockSpec((seq, GROUP_W), lambda b, j=j: (b, j))
    full = lambda a: pl.BlockSpec(a.shape, lambda b: (0,) * a.ndim)
    cf_b, ln_g, ln_b = (a.reshape(1, GROUP_W) for a in (cf_b, ln_g, ln_b))
    out_spec = pl.BlockSpec((seq, GROUP_W), lambda b: (b, 0))
    return pl.pallas_call(
        _conv_kernel,
        grid=(bsz,),
        in_specs=[blk(0), blk(1), blk(2), blk(3), blk(4),
                  full(sc_w), full(cf_w), full(cf_b), full(ln_g), full(ln_b)],
        out_specs=[out_spec, out_spec],
        out_shape=[jax.ShapeDtypeStruct((t, GROUP_W), F32)] * 2,
        scratch_shapes=[pltpu.VMEM((seq + 2 * CONV_PAD, GROUP_W), F32)] * 2,
        compiler_params=_cparams(("arbitrary",)),
        name="conv_mixers",
    )(rest, rest, rest, rest, rest, sc_w, cf_w, cf_b, ln_g, ln_b)


S5_TL = 128
S5_LANE_SPLIT = 1


def _s5_discretise(a_re, a_im, log_dt, b_re, b_im, c_re, c_im, bsz):
    f32 = F32
    a_re, a_im, log_dt = a_re.astype(f32), a_im.astype(f32), log_dt.astype(f32)
    n_dir, n_grp, n_state = a_re.shape
    n_ch = b_re.shape[-1]
    dt = jnp.exp(log_dt)[..., None]
    mag = jnp.exp(a_re * dt)
    abr = mag * jnp.cos(a_im * dt)
    abi = mag * jnp.sin(a_im * dt)
    den = a_re * a_re + a_im * a_im
    fr = ((abr - 1.0) * a_re + abi * a_im) / den
    fi = (abi * a_re - (abr - 1.0) * a_im) / den
    bbr = fr[..., None] * b_re - fi[..., None] * b_im
    bbi = fr[..., None] * b_im + fi[..., None] * b_re
    eye = jnp.eye(n_grp, dtype=f32)
    blk_b = lambda m: jnp.einsum("dgps,gh->dgshp", m, eye).reshape(n_dir, n_grp * n_ch, n_grp * n_state)
    blk_c = lambda m: jnp.einsum("dgsp,gh->dgphs", m, eye).reshape(n_dir, n_grp * n_state, n_grp * n_ch)
    bmat = jnp.concatenate([blk_b(bbr), blk_b(bbi)], axis=-1)
    cmat = jnp.concatenate([blk_c(c_re.astype(f32)), blk_c(-c_im.astype(f32))], axis=1)
    amat = jnp.stack([abr.reshape(n_dir, -1), abi.reshape(n_dir, -1)], axis=1)
    amat = jnp.broadcast_to(amat[:, :, None, :], (n_dir, 2, bsz, n_grp * n_state))
    return bmat.astype(BF16), cmat.astype(BF16), amat


def _s5_kernel(*refs, bsz, backward):
    if backward:
        (u_ref, b_ref, c_ref, a_ref, yf_ref, d_ref, w_ref, bg_ref, o_ref, xs, st, tb) = refs
    else:
        (u_ref, b_ref, c_ref, a_ref, o_ref, xs, st, tb) = refs
    n_state = a_ref.shape[-1]

    @pl.when(pl.program_id(0) == 0)
    def _():
        st[...] = jnp.zeros_like(st)

    n_half = GROUP_W // LANES
    for b in range(bsz):
        for h in range(n_half):
            tb[h, pl.ds(b, S5_TL, stride=bsz), :] = u_ref[b, :, h * LANES:(h + 1) * LANES]
    u_tb = jnp.concatenate([tb[h] for h in range(n_half)], axis=1)
    xs[...] = jnp.dot(u_tb.astype(BF16), b_ref[...], preferred_element_type=F32)

    cw = n_state // S5_LANE_SPLIT
    for c in range(S5_LANE_SPLIT):
        re = slice(c * cw, (c + 1) * cw)
        im = slice(n_state + c * cw, n_state + (c + 1) * cw)
        a_r = a_ref[0, :, re]
        a_i = a_ref[1, :, re]

        def step(t, carry, re=re, im=im, a_r=a_r, a_i=a_i):
            s_r, s_i = carry
            tt = S5_TL - 1 - t if backward else t
            rows = pl.ds(pl.multiple_of(tt * bsz, bsz), bsz)
            n_r = a_r * s_r - a_i * s_i + xs[rows, re]
            n_i = a_r * s_i + a_i * s_r + xs[rows, im]
            xs[rows, re] = n_r
            xs[rows, im] = n_i
            return n_r, n_i

        s_r, s_i = lax.fori_loop(0, S5_TL, step, (st[:, re], st[:, im]), unroll=4)
        st[:, re] = s_r
        st[:, im] = s_i

    y_tb = jnp.dot(xs[...].astype(BF16), c_ref[...], preferred_element_type=F32)
    if not backward:
        o_ref[...] = y_tb
        return
    y_sum = y_tb + yf_ref[...]
    for h in range(n_half):
        tb[h] = y_sum[:, h * LANES:(h + 1) * LANES]
    y_bt = jnp.concatenate(
        [jnp.concatenate([tb[h, pl.ds(b, S5_TL, stride=bsz), :] for h in range(n_half)], axis=1)
         for b in range(bsz)], axis=0)
    y = jax.nn.gelu(d_ref[...] * u_ref[...].reshape(S5_TL * bsz, GROUP_W) + y_bt)
    gate = jnp.dot(y.astype(BF16), w_ref[...], preferred_element_type=F32) + bg_ref[...]
    o_ref[...] = (y * jax.nn.sigmoid(gate)).reshape(bsz, S5_TL, GROUP_W)


def _s5_mixer(rest, ssm_col, bmat, cmat, amat, d_skip, w_glu, b_glu, bsz, seq):
    n_chunks = seq // S5_TL
    n_rows = S5_TL * bsz
    n_st2 = bmat.shape[-1]
    rest3 = rest.reshape(bsz, seq, rest.shape[1])
    full = lambda a: pl.BlockSpec(a.shape, lambda i: (0,) * a.ndim)
    scratch = [pltpu.VMEM((n_rows, n_st2), F32),
               pltpu.VMEM((bsz, n_st2), F32),
               pltpu.VMEM((GROUP_W // LANES, n_rows, LANES), F32)]

    def direction(d, chunk_of):
        return [
            pl.BlockSpec((bsz, S5_TL, GROUP_W), lambda i: (0, chunk_of(i), ssm_col)),
            pl.BlockSpec((None, GROUP_W, n_st2), lambda i: (d, 0, 0)),
            pl.BlockSpec((None, n_st2, GROUP_W), lambda i: (d, 0, 0)),
            pl.BlockSpec((None, 2, bsz, n_st2 // 2), lambda i: (d, 0, 0, 0)),
        ]

    y_fwd = pl.pallas_call(
        functools.partial(_s5_kernel, bsz=bsz, backward=False),
        grid=(n_chunks,),
        in_specs=direction(0, lambda i: i),
        out_specs=pl.BlockSpec((n_rows, GROUP_W), lambda i: (i, 0)),
        out_shape=jax.ShapeDtypeStruct((seq * bsz, GROUP_W), F32),
        scratch_shapes=scratch,
        compiler_params=_cparams(("arbitrary",)),
        name="s5_forward",
    )(rest3, bmat, cmat, amat)

    rev = lambda i: n_chunks - 1 - i
    d_skip, b_glu = d_skip.reshape(1, GROUP_W), b_glu.reshape(1, GROUP_W)
    out = pl.pallas_call(
        functools.partial(_s5_kernel, bsz=bsz, backward=True),
        grid=(n_chunks,),
        in_specs=direction(1, rev) + [
            pl.BlockSpec((n_rows, GROUP_W), lambda i: (rev(i), 0)),
            full(d_skip), full(w_glu), full(b_glu),
        ],
        out_specs=pl.BlockSpec((bsz, S5_TL, GROUP_W), lambda i: (0, rev(i), 0)),
        out_shape=jax.ShapeDtypeStruct((bsz, seq, GROUP_W), F32),
        scratch_shapes=scratch,
        compiler_params=_cparams(("arbitrary",)),
        name="s5_backward",
    )(rest3, bmat, cmat, amat, y_fwd, d_skip, w_glu, b_glu)
    return out.reshape(bsz * seq, GROUP_W)


FFN_CHUNK = 256
ROUTE_COLS = 8
ROUTER_LANES = 128


def _route_records(h, wr_ref):
    h_hi = h.astype(BF16)
    h_lo = (h - h_hi.astype(F32)).astype(BF16)
    logits = jnp.dot(jnp.concatenate([h_hi, h_lo, h_hi], axis=1), wr_ref[...], preferred_element_type=F32)
    lane = lax.broadcasted_iota(jnp.int32, logits.shape, 1)
    minus_inf = -jnp.inf
    l1 = jnp.where(lane < N_EXPERTS, logits, minus_inf)
    m1 = jnp.max(l1, axis=-1, keepdims=True)
    i1 = jnp.min(jnp.where(l1 == m1, lane, ROUTER_LANES), axis=-1, keepdims=True)
    l2 = jnp.where(lane == i1, minus_inf, l1)
    m2 = jnp.max(l2, axis=-1, keepdims=True)
    i2 = jnp.min(jnp.where(l2 == m2, lane, ROUTER_LANES), axis=-1, keepdims=True)
    e2 = jnp.exp(m2 - m1)
    den = 1.0 + e2
    rec = jnp.where(lane == 0, i1.astype(F32),
                    jnp.where(lane == 1, i2.astype(F32),
                              jnp.where(lane == 2, 1.0 / den,
                                        jnp.where(lane == 3, e2 / den, 0.0))))
    return rec[:, :ROUTE_COLS]


def _mixer_out_kernel(*refs, tail):
    y_refs, (gn_ref, w_ref, x_ref, g2_ref) = refs[:4], refs[4:8]
    acc = x_ref[...]
    for j, y_ref in enumerate(y_refs):
        yn = _rms(y_ref[...], gn_ref[j:j + 1, :]).astype(BF16)
        acc = acc + jnp.dot(yn, w_ref[j * GROUP_W:(j + 1) * GROUP_W, :], preferred_element_type=F32)
    h = _rms(acc, g2_ref[...])
    if tail == "route":
        wr_ref, o_ref, route_ref = refs[8:]
        _store_token_tiles(o_ref, acc)
        route_ref[...] = _route_records(h, wr_ref)
        return
    wg_hbm, wu_hbm, wd_hbm, o_ref, wg_ref, wu_ref, wd_ref, sem = refs[8:]

    @pl.when(pl.program_id(0) == 0)
    def _():
        copies = [pltpu.make_async_copy(src, dst, sem.at[k])
                  for k, (src, dst) in enumerate(((wg_hbm, wg_ref), (wu_hbm, wu_ref), (wd_hbm, wd_ref)))]
        for c in copies:
            c.start()
        for c in copies:
            c.wait()

    h = h.astype(BF16)
    acts = []
    for c in range(wg_ref.shape[0]):
        gate = jnp.dot(h, wg_ref[c], preferred_element_type=F32)
        up = jnp.dot(h, wu_ref[c], preferred_element_type=F32)
        acts.append((gate * jax.nn.sigmoid(gate) * up).astype(BF16))
    o_ref[...] = acc + jnp.dot(jnp.concatenate(acts, axis=1), wd_ref[...], preferred_element_type=F32)


def _mixer_out(ys, gn, w_out, x2, g2, tail, tail_weights, tm=512):
    t, d = x2.shape
    row = lambda n: pl.BlockSpec((tm, n), lambda i: (i, 0))
    whole = lambda a: pl.BlockSpec(a.shape, lambda i: (0, 0))
    if tail == "route":
        wr = jnp.zeros((d, ROUTER_LANES), F32).at[:, :N_EXPERTS].set(tail_weights[0].astype(F32))
        wr_hi = wr.astype(BF16)
        wr_lo = (wr - wr_hi.astype(F32)).astype(BF16)
        tail_weights = (jnp.concatenate([wr_hi, wr_hi, wr_lo], axis=0),)
        tail_specs = [whole(w) for w in tail_weights]
        out_specs = [pl.BlockSpec((tm * SUBLANES, LANES), lambda i: (i, 0)), row(ROUTE_COLS)]
        out_shape = [jax.ShapeDtypeStruct((t * SUBLANES, LANES), F32), jax.ShapeDtypeStruct((t, ROUTE_COLS), F32)]
        scratch = []
    else:
        w_gate, w_up, w_down = tail_weights
        n_chunks = w_gate.shape[1] // FFN_CHUNK
        by_cols = lambda w: jnp.transpose(w.reshape(d, n_chunks, FFN_CHUNK), (1, 0, 2))
        tail_weights = (by_cols(w_gate), by_cols(w_up), w_down)
        tail_specs = [pl.BlockSpec(memory_space=pl.ANY)] * 3
        out_specs, out_shape = row(d), jax.ShapeDtypeStruct((t, d), F32)
        scratch = [pltpu.VMEM(w.shape, w.dtype) for w in tail_weights] + [pltpu.SemaphoreType.DMA((3,))]
    return pl.pallas_call(
        functools.partial(_mixer_out_kernel, tail=tail),
        grid=(t // tm,),
        in_specs=[row(GROUP_W)] * 4 + [whole(gn), whole(w_out), row(d), whole(g2)] + tail_specs,
        out_specs=out_specs,
        out_shape=out_shape,
        scratch_shapes=scratch,
        compiler_params=_cparams(("arbitrary",)),
        name="mixer_out_" + tail,
    )(*ys, gn, w_out, x2, g2, *tail_weights)


def _moe_plan(route, tm, n_tiles):
    ids = route[:, :2]
    experts = jnp.where((ids >= 0) & (ids <= N_EXPERTS - 1), ids, 0).astype(jnp.int32).reshape(-1)
    onehot = (experts[:, None] == jnp.arange(N_EXPERTS, dtype=jnp.int32)[None, :]).astype(jnp.int32)
    csum = jnp.cumsum(onehot, axis=0)
    rank = jnp.sum((csum - onehot) * onehot, axis=1)
    count = csum[-1]
    padded = ((count + tm - 1) // tm) * tm
    end = jnp.cumsum(padded)
    start = end - padded
    slot = jnp.sum(onehot * start[None, :], axis=1) + rank
    n_active = end[-1] // tm
    tile_start = jnp.arange(n_tiles, dtype=jnp.int32) * tm
    tile_expert = jnp.sum((tile_start[:, None] >= end[None, :]).astype(jnp.int32), axis=1)
    tile_expert = jnp.minimum(tile_expert, N_EXPERTS - 1)
    last = tile_expert[jnp.maximum(n_active - 1, 0)]
    tile_expert = jnp.where(jnp.arange(n_tiles) < n_active, tile_expert, last)
    return slot.astype(jnp.int32), tile_expert.astype(jnp.int32), n_active.astype(jnp.int32).reshape(1)


def _token_copy(src, src_tok, dst, dst_tok, sem):
    rows = lambda tok: pl.ds(pl.multiple_of(tok * SUBLANES, SUBLANES), SUBLANES)
    return pltpu.make_async_copy(src.at[rows(src_tok)], dst.at[rows(dst_tok)], sem)


def _invert_kernel(slot_ref, word_ref, pad_hbm, plan_ref, sem):
    @pl.when(pl.program_id(0) == 0)
    def _():
        init = pltpu.make_async_copy(pad_hbm, plan_ref, sem)
        init.start()
        init.wait()

    def place(a, carry):
        plan_ref[slot_ref[0, 0, a]] = word_ref[0, 0, a]
        return carry

    lax.fori_loop(0, slot_ref.shape[-1], place, 0, unroll=8)


def _plan_bits(n_tok, tm):
    src_bits = (n_tok - 1).bit_length()
    assert (2 * n_tok + 2 * tm) << src_bits < 2 ** 31 and tm & (tm - 1) == 0
    return src_bits


def _invert(slot, n_slots, tm, n_tok, chunk=4096):
    src_bits = _plan_bits(n_tok, tm)
    a = jnp.arange(slot.shape[0], dtype=jnp.int32)
    tok = a >> 1
    word = tok + (((a & 1) * n_tok + tok) << src_bits)
    s = jnp.arange(n_slots, dtype=jnp.int32)
    pad_plan = (2 * n_tok + (s & (2 * tm - 1))) << src_bits
    n_chunks = slot.shape[0] // chunk
    chunked = pl.BlockSpec((1, 1, chunk), lambda c: (c, 0, 0), memory_space=pltpu.SMEM)
    return pl.pallas_call(
        _invert_kernel,
        grid=(n_chunks,),
        in_specs=[chunked, chunked, pl.BlockSpec(memory_space=pl.ANY)],
        out_specs=pl.BlockSpec((n_slots,), lambda c: (0,), memory_space=pltpu.SMEM),
        out_shape=jax.ShapeDtypeStruct((n_slots,), jnp.int32),
        scratch_shapes=[pltpu.SemaphoreType.DMA(())],
        compiler_params=pltpu.CompilerParams(dimension_semantics=("arbitrary",)),
        name="moe_invert",
    )(slot.reshape(n_chunks, 1, chunk), word.reshape(n_chunks, 1, chunk), pad_plan)


def _tokens_done(ref, n_tok, sem):
    rows = pl.ds(0, n_tok * SUBLANES)
    pltpu.make_async_copy(ref.at[rows], ref.at[rows], sem).wait()


def _moe_ffn_kernel(te_ref, nact_ref, plan_ref, x_hbm, g_ref, wg_ref, wu_ref, wd_ref, yk_hbm,
                    xbuf, xb16, ybuf, acc_ref, sem_in, sem_out, *, tm, n_tok, n_f, pad_tile, src_bits):
    del te_ref
    i = pl.program_id(0)
    f = pl.program_id(1)
    par = i % 2
    nact = nact_ref[0]

    def fetch(tile, p, j):
        src = plan_ref[tile * tm + j] & ((1 << src_bits) - 1)
        _token_copy(x_hbm, src, xbuf.at[p], j, sem_in.at[p]).start(priority=1)

    def send(tile, p, j):
        dst = plan_ref[tile * tm + j] >> src_bits
        _token_copy(ybuf.at[p], j, yk_hbm, dst, sem_out.at[p]).start()

    @pl.when((i == 0) & (f == 0))
    def _():
        def first(j, carry):
            fetch(0, 0, j)
            return carry
        lax.fori_loop(0, tm, first, 0)
        ybuf[...] = jnp.zeros_like(ybuf)
        for p in range(2):
            spare_rows = pl.ds((2 * n_tok + p * tm) * SUBLANES, tm * SUBLANES)
            spare = pltpu.make_async_copy(ybuf.at[p], yk_hbm.at[spare_rows], sem_out.at[p])
            spare.start()
            spare.wait()
        _tokens_done(xbuf.at[0], tm, sem_in.at[0])

    @pl.when((i >= 1) & (i <= nact) & (f == 0))
    def _():
        _tokens_done(xbuf.at[par], tm, sem_in.at[par])

    @pl.when(i < nact)
    def _():
        @pl.when(f == 0)
        def _():
            xb16[...] = _rms(_load_token_tiles(xbuf.at[par], tm), g_ref[...]).astype(BF16)
            acc_ref[...] = jnp.zeros_like(acc_ref)

            for j in range(tm):
                fetch(i + 1, 1 - par, j)

        @pl.when(f == n_f - 1)
        def _():
            prev = jnp.where(i > 0, i - 1, pad_tile)

            for j in range(tm):
                send(prev, 1 - par, j)

        x = xb16[...]
        gate = jnp.dot(x, wg_ref[...], preferred_element_type=F32)
        up = jnp.dot(x, wu_ref[...], preferred_element_type=F32)
        act = (gate * jax.nn.sigmoid(gate) * up).astype(BF16)
        acc_ref[...] += jnp.dot(act, wd_ref[...], preferred_element_type=F32)

        @pl.when(f == n_f - 1)
        def _():
            @pl.when(i >= 1)
            def _():
                _tokens_done(ybuf.at[par], tm, sem_out.at[par])

            _store_token_tiles(ybuf.at[par], acc_ref[...])

    @pl.when((i == nact) & (f == 0))
    def _():
        def last(j, carry):
            send(i - 1, 1 - par, j)
            return carry
        lax.fori_loop(0, tm, last, 0)
        _tokens_done(ybuf.at[0], tm, sem_out.at[0])
        _tokens_done(ybuf.at[1], tm, sem_out.at[1])


def _moe_ffn(xt, g, plan, tile_expert, n_active, wg, wu, wd, tm):
    n_tok = xt.shape[0] // SUBLANES
    d = SUBLANES * LANES
    n_tiles = tile_expert.shape[0]
    pad_tile = n_tiles - 1 if (n_tiles - 1) % 2 else n_tiles
    assert plan.shape[0] >= (pad_tile + 1) * tm
    dff = wg.shape[-1]
    n_f = 2
    tf = dff // n_f
    fidx = lambda i, f, na: jnp.where(i < na[0], f, n_f - 1)
    grid_spec = pltpu.PrefetchScalarGridSpec(
        num_scalar_prefetch=3,
        grid=(n_tiles, n_f),
        in_specs=[
            pl.BlockSpec(memory_space=pl.ANY),
            pl.BlockSpec((1, d), lambda i, f, te, na, plan: (0, 0)),
            pl.BlockSpec((None, d, tf), lambda i, f, te, na, plan: (te[i], 0, fidx(i, f, na))),
            pl.BlockSpec((None, d, tf), lambda i, f, te, na, plan: (te[i], 0, fidx(i, f, na))),
            pl.BlockSpec((None, tf, d), lambda i, f, te, na, plan: (te[i], fidx(i, f, na), 0)),
        ],
        out_specs=pl.BlockSpec(memory_space=pl.ANY),
        scratch_shapes=[
            pltpu.VMEM((2, tm * SUBLANES, LANES), F32),
            pltpu.VMEM((tm, d), BF16),
            pltpu.VMEM((2, tm * SUBLANES, LANES), F32),
            pltpu.VMEM((tm, d), F32),
            pltpu.SemaphoreType.DMA((2,)),
            pltpu.SemaphoreType.DMA((2,)),
        ],
    )
    return pl.pallas_call(
        functools.partial(_moe_ffn_kernel, tm=tm, n_tok=n_tok, n_f=n_f, pad_tile=pad_tile,
                          src_bits=_plan_bits(n_tok, tm)),
        grid_spec=grid_spec,
        out_shape=jax.ShapeDtypeStruct(((2 * n_tok + 2 * tm) * SUBLANES, LANES), F32),
        compiler_params=_cparams(("arbitrary", "arbitrary")),
        name="moe_ffn",
    )(tile_expert, n_active, plan, xt, g.reshape(1, d), wg, wu, wd)


def _combine_kernel(route_ref, x_ref, y0_ref, y1_ref, fg_ref, o_ref, *, final_norm):
    r = route_ref[...]
    tm = r.shape[0]
    y = (_load_token_tiles(x_ref, tm) + r[:, 2:3] * _load_token_tiles(y0_ref, tm)
         + r[:, 3:4] * _load_token_tiles(y1_ref, tm))
    o_ref[...] = _rms(y, fg_ref[...]) if final_norm else y


def _combine(yk, route, xt, final_g, final_norm, tm=512):
    t = xt.shape[0] // SUBLANES
    d = SUBLANES * LANES
    n_blk = t // tm
    tiles = lambda f: pl.BlockSpec((tm * SUBLANES, LANES), f)
    return pl.pallas_call(
        functools.partial(_combine_kernel, final_norm=final_norm),
        grid=(n_blk,),
        in_specs=[
            pl.BlockSpec((tm, ROUTE_COLS), lambda i: (i, 0)),
            tiles(lambda i: (i, 0)),
            tiles(lambda i: (i, 0)),
            tiles(lambda i: (n_blk + i, 0)),
            pl.BlockSpec((1, d), lambda i: (0, 0)),
        ],
        out_specs=pl.BlockSpec((tm, d), lambda i: (i, 0)),
        out_shape=jax.ShapeDtypeStruct((t, d), F32),
        compiler_params=_cparams(("arbitrary",)),
        name="moe_combine",
    )(route, xt, yk, yk, final_g.reshape(1, d))


def _final_norm_kernel(x_ref, g_ref, o_ref):
    o_ref[...] = _rms(x_ref[...], g_ref[...])


def _final_norm(x2, g, tm=512):
    t, d = x2.shape
    return pl.pallas_call(
        _final_norm_kernel,
        grid=(t // tm,),
        in_specs=[pl.BlockSpec((tm, d), lambda i: (i, 0)), pl.BlockSpec((1, d), lambda i: (0, 0))],
        out_specs=pl.BlockSpec((tm, d), lambda i: (i, 0)),
        out_shape=jax.ShapeDtypeStruct((t, d), F32),
        compiler_params=_cparams(("arbitrary",)),
        name="final_norm",
    )(x2, g.reshape(1, d))


MOE_TM = 512


def kernel(x, norm1_g, w_in, na_rpb, sc_conv_w, cf_conv_w, cf_conv_b, cf_ln_g, cf_ln_b, ssm_a_re, ssm_a_im,
           ssm_log_dt, ssm_b_re, ssm_b_im, ssm_c_re, ssm_c_im, ssm_d, ssm_w_glu, ssm_b_glu, grp_norm_g, w_out,
           norm2_g, ffn_w_gate, ffn_w_up, ffn_w_down, moe_w_router, moe_w_gate, moe_w_up, moe_w_down,
           final_norm_g):
    bsz, seq, d = x.shape
    depth = w_in.shape[0]
    t = bsz * seq
    rows = seq // GRID_W
    x2 = x.reshape(t, d).astype(F32)
    final_done = False
    na_bias = jax.vmap(lambda rpb: _na_bias_table(rpb, rows))(na_rpb)
    s5_mats = jax.vmap(lambda *p: _s5_discretise(*p, bsz))(ssm_a_re, ssm_a_im, ssm_log_dt, ssm_b_re, ssm_b_im,
                                                           ssm_c_re, ssm_c_im)
    for i in range(depth):
        qkv, rest = _in_proj(x2, norm1_g[i], w_in[i].astype(BF16))
        y_na = _na_attention(qkv, na_bias[i], bsz, seq)
        y_sc, y_cf = _conv_mixers(rest, sc_conv_w[i], cf_conv_w[i], cf_conv_b[i], cf_ln_g[i], cf_ln_b[i],
                                  bsz, seq)
        bmat, cmat, amat = (m[i] for m in s5_mats)
        y_ssm = _s5_mixer(rest, SSM_COL, bmat, cmat, amat, ssm_d[i], ssm_w_glu[i].astype(BF16), ssm_b_glu[i],
                          bsz, seq)
        ys = (y_na, y_sc, y_cf, y_ssm)
        gn, g2 = grp_norm_g[i].reshape(4, GROUP_W), norm2_g[i].reshape(1, d)
        j = i // 2
        if i % 2 == 0:
            ffn_w = (ffn_w_gate[j].astype(BF16), ffn_w_up[j].astype(BF16), ffn_w_down[j].astype(BF16))
            x2 = _mixer_out(ys, gn, w_out[i].astype(BF16), x2, g2, "ffn", ffn_w)
        else:
            x2, route = _mixer_out(ys, gn, w_out[i].astype(BF16), x2, g2, "route", (moe_w_router[j],))
            n_tiles = (2 * t) // MOE_TM + N_EXPERTS + 1
            slot, tile_expert, n_active = _moe_plan(route, MOE_TM, n_tiles)
            plan = _invert(slot, (n_tiles + 1) * MOE_TM, MOE_TM, t)
            yk = _moe_ffn(x2, norm2_g[i], plan, tile_expert, n_active, moe_w_gate[j].astype(BF16),
                          moe_w_up[j].astype(BF16), moe_w_down[j].astype(BF16), MOE_TM)
            final_done = i == depth - 1
            x2 = _combine(yk, route, x2, final_norm_g, final_done)
    if not final_done:
        x2 = _final_norm(x2, final_norm_g)
    return x2.reshape(bsz, seq, d)
```

```python
import functools
import math

import jax
import jax.numpy as jnp
import numpy as np
from jax import lax
from jax.experimental import pallas as pl
from jax.experimental.pallas import tpu as pltpu

F32 = jnp.float32
BF16 = jnp.bfloat16
EPS = 1e-6
NEG_INF = -1e30

GRID_W = 64
NA_HEAD_DIM = 64
NA_WIN_H = 8
NA_WIN_W = 16
GROUP_W = 256
SSM_GROUP_CH = 16
SSM_STATE = 64
N_EXPERTS = 8
VMEM_LIMIT = 56 * 1024 * 1024


def _cparams(sem, vmem=VMEM_LIMIT):
    return pltpu.CompilerParams(dimension_semantics=sem, vmem_limit_bytes=vmem)


def _rms(x, g):
    return x * lax.rsqrt(jnp.mean(x * x, axis=-1, keepdims=True) + EPS) * g


SUBLANES = 8
LANES = 128


def _load_token_tiles(ref, n):
    return jnp.concatenate([ref[pl.ds(s, n, stride=SUBLANES), :] for s in range(SUBLANES)], axis=1)


def _store_token_tiles(ref, val):
    for s in range(SUBLANES):
        ref[pl.ds(s, val.shape[0], stride=SUBLANES), :] = val[:, s * LANES:(s + 1) * LANES]


N_QKV_COLS = 3 * GROUP_W
SSM_COL = 5


def _in_proj_kernel(x_ref, g_ref, w_ref, qkv_ref, rest_ref):
    h = _rms(x_ref[...], g_ref[...]).astype(BF16)
    n_qkv = qkv_ref.shape[1] // GROUP_W
    n_rest = rest_ref.shape[1] // GROUP_W
    for j in range(n_qkv + n_rest):
        z = jnp.dot(h, w_ref[:, j * GROUP_W:(j + 1) * GROUP_W], preferred_element_type=F32)
        if j < n_qkv:
            qkv_ref[:, j * GROUP_W:(j + 1) * GROUP_W] = z.astype(BF16)
        else:
            k = j - n_qkv
            rest_ref[:, k * GROUP_W:(k + 1) * GROUP_W] = z


def _in_proj(x2, g, w, tm=512):
    t, d = x2.shape
    n_cols = w.shape[1]
    n_rest = n_cols - N_QKV_COLS
    return pl.pallas_call(
        _in_proj_kernel,
        grid=(t // tm,),
        in_specs=[
            pl.BlockSpec((tm, d), lambda i: (i, 0)),
            pl.BlockSpec((1, d), lambda i: (0, 0)),
            pl.BlockSpec((d, n_cols), lambda i: (0, 0)),
        ],
        out_specs=[
            pl.BlockSpec((tm, N_QKV_COLS), lambda i: (i, 0)),
            pl.BlockSpec((tm, n_rest), lambda i: (i, 0)),
        ],
        out_shape=[
            jax.ShapeDtypeStruct((t, N_QKV_COLS), BF16),
            jax.ShapeDtypeStruct((t, n_rest), F32),
        ],
        compiler_params=_cparams(("arbitrary",)),
        name="in_proj",
    )(x2, g.reshape(1, d), w)


def _na_bias_table(rpb, rows):
    n_heads = rpb.shape[0]
    kh = NA_WIN_H
    c = np.arange(GRID_W)
    qcs = np.clip(c - NA_WIN_W // 2, 0, GRID_W - NA_WIN_W)
    kc = np.arange(GRID_W)
    in_win = (kc[None, :] >= qcs[:, None]) & (kc[None, :] < qcs[:, None] + NA_WIN_W)
    dc_idx = np.clip(kc[None, :] - c[:, None] + NA_WIN_W - 1, 0, 2 * NA_WIN_W - 2)
    onehot_dc = (dc_idx[:, :, None] == np.arange(2 * NA_WIN_W - 1)).astype(np.float32)
    rep_rows = np.array([0, 1, 2, 3, 4, rows - 3, rows - 2, rows - 1])
    row_start = np.clip(rep_rows - kh // 2, 0, rows - kh)
    dr_idx = (row_start - rep_rows + NA_WIN_H - 1)[:, None] + np.arange(kh)[None, :]
    onehot_dr = (dr_idx[:, :, None] == np.arange(2 * NA_WIN_H - 1)).astype(np.float32)
    tab = jnp.einsum("via,hab,ckb->vhcik", jnp.asarray(onehot_dr), rpb.astype(F32), jnp.asarray(onehot_dc),
                     precision=lax.Precision.HIGHEST)
    tab = jnp.where(jnp.asarray(in_win)[None, None, :, None, :], tab, NEG_INF)
    return tab.reshape(len(rep_rows), n_heads * GRID_W, kh * GRID_W)


def _na_kernel(q_ref, k_ref, v_ref, bias_ref, o_ref, *, rows, n_heads):
    kh = NA_WIN_H
    lane = lax.broadcasted_iota(jnp.int32, (GRID_W, n_heads * NA_HEAD_DIM), 1)
    masks = [(lane >= h * NA_HEAD_DIM) & (lane < (h + 1) * NA_HEAD_DIM) for h in range(n_heads)]
    scale = NA_HEAD_DIM ** -0.5

    def body(r, carry):
        q0 = pl.multiple_of(r * GRID_W, GRID_W)
        q = q_ref[pl.ds(q0, GRID_W), :].astype(F32) * scale
        qs = jnp.concatenate([jnp.where(m, q, 0.0) for m in masks], axis=0).astype(BF16)
        rs = jnp.clip(r - kh // 2, 0, rows - kh)
        k0 = pl.multiple_of(rs * GRID_W, GRID_W)
        ks = k_ref[pl.ds(k0, kh * GRID_W), :]
        vs = v_ref[pl.ds(k0, kh * GRID_W), :]
        s = lax.dot_general(qs, ks, (((1,), (1,)), ((), ())), preferred_element_type=F32)
        var = jnp.where(r < kh // 2, r, jnp.where(r > rows - kh // 2, r - (rows - 8), kh // 2))
        s = s + bias_ref[var]
        m = jnp.max(s, axis=-1, keepdims=True)
        p = jnp.exp(s - m)
        den = jnp.sum(p, axis=-1, keepdims=True)
        o = jnp.dot(p.astype(BF16), vs, preferred_element_type=F32) / den
        out = jnp.where(masks[0], o[0:GRID_W], 0.0)
        for h in range(1, n_heads):
            out = out + jnp.where(masks[h], o[h * GRID_W:(h + 1) * GRID_W], 0.0)
        o_ref[pl.ds(q0, GRID_W), :] = out
        return carry

    lax.fori_loop(0, rows, body, 0, unroll=8)


def _na_attention(qkv, bias, bsz, seq):
    rows = seq // GRID_W
    n_heads = GROUP_W // NA_HEAD_DIM
    t = qkv.shape[0]
    blk = lambda j: pl.BlockSpec((seq, GROUP_W), lambda b, j=j: (b, j))
    return pl.pallas_call(
        functools.partial(_na_kernel, rows=rows, n_heads=n_heads),
        grid=(bsz,),
        in_specs=[blk(0), blk(1), blk(2),
                  pl.BlockSpec(bias.shape, lambda b: (0, 0, 0))],
        out_specs=pl.BlockSpec((seq, GROUP_W), lambda b: (b, 0)),
        out_shape=jax.ShapeDtypeStruct((t, GROUP_W), F32),
        compiler_params=_cparams(("arbitrary",)),
        name="na_attention",
    )(qkv, qkv, qkv, bias)


CONV_PAD = 16
CONV_CHUNK = 256


def _conv_kernel(b_ref, c_ref, x_ref, a_ref, g_ref, scw_ref, cfw_ref, cfb_ref, lng_ref, lnb_ref,
                 ysc_ref, ycf_ref, psc, pcf):
    seq = b_ref.shape[0]
    n_chunks = seq // CONV_CHUNK
    zeros = jnp.zeros((CONV_PAD, GROUP_W), F32)
    psc[0:CONV_PAD, :] = zeros
    pcf[0:CONV_PAD, :] = zeros
    psc[CONV_PAD + seq:2 * CONV_PAD + seq, :] = zeros
    pcf[CONV_PAD + seq:2 * CONV_PAD + seq, :] = zeros

    def fill(i, carry):
        r0 = pl.multiple_of(i * CONV_CHUNK, CONV_CHUNK)
        rows = pl.ds(r0, CONV_CHUNK)
        dst = pl.ds(r0 + CONV_PAD, CONV_CHUNK)
        psc[dst, :] = c_ref[rows, :] * x_ref[rows, :]
        pcf[dst, :] = a_ref[rows, :] * jax.nn.sigmoid(g_ref[rows, :])
        return carry

    lax.fori_loop(0, n_chunks, fill, 0)

    def taps(win, w_ref):
        n_taps = w_ref.shape[0]
        offs = [CONV_PAD + k - n_taps // 2 for k in range(n_taps)]
        n_win = win.shape[0]
        acc = None
        for s in range(SUBLANES):
            ks = [k for k in range(n_taps) if offs[k] % SUBLANES == s]
            if not ks:
                continue
            shifted = pltpu.roll(win, n_win - s, axis=0) if s else win
            for k in ks:
                q = offs[k] - s
                term = shifted[q:q + CONV_CHUNK, :] * w_ref[k:k + 1, :]
                acc = term if acc is None else acc + term
        return acc

    def conv(i, carry):
        r0 = pl.multiple_of(i * CONV_CHUNK, CONV_CHUNK)
        rows = pl.ds(r0, CONV_CHUNK)
        window = pl.ds(r0, CONV_CHUNK + 2 * CONV_PAD)
        ysc_ref[rows, :] = b_ref[rows, :] * taps(psc[window, :], scw_ref)
        cf = taps(pcf[window, :], cfw_ref) + cfb_ref[...]
        mu = jnp.mean(cf, axis=-1, keepdims=True)
        xc = cf - mu
        var = jnp.mean(xc * xc, axis=-1, keepdims=True)
        ln = xc * lax.rsqrt(var + EPS) * lng_ref[...] + lnb_ref[...]
        ycf_ref[rows, :] = ln * jax.nn.sigmoid(ln)
        return carry

    lax.fori_loop(0, n_chunks, conv, 0)


def _conv_mixers(rest, sc_w, cf_w, cf_b, ln_g, ln_b, bsz, seq):
    t = rest.shape[0]
    blk = lambda j: pl.BlockSpec((seq, GROUP_W), lambda b, j=j: (b, j))
    full = lambda a: pl.BlockSpec(a.shape, lambda b: (0,) * a.ndim)
    cf_b, ln_g, ln_b = (a.reshape(1, GROUP_W) for a in (cf_b, ln_g, ln_b))
    out_spec = pl.BlockSpec((seq, GROUP_W), lambda b: (b, 0))
    return pl.pallas_call(
        _conv_kernel,
        grid=(bsz,),
        in_specs=[blk(0), blk(1), blk(2), blk(3), blk(4),
                  full(sc_w), full(cf_w), full(cf_b), full(ln_g), full(ln_b)],
        out_specs=[out_spec, out_spec],
        out_shape=[jax.ShapeDtypeStruct((t, GROUP_W), F32)] * 2,
        scratch_shapes=[pltpu.VMEM((seq + 2 * CONV_PAD, GROUP_W), F32)] * 2,
        compiler_params=_cparams(("arbitrary",)),
        name="conv_mixers",
    )(rest, rest, rest, rest, rest, sc_w, cf_w, cf_b, ln_g, ln_b)


S5_TL = 128
S5_LANE_SPLIT = 1
S5_ROW_BLOCK = 256


def _s5_discretise(a_re, a_im, log_dt, b_re, b_im, c_re, c_im, bsz):
    f32 = F32
    a_re, a_im, log_dt = a_re.astype(f32), a_im.astype(f32), log_dt.astype(f32)
    n_dir, n_grp, n_state = a_re.shape
    n_ch = b_re.shape[-1]
    dt = jnp.exp(log_dt)[..., None]
    mag = jnp.exp(a_re * dt)
    abr = mag * jnp.cos(a_im * dt)
    abi = mag * jnp.sin(a_im * dt)
    den = a_re * a_re + a_im * a_im
    fr = ((abr - 1.0) * a_re + abi * a_im) / den
    fi = (abi * a_re - (abr - 1.0) * a_im) / den
    bbr = fr[..., None] * b_re - fi[..., None] * b_im
    bbi = fr[..., None] * b_im + fi[..., None] * b_re
    eye = jnp.eye(n_grp, dtype=f32)
    blk_b = lambda m: jnp.einsum("dgps,gh->dgshp", m, eye).reshape(n_dir, n_grp * n_ch, n_grp * n_state)
    blk_c = lambda m: jnp.einsum("dgsp,gh->dgphs", m, eye).reshape(n_dir, n_grp * n_state, n_grp * n_ch)
    bmat = jnp.concatenate([blk_b(bbr), blk_b(bbi)], axis=-1)
    cmat = jnp.concatenate([blk_c(c_re.astype(f32)), blk_c(-c_im.astype(f32))], axis=1)
    amat = jnp.stack([abr.reshape(n_dir, -1), abi.reshape(n_dir, -1)], axis=1)
    amat = jnp.broadcast_to(amat[:, :, None, :], (n_dir, 2, bsz, n_grp * n_state))
    return bmat.astype(BF16), cmat.astype(BF16), amat


def _s5_kernel(*refs, bsz, backward):
    if backward:
        (u_ref, b_ref, c_ref, a_ref, yf_ref, d_ref, w_ref, bg_ref, o_ref, xs, st, tb) = refs
    else:
        (u_ref, b_ref, c_ref, a_ref, o_ref, xs, st, tb) = refs
    n_state = a_ref.shape[-1]

    @pl.when(pl.program_id(0) == 0)
    def _():
        st[...] = jnp.zeros_like(st)

    n_half = GROUP_W // LANES
    for b in range(bsz):
        for h in range(n_half):
            tb[h, pl.ds(b, S5_TL, stride=bsz), :] = u_ref[b, :, h * LANES:(h + 1) * LANES]
    for r0 in range(0, S5_TL * bsz, S5_ROW_BLOCK):
        rows = slice(r0, r0 + S5_ROW_BLOCK)
        u_tb = jnp.concatenate([tb[h, rows, :] for h in range(n_half)], axis=1)
        xs[rows, :] = jnp.dot(u_tb.astype(BF16), b_ref[...], preferred_element_type=F32)

    cw = n_state // S5_LANE_SPLIT
    for c in range(S5_LANE_SPLIT):
        re = slice(c * cw, (c + 1) * cw)
        im = slice(n_state + c * cw, n_state + (c + 1) * cw)
        a_r = a_ref[0, :, re]
        a_i = a_ref[1, :, re]

        def step(t, carry, re=re, im=im, a_r=a_r, a_i=a_i):
            s_r, s_i = carry
            tt = S5_TL - 1 - t if backward else t
            rows = pl.ds(pl.multiple_of(tt * bsz, bsz), bsz)
            n_r = a_r * s_r - a_i * s_i + xs[rows, re]
            n_i = a_r * s_i + a_i * s_r + xs[rows, im]
            xs[rows, re] = n_r
            xs[rows, im] = n_i
            return n_r, n_i

        s_r, s_i = lax.fori_loop(0, S5_TL, step, (st[:, re], st[:, im]), unroll=4)
        st[:, re] = s_r
        st[:, im] = s_i

    n_rows = S5_TL * bsz
    y_tb = jnp.concatenate(
        [jnp.dot(xs[r0:r0 + S5_ROW_BLOCK, :].astype(BF16), c_ref[...], preferred_element_type=F32)
         for r0 in range(0, n_rows, S5_ROW_BLOCK)], axis=0)
    if not backward:
        o_ref[...] = y_tb
        return
    y_sum = y_tb + yf_ref[...]
    for h in range(n_half):
        tb[h] = y_sum[:, h * LANES:(h + 1) * LANES]
    y_bt = jnp.concatenate(
        [jnp.concatenate([tb[h, pl.ds(b, S5_TL, stride=bsz), :] for h in range(n_half)], axis=1)
         for b in range(bsz)], axis=0)
    y = jax.nn.gelu(d_ref[...] * u_ref[...].reshape(S5_TL * bsz, GROUP_W) + y_bt)
    gate = jnp.dot(y.astype(BF16), w_ref[...], preferred_element_type=F32) + bg_ref[...]
    o_ref[...] = (y * jax.nn.sigmoid(gate)).reshape(bsz, S5_TL, GROUP_W)


def _s5_mixer(rest, ssm_col, bmat, cmat, amat, d_skip, w_glu, b_glu, bsz, seq):
    n_chunks = seq // S5_TL
    n_rows = S5_TL * bsz
    n_st2 = bmat.shape[-1]
    rest3 = rest.reshape(bsz, seq, rest.shape[1])
    full = lambda a: pl.BlockSpec(a.shape, lambda i: (0,) * a.ndim)
    scratch = [pltpu.VMEM((n_rows, n_st2), F32),
               pltpu.VMEM((bsz, n_st2), F32),
               pltpu.VMEM((GROUP_W // LANES, n_rows, LANES), F32)]

    def direction(d, chunk_of):
        return [
            pl.BlockSpec((bsz, S5_TL, GROUP_W), lambda i: (0, chunk_of(i), ssm_col)),
            pl.BlockSpec((None, GROUP_W, n_st2), lambda i: (d, 0, 0)),
            pl.BlockSpec((None, n_st2, GROUP_W), lambda i: (d, 0, 0)),
            pl.BlockSpec((None, 2, bsz, n_st2 // 2), lambda i: (d, 0, 0, 0)),
        ]

    y_fwd = pl.pallas_call(
        functools.partial(_s5_kernel, bsz=bsz, backward=False),
        grid=(n_chunks,),
        in_specs=direction(0, lambda i: i),
        out_specs=pl.BlockSpec((n_rows, GROUP_W), lambda i: (i, 0)),
        out_shape=jax.ShapeDtypeStruct((seq * bsz, GROUP_W), F32),
        scratch_shapes=scratch,
        compiler_params=_cparams(("arbitrary",)),
        name="s5_forward",
    )(rest3, bmat, cmat, amat)

    rev = lambda i: n_chunks - 1 - i
    d_skip, b_glu = d_skip.reshape(1, GROUP_W), b_glu.reshape(1, GROUP_W)
    out = pl.pallas_call(
        functools.partial(_s5_kernel, bsz=bsz, backward=True),
        grid=(n_chunks,),
        in_specs=direction(1, rev) + [
            pl.BlockSpec((n_rows, GROUP_W), lambda i: (rev(i), 0)),
            full(d_skip), full(w_glu), full(b_glu),
        ],
        out_specs=pl.BlockSpec((bsz, S5_TL, GROUP_W), lambda i: (0, rev(i), 0)),
        out_shape=jax.ShapeDtypeStruct((bsz, seq, GROUP_W), F32),
        scratch_shapes=scratch,
        compiler_params=_cparams(("arbitrary",)),
        name="s5_backward",
    )(rest3, bmat, cmat, amat, y_fwd, d_skip, w_glu, b_glu)
    return out.reshape(bsz * seq, GROUP_W)


FFN_CHUNK = 256
ROUTE_COLS = 8
ROUTER_LANES = 128


def _route_records(h, wr_ref):
    h_hi = h.astype(BF16)
    h_lo = (h - h_hi.astype(F32)).astype(BF16)
    logits = jnp.dot(jnp.concatenate([h_hi, h_lo, h_hi], axis=1), wr_ref[...], preferred_element_type=F32)
    lane = lax.broadcasted_iota(jnp.int32, logits.shape, 1)
    minus_inf = -jnp.inf
    l1 = jnp.where(lane < N_EXPERTS, logits, minus_inf)
    m1 = jnp.max(l1, axis=-1, keepdims=True)
    i1 = jnp.min(jnp.where(l1 == m1, lane, ROUTER_LANES), axis=-1, keepdims=True)
    l2 = jnp.where(lane == i1, minus_inf, l1)
    m2 = jnp.max(l2, axis=-1, keepdims=True)
    i2 = jnp.min(jnp.where(l2 == m2, lane, ROUTER_LANES), axis=-1, keepdims=True)
    e2 = jnp.exp(m2 - m1)
    den = 1.0 + e2
    rec = jnp.where(lane == 0, i1.astype(F32),
                    jnp.where(lane == 1, i2.astype(F32),
                              jnp.where(lane == 2, 1.0 / den,
                                        jnp.where(lane == 3, e2 / den, 0.0))))
    return rec[:, :ROUTE_COLS]


def _mixer_out_kernel(*refs, tail):
    y_refs, (gn_ref, w_ref, x_ref, g2_ref) = refs[:4], refs[4:8]
    acc = x_ref[...]
    for j, y_ref in enumerate(y_refs):
        yn = _rms(y_ref[...], gn_ref[j:j + 1, :]).astype(BF16)
        acc = acc + jnp.dot(yn, w_ref[j * GROUP_W:(j + 1) * GROUP_W, :], preferred_element_type=F32)
    h = _rms(acc, g2_ref[...])
    if tail == "route":
        wr_ref, o_ref, route_ref = refs[8:]
        _store_token_tiles(o_ref, acc)
        route_ref[...] = _route_records(h, wr_ref)
        return
    wg_hbm, wu_hbm, wd_hbm, o_ref, wg_ref, wu_ref, wd_ref, sem = refs[8:]

    @pl.when(pl.program_id(0) == 0)
    def _():
        copies = [pltpu.make_async_copy(src, dst, sem.at[k])
                  for k, (src, dst) in enumerate(((wg_hbm, wg_ref), (wu_hbm, wu_ref), (wd_hbm, wd_ref)))]
        for c in copies:
            c.start()
        for c in copies:
            c.wait()

    h = h.astype(BF16)
    acts = []
    for c in range(wg_ref.shape[0]):
        gate = jnp.dot(h, wg_ref[c], preferred_element_type=F32)
        up = jnp.dot(h, wu_ref[c], preferred_element_type=F32)
        acts.append((gate * jax.nn.sigmoid(gate) * up).astype(BF16))
    o_ref[...] = acc + jnp.dot(jnp.concatenate(acts, axis=1), wd_ref[...], preferred_element_type=F32)


def _mixer_out(ys, gn, w_out, x2, g2, tail, tail_weights, tm=512):
    t, d = x2.shape
    row = lambda n: pl.BlockSpec((tm, n), lambda i: (i, 0))
    whole = lambda a: pl.BlockSpec(a.shape, lambda i: (0, 0))
    if tail == "route":
        wr = jnp.zeros((d, ROUTER_LANES), F32).at[:, :N_EXPERTS].set(tail_weights[0].astype(F32))
        wr_hi = wr.astype(BF16)
        wr_lo = (wr - wr_hi.astype(F32)).astype(BF16)
        tail_weights = (jnp.concatenate([wr_hi, wr_hi, wr_lo], axis=0),)
        tail_specs = [whole(w) for w in tail_weights]
        out_specs = [pl.BlockSpec((tm * SUBLANES, LANES), lambda i: (i, 0)), row(ROUTE_COLS)]
        out_shape = [jax.ShapeDtypeStruct((t * SUBLANES, LANES), F32), jax.ShapeDtypeStruct((t, ROUTE_COLS), F32)]
        scratch = []
    else:
        w_gate, w_up, w_down = tail_weights
        n_chunks = w_gate.shape[1] // FFN_CHUNK
        by_cols = lambda w: jnp.transpose(w.reshape(d, n_chunks, FFN_CHUNK), (1, 0, 2))
        tail_weights = (by_cols(w_gate), by_cols(w_up), w_down)
        tail_specs = [pl.BlockSpec(memory_space=pl.ANY)] * 3
        out_specs, out_shape = row(d), jax.ShapeDtypeStruct((t, d), F32)
        scratch = [pltpu.VMEM(w.shape, w.dtype) for w in tail_weights] + [pltpu.SemaphoreType.DMA((3,))]
    return pl.pallas_call(
        functools.partial(_mixer_out_kernel, tail=tail),
        grid=(t // tm,),
        in_specs=[row(GROUP_W)] * 4 + [whole(gn), whole(w_out), row(d), whole(g2)] + tail_specs,
        out_specs=out_specs,
        out_shape=out_shape,
        scratch_shapes=scratch,
        compiler_params=_cparams(("arbitrary",)),
        name="mixer_out_" + tail,
    )(*ys, gn, w_out, x2, g2, *tail_weights)


def _moe_plan(route, tm, n_tiles):
    ids = route[:, :2]
    experts = jnp.where((ids >= 0) & (ids <= N_EXPERTS - 1), ids, 0).astype(jnp.int32).reshape(-1)
    onehot = (experts[:, None] == jnp.arange(N_EXPERTS, dtype=jnp.int32)[None, :]).astype(jnp.int32)
    csum = jnp.cumsum(onehot, axis=0)
    rank = jnp.sum((csum - onehot) * onehot, axis=1)
    count = csum[-1]
    padded = ((count + tm - 1) // tm) * tm
    end = jnp.cumsum(padded)
    start = end - padded
    slot = jnp.sum(onehot * start[None, :], axis=1) + rank
    n_active = end[-1] // tm
    tile_start = jnp.arange(n_tiles, dtype=jnp.int32) * tm
    tile_expert = jnp.sum((tile_start[:, None] >= end[None, :]).astype(jnp.int32), axis=1)
    tile_expert = jnp.minimum(tile_expert, N_EXPERTS - 1)
    last = tile_expert[jnp.maximum(n_active - 1, 0)]
    tile_expert = jnp.where(jnp.arange(n_tiles) < n_active, tile_expert, last)
    return slot.astype(jnp.int32), tile_expert.astype(jnp.int32), n_active.astype(jnp.int32).reshape(1)


def _token_copy(src, src_tok, dst, dst_tok, sem):
    rows = lambda tok: pl.ds(pl.multiple_of(tok * SUBLANES, SUBLANES), SUBLANES)
    return pltpu.make_async_copy(src.at[rows(src_tok)], dst.at[rows(dst_tok)], sem)


def _invert_kernel(slot_ref, word_ref, pad_hbm, plan_ref, sem):
    @pl.when(pl.program_id(0) == 0)
    def _():
        init = pltpu.make_async_copy(pad_hbm, plan_ref, sem)
        init.start()
        init.wait()

    def place(a, carry):
        plan_ref[slot_ref[0, 0, a]] = word_ref[0, 0, a]
        return carry

    lax.fori_loop(0, slot_ref.shape[-1], place, 0, unroll=8)


def _plan_bits(n_tok, tm):
    src_bits = (n_tok - 1).bit_length()
    assert (2 * n_tok + 2 * tm) << src_bits < 2 ** 31 and tm & (tm - 1) == 0
    return src_bits


def _invert(slot, n_slots, tm, n_tok, chunk=4096):
    src_bits = _plan_bits(n_tok, tm)
    a = jnp.arange(slot.shape[0], dtype=jnp.int32)
    tok = a >> 1
    word = tok + (((a & 1) * n_tok + tok) << src_bits)
    s = jnp.arange(n_slots, dtype=jnp.int32)
    pad_plan = (2 * n_tok + (s & (2 * tm - 1))) << src_bits
    n_chunks = slot.shape[0] // chunk
    chunked = pl.BlockSpec((1, 1, chunk), lambda c: (c, 0, 0), memory_space=pltpu.SMEM)
    return pl.pallas_call(
        _invert_kernel,
        grid=(n_chunks,),
        in_specs=[chunked, chunked, pl.BlockSpec(memory_space=pl.ANY)],
        out_specs=pl.BlockSpec((n_slots,), lambda c: (0,), memory_space=pltpu.SMEM),
        out_shape=jax.ShapeDtypeStruct((n_slots,), jnp.int32),
        scratch_shapes=[pltpu.SemaphoreType.DMA(())],
        compiler_params=pltpu.CompilerParams(dimension_semantics=("arbitrary",)),
        name="moe_invert",
    )(slot.reshape(n_chunks, 1, chunk), word.reshape(n_chunks, 1, chunk), pad_plan)


def _tokens_done(ref, n_tok, sem):
    rows = pl.ds(0, n_tok * SUBLANES)
    pltpu.make_async_copy(ref.at[rows], ref.at[rows], sem).wait()


def _moe_ffn_kernel(te_ref, nact_ref, plan_ref, x_hbm, g_ref, wg_ref, wu_ref, wd_ref, yk_hbm,
                    xbuf, xb16, ybuf, acc_ref, sem_in, sem_out, *, tm, n_tok, n_f, pad_tile, src_bits):
    del te_ref
    i = pl.program_id(0)
    f = pl.program_id(1)
    par = i % 2
    nact = nact_ref[0]

    def fetch(tile, p, j):
        src = plan_ref[tile * tm + j] & ((1 << src_bits) - 1)
        _token_copy(x_hbm, src, xbuf.at[p], j, sem_in.at[p]).start(priority=1)

    def send(tile, p, j):
        dst = plan_ref[tile * tm + j] >> src_bits
        _token_copy(ybuf.at[p], j, yk_hbm, dst, sem_out.at[p]).start()

    @pl.when((i == 0) & (f == 0))
    def _():
        def first(j, carry):
            fetch(0, 0, j)
            return carry
        lax.fori_loop(0, tm, first, 0)
        ybuf[...] = jnp.zeros_like(ybuf)
        for p in range(2):
            spare_rows = pl.ds((2 * n_tok + p * tm) * SUBLANES, tm * SUBLANES)
            spare = pltpu.make_async_copy(ybuf.at[p], yk_hbm.at[spare_rows], sem_out.at[p])
            spare.start()
            spare.wait()
        _tokens_done(xbuf.at[0], tm, sem_in.at[0])

    @pl.when((i >= 1) & (i <= nact) & (f == 0))
    def _():
        _tokens_done(xbuf.at[par], tm, sem_in.at[par])

    @pl.when(i < nact)
    def _():
        @pl.when(f == 0)
        def _():
            xb16[...] = _rms(_load_token_tiles(xbuf.at[par], tm), g_ref[...]).astype(BF16)
            acc_ref[...] = jnp.zeros_like(acc_ref)

            for j in range(tm):
                fetch(i + 1, 1 - par, j)

        @pl.when(f == n_f - 1)
        def _():
            prev = jnp.where(i > 0, i - 1, pad_tile)

            for j in range(tm):
                send(prev, 1 - par, j)

        x = xb16[...]
        gate = jnp.dot(x, wg_ref[...], preferred_element_type=F32)
        up = jnp.dot(x, wu_ref[...], preferred_element_type=F32)
        act = (gate * jax.nn.sigmoid(gate) * up).astype(BF16)
        acc_ref[...] += jnp.dot(act, wd_ref[...], preferred_element_type=F32)

        @pl.when(f == n_f - 1)
        def _():
            @pl.when(i >= 1)
            def _():
                _tokens_done(ybuf.at[par], tm, sem_out.at[par])

            _store_token_tiles(ybuf.at[par], acc_ref[...])

    @pl.when((i == nact) & (f == 0))
    def _():
        def last(j, carry):
            send(i - 1, 1 - par, j)
            return carry
        lax.fori_loop(0, tm, last, 0)
        _tokens_done(ybuf.at[0], tm, sem_out.at[0])
        _tokens_done(ybuf.at[1], tm, sem_out.at[1])


def _moe_ffn(xt, g, plan, tile_expert, n_active, wg, wu, wd, tm):
    n_tok = xt.shape[0] // SUBLANES
    d = SUBLANES * LANES
    n_tiles = tile_expert.shape[0]
    pad_tile = n_tiles - 1 if (n_tiles - 1) % 2 else n_tiles
    assert plan.shape[0] >= (pad_tile + 1) * tm
    dff = wg.shape[-1]
    n_f = 2
    tf = dff // n_f
    fidx = lambda i, f, na: jnp.where(i < na[0], f, n_f - 1)
    grid_spec = pltpu.PrefetchScalarGridSpec(
        num_scalar_prefetch=3,
        grid=(n_tiles, n_f),
        in_specs=[
            pl.BlockSpec(memory_space=pl.ANY),
            pl.BlockSpec((1, d), lambda i, f, te, na, plan: (0, 0)),
            pl.BlockSpec((None, d, tf), lambda i, f, te, na, plan: (te[i], 0, fidx(i, f, na))),
            pl.BlockSpec((None, d, tf), lambda i, f, te, na, plan: (te[i], 0, fidx(i, f, na))),
            pl.BlockSpec((None, tf, d), lambda i, f, te, na, plan: (te[i], fidx(i, f, na), 0)),
        ],
        out_specs=pl.BlockSpec(memory_space=pl.ANY),
        scratch_shapes=[
            pltpu.VMEM((2, tm * SUBLANES, LANES), F32),
            pltpu.VMEM((tm, d), BF16),
            pltpu.VMEM((2, tm * SUBLANES, LANES), F32),
            pltpu.VMEM((tm, d), F32),
            pltpu.SemaphoreType.DMA((2,)),
            pltpu.SemaphoreType.DMA((2,)),
        ],
    )
    return pl.pallas_call(
        functools.partial(_moe_ffn_kernel, tm=tm, n_tok=n_tok, n_f=n_f, pad_tile=pad_tile,
                          src_bits=_plan_bits(n_tok, tm)),
        grid_spec=grid_spec,
        out_shape=jax.ShapeDtypeStruct(((2 * n_tok + 2 * tm) * SUBLANES, LANES), F32),
        compiler_params=_cparams(("arbitrary", "arbitrary")),
        name="moe_ffn",
    )(tile_expert, n_active, plan, xt, g.reshape(1, d), wg, wu, wd)


def _combine_kernel(route_ref, x_ref, y0_ref, y1_ref, fg_ref, o_ref, *, final_norm):
    r = route_ref[...]
    tm = r.shape[0]
    y = (_load_token_tiles(x_ref, tm) + r[:, 2:3] * _load_token_tiles(y0_ref, tm)
         + r[:, 3:4] * _load_token_tiles(y1_ref, tm))
    o_ref[...] = _rms(y, fg_ref[...]) if final_norm else y


def _combine(yk, route, xt, final_g, final_norm, tm=512):
    t = xt.shape[0] // SUBLANES
    d = SUBLANES * LANES
    n_blk = t // tm
    tiles = lambda f: pl.BlockSpec((tm * SUBLANES, LANES), f)
    return pl.pallas_call(
        functools.partial(_combine_kernel, final_norm=final_norm),
        grid=(n_blk,),
        in_specs=[
            pl.BlockSpec((tm, ROUTE_COLS), lambda i: (i, 0)),
            tiles(lambda i: (i, 0)),
            tiles(lambda i: (i, 0)),
            tiles(lambda i: (n_blk + i, 0)),
            pl.BlockSpec((1, d), lambda i: (0, 0)),
        ],
        out_specs=pl.BlockSpec((tm, d), lambda i: (i, 0)),
        out_shape=jax.ShapeDtypeStruct((t, d), F32),
        compiler_params=_cparams(("arbitrary",)),
        name="moe_combine",
    )(route, xt, yk, yk, final_g.reshape(1, d))


def _final_norm_kernel(x_ref, g_ref, o_ref):
    o_ref[...] = _rms(x_ref[...], g_ref[...])


def _final_norm(x2, g, tm=512):
    t, d = x2.shape
    return pl.pallas_call(
        _final_norm_kernel,
        grid=(t // tm,),
        in_specs=[pl.BlockSpec((tm, d), lambda i: (i, 0)), pl.BlockSpec((1, d), lambda i: (0, 0))],
        out_specs=pl.BlockSpec((tm, d), lambda i: (i, 0)),
        out_shape=jax.ShapeDtypeStruct((t, d), F32),
        compiler_params=_cparams(("arbitrary",)),
        name="final_norm",
    )(x2, g.reshape(1, d))


MOE_TM = 512


def kernel(x, norm1_g, w_in, na_rpb, sc_conv_w, cf_conv_w, cf_conv_b, cf_ln_g, cf_ln_b, ssm_a_re, ssm_a_im,
           ssm_log_dt, ssm_b_re, ssm_b_im, ssm_c_re, ssm_c_im, ssm_d, ssm_w_glu, ssm_b_glu, grp_norm_g, w_out,
           norm2_g, ffn_w_gate, ffn_w_up, ffn_w_down, moe_w_router, moe_w_gate, moe_w_up, moe_w_down,
           final_norm_g):
    bsz, seq, d = x.shape
    depth = w_in.shape[0]
    t = bsz * seq
    rows = seq // GRID_W
    x2 = x.reshape(t, d).astype(F32)
    final_done = False
    na_bias = jax.vmap(lambda rpb: _na_bias_table(rpb, rows))(na_rpb)
    s5_mats = jax.vmap(lambda *p: _s5_discretise(*p, bsz))(ssm_a_re, ssm_a_im, ssm_log_dt, ssm_b_re, ssm_b_im,
                                                           ssm_c_re, ssm_c_im)
    for i in range(depth):
        qkv, rest = _in_proj(x2, norm1_g[i], w_in[i].astype(BF16))
        y_na = _na_attention(qkv, na_bias[i], bsz, seq)
        y_sc, y_cf = _conv_mixers(rest, sc_conv_w[i], cf_conv_w[i], cf_conv_b[i], cf_ln_g[i], cf_ln_b[i],
                                  bsz, seq)
        bmat, cmat, amat = (m[i] for m in s5_mats)
        y_ssm = _s5_mixer(rest, SSM_COL, bmat, cmat, amat, ssm_d[i], ssm_w_glu[i].astype(BF16), ssm_b_glu[i],
                          bsz, seq)
        ys = (y_na, y_sc, y_cf, y_ssm)
        gn, g2 = grp_norm_g[i].reshape(4, GROUP_W), norm2_g[i].reshape(1, d)
        j = i // 2
        if i % 2 == 0:
            ffn_w = (ffn_w_gate[j].astype(BF16), ffn_w_up[j].astype(BF16), ffn_w_down[j].astype(BF16))
            x2 = _mixer_out(ys, gn, w_out[i].astype(BF16), x2, g2, "ffn", ffn_w)
        else:
            x2, route = _mixer_out(ys, gn, w_out[i].astype(BF16), x2, g2, "route", (moe_w_router[j],))
            n_tiles = (2 * t) // MOE_TM + N_EXPERTS + 1
            slot, tile_expert, n_active = _moe_plan(route, MOE_TM, n_tiles)
            plan = _invert(slot, (n_tiles + 1) * MOE_TM, MOE_TM, t)
            yk = _moe_ffn(x2, norm2_g[i], plan, tile_expert, n_active, moe_w_gate[j].astype(BF16),
                          moe_w_up[j].astype(BF16), moe_w_down[j].astype(BF16), MOE_TM)
            final_done = i == depth - 1
            x2 = _combine(yk, route, x2, final_norm_g, final_done)
    if not final_done:
        x2 = _final_norm(x2, final_norm_g)
    return x2.reshape(bsz, seq, d)
```

```python
import functools
import math

import jax
import jax.numpy as jnp
import numpy as np
from jax import lax
from jax.experimental import pallas as pl
from jax.experimental.pallas import tpu as pltpu

F32 = jnp.float32
BF16 = jnp.bfloat16
EPS = 1e-6
NEG_INF = -1e30

GRID_W = 64
NA_HEAD_DIM = 64
NA_WIN_H = 8
NA_WIN_W = 16
GROUP_W = 256
SSM_GROUP_CH = 16
SSM_STATE = 64
N_EXPERTS = 8
VMEM_LIMIT = 56 * 1024 * 1024


def _cparams(sem, vmem=VMEM_LIMIT):
    return pltpu.CompilerParams(dimension_semantics=sem, vmem_limit_bytes=vmem)


def _rms(x, g):
    return x * lax.rsqrt(jnp.mean(x * x, axis=-1, keepdims=True) + EPS) * g


SUBLANES = 8
LANES = 128


def _load_token_tiles(ref, n):
    return jnp.concatenate([ref[pl.ds(s, n, stride=SUBLANES), :] for s in range(SUBLANES)], axis=1)


def _store_token_tiles(ref, val):
    for s in range(SUBLANES):
        ref[pl.ds(s, val.shape[0], stride=SUBLANES), :] = val[:, s * LANES:(s + 1) * LANES]


N_QKV_COLS = 3 * GROUP_W
SSM_COL = 5


def _in_proj_kernel(x_ref, g_ref, w_ref, qkv_ref, rest_ref):
    h = _rms(x_ref[...], g_ref[...]).astype(BF16)
    n_qkv = qkv_ref.shape[1] // GROUP_W
    n_rest = rest_ref.shape[1] // GROUP_W
    for j in range(n_qkv + n_rest):
        z = jnp.dot(h, w_ref[:, j * GROUP_W:(j + 1) * GROUP_W], preferred_element_type=F32)
        if j < n_qkv:
            qkv_ref[:, j * GROUP_W:(j + 1) * GROUP_W] = z.astype(BF16)
        else:
            k = j - n_qkv
            rest_ref[:, k * GROUP_W:(k + 1) * GROUP_W] = z


def _in_proj(x2, g, w, tm=512):
    t, d = x2.shape
    n_cols = w.shape[1]
    n_rest = n_cols - N_QKV_COLS
    return pl.pallas_call(
        _in_proj_kernel,
        grid=(t // tm,),
        in_specs=[
            pl.BlockSpec((tm, d), lambda i: (i, 0)),
            pl.BlockSpec((1, d), lambda i: (0, 0)),
            pl.BlockSpec((d, n_cols), lambda i: (0, 0)),
        ],
        out_specs=[
            pl.BlockSpec((tm, N_QKV_COLS), lambda i: (i, 0)),
            pl.BlockSpec((tm, n_rest), lambda i: (i, 0)),
        ],
        out_shape=[
            jax.ShapeDtypeStruct((t, N_QKV_COLS), BF16),
            jax.ShapeDtypeStruct((t, n_rest), F32),
        ],
        compiler_params=_cparams(("arbitrary",)),
        name="in_proj",
    )(x2, g.reshape(1, d), w)


def _na_bias_table(rpb, rows):
    n_heads = rpb.shape[0]
    kh = NA_WIN_H
    c = np.arange(GRID_W)
    qcs = np.clip(c - NA_WIN_W // 2, 0, GRID_W - NA_WIN_W)
    kc = np.arange(GRID_W)
    in_win = (kc[None, :] >= qcs[:, None]) & (kc[None, :] < qcs[:, None] + NA_WIN_W)
    dc_idx = np.clip(kc[None, :] - c[:, None] + NA_WIN_W - 1, 0, 2 * NA_WIN_W - 2)
    onehot_dc = (dc_idx[:, :, None] == np.arange(2 * NA_WIN_W - 1)).astype(np.float32)
    rep_rows = np.array([0, 1, 2, 3, 4, rows - 3, rows - 2, rows - 1])
    row_start = np.clip(rep_rows - kh // 2, 0, rows - kh)
    dr_idx = (row_start - rep_rows + NA_WIN_H - 1)[:, None] + np.arange(kh)[None, :]
    onehot_dr = (dr_idx[:, :, None] == np.arange(2 * NA_WIN_H - 1)).astype(np.float32)
    tab = jnp.einsum("via,hab,ckb->vhcik", jnp.asarray(onehot_dr), rpb.astype(F32), jnp.asarray(onehot_dc),
                     precision=lax.Precision.HIGHEST)
    tab = jnp.where(jnp.asarray(in_win)[None, None, :, None, :], tab, NEG_INF)
    return tab.reshape(len(rep_rows), n_heads * GRID_W, kh * GRID_W)


def _na_kernel(q_ref, k_ref, v_ref, bias_ref, o_ref, *, rows, n_heads):
    kh = NA_WIN_H
    lane = lax.broadcasted_iota(jnp.int32, (GRID_W, n_heads * NA_HEAD_DIM), 1)
    masks = [(lane >= h * NA_HEAD_DIM) & (lane < (h + 1) * NA_HEAD_DIM) for h in range(n_heads)]
    scale = NA_HEAD_DIM ** -0.5

    def body(r, carry):
        q0 = pl.multiple_of(r * GRID_W, GRID_W)
        q = q_ref[pl.ds(q0, GRID_W), :].astype(F32) * scale
        qs = jnp.concatenate([jnp.where(m, q, 0.0) for m in masks], axis=0).astype(BF16)
        rs = jnp.clip(r - kh // 2, 0, rows - kh)
        k0 = pl.multiple_of(rs * GRID_W, GRID_W)
        ks = k_ref[pl.ds(k0, kh * GRID_W), :]
        vs = v_ref[pl.ds(k0, kh * GRID_W), :]
        s = lax.dot_general(qs, ks, (((1,), (1,)), ((), ())), preferred_element_type=F32)
        var = jnp.where(r < kh // 2, r, jnp.where(r > rows - kh // 2, r - (rows - 8), kh // 2))
        s = s + bias_ref[var]
        m = jnp.max(s, axis=-1, keepdims=True)
        p = jnp.exp(s - m)
        den = jnp.sum(p, axis=-1, keepdims=True)
        o = jnp.dot(p.astype(BF16), vs, preferred_element_type=F32) / den
        out = jnp.where(masks[0], o[0:GRID_W], 0.0)
        for h in range(1, n_heads):
            out = out + jnp.where(masks[h], o[h * GRID_W:(h + 1) * GRID_W], 0.0)
        o_ref[pl.ds(q0, GRID_W), :] = out
        return carry

    lax.fori_loop(0, rows, body, 0, unroll=8)


def _na_attention(qkv, bias, layer, bsz, seq):
    rows = seq // GRID_W
    n_heads = GROUP_W // NA_HEAD_DIM
    t = qkv.shape[0]
    blk = lambda j: pl.BlockSpec((seq, GROUP_W), lambda b, j=j: (b, j))
    return pl.pallas_call(
        functools.partial(_na_kernel, rows=rows, n_heads=n_heads),
        grid=(bsz,),
        in_specs=[blk(0), blk(1), blk(2),
                  pl.BlockSpec((None,) + bias.shape[1:], lambda b: (layer, 0, 0, 0))],
        out_specs=pl.BlockSpec((seq, GROUP_W), lambda b: (b, 0)),
        out_shape=jax.ShapeDtypeStruct((t, GROUP_W), F32),
        compiler_params=_cparams(("arbitrary",)),
        name="na_attention",
    )(qkv, qkv, qkv, bias)


CONV_PAD = 16
CONV_CHUNK = 256


def _conv_kernel(b_ref, c_ref, x_ref, a_ref, g_ref, scw_ref, cfw_ref, cfb_ref, lng_ref, lnb_ref,
                 ysc_ref, ycf_ref, psc, pcf):
    seq = b_ref.shape[0]
    n_chunks = seq // CONV_CHUNK
    zeros = jnp.zeros((CONV_PAD, GROUP_W), F32)
    psc[0:CONV_PAD, :] = zeros
    pcf[0:CONV_PAD, :] = zeros
    psc[CONV_PAD + seq:2 * CONV_PAD + seq, :] = zeros
    pcf[CONV_PAD + seq:2 * CONV_PAD + seq, :] = zeros

    def fill(i, carry):
        r0 = pl.multiple_of(i * CONV_CHUNK, CONV_CHUNK)
        rows = pl.ds(r0, CONV_CHUNK)
        dst = pl.ds(r0 + CONV_PAD, CONV_CHUNK)
        psc[dst, :] = c_ref[rows, :] * x_ref[rows, :]
        pcf[dst, :] = a_ref[rows, :] * jax.nn.sigmoid(g_ref[rows, :])
        return carry

    lax.fori_loop(0, n_chunks, fill, 0)

    def taps(win, w_ref):
        n_taps = w_ref.shape[0]
        offs = [CONV_PAD + k - n_taps // 2 for k in range(n_taps)]
        n_win = win.shape[0]
        acc = None
        for s in range(SUBLANES):
            ks = [k for k in range(n_taps) if offs[k] % SUBLANES == s]
            if not ks:
                continue
            shifted = pltpu.roll(win, n_win - s, axis=0) if s else win
            for k in ks:
                q = offs[k] - s
                term = shifted[q:q + CONV_CHUNK, :] * w_ref[k:k + 1, :]
                acc = term if acc is None else acc + term
        return acc

    def conv(i, carry):
        r0 = pl.multiple_of(i * CONV_CHUNK, CONV_CHUNK)
        rows = pl.ds(r0, CONV_CHUNK)
        window = pl.ds(r0, CONV_CHUNK + 2 * CONV_PAD)
        ysc_ref[rows, :] = b_ref[rows, :] * taps(psc[window, :], scw_ref)
        cf = taps(pcf[window, :], cfw_ref) + cfb_ref[...]
        mu = jnp.mean(cf, axis=-1, keepdims=True)
        xc = cf - mu
        var = jnp.mean(xc * xc, axis=-1, keepdims=True)
        ln = xc * lax.rsqrt(var + EPS) * lng_ref[...] + lnb_ref[...]
        ycf_ref[rows, :] = ln * jax.nn.sigmoid(ln)
        return carry

    lax.fori_loop(0, n_chunks, conv, 0)


def _conv_mixers(rest, sc_w, cf_w, cf_b, ln_g, ln_b, bsz, seq):
    t = rest.shape[0]
    blk = lambda j: pl.BlockSpec((seq, GROUP_W), lambda b, j=j: (b, j))
    full = lambda a: pl.BlockSpec(a.shape, lambda b: (0,) * a.ndim)
    cf_b, ln_g, ln_b = (a.reshape(1, GROUP_W) for a in (cf_b, ln_g, ln_b))
    out_spec = pl.BlockSpec((seq, GROUP_W), lambda b: (b, 0))
    return pl.pallas_call(
        _conv_kernel,
        grid=(bsz,),
        in_specs=[blk(0), blk(1), blk(2), blk(3), blk(4),
                  full(sc_w), full(cf_w), full(cf_b), full(ln_g), full(ln_b)],
        out_specs=[out_spec, out_spec],
        out_shape=[jax.ShapeDtypeStruct((t, GROUP_W), F32)] * 2,
        scratch_shapes=[pltpu.VMEM((seq + 2 * CONV_PAD, GROUP_W), F32)] * 2,
        compiler_params=_cparams(("arbitrary",)),
        name="conv_mixers",
    )(rest, rest, rest, rest, rest, sc_w, cf_w, cf_b, ln_g, ln_b)


S5_TL = 128
S5_LANE_SPLIT = 1
S5_ROW_BLOCK = 256


def _s5_discretise(a_re, a_im, log_dt, b_re, b_im, c_re, c_im, bsz):
    f32 = F32
    a_re, a_im, log_dt = a_re.astype(f32), a_im.astype(f32), log_dt.astype(f32)
    n_dir, n_grp, n_state = a_re.shape
    n_ch = b_re.shape[-1]
    dt = jnp.exp(log_dt)[..., None]
    mag = jnp.exp(a_re * dt)
    abr = mag * jnp.cos(a_im * dt)
    abi = mag * jnp.sin(a_im * dt)
    den = a_re * a_re + a_im * a_im
    fr = ((abr - 1.0) * a_re + abi * a_im) / den
    fi = (abi * a_re - (abr - 1.0) * a_im) / den
    bbr = fr[..., None] * b_re - fi[..., None] * b_im
    bbi = fr[..., None] * b_im + fi[..., None] * b_re
    eye = jnp.eye(n_grp, dtype=f32)
    blk_b = lambda m: jnp.einsum("dgps,gh->dgshp", m, eye).reshape(n_dir, n_grp * n_ch, n_grp * n_state)
    blk_c = lambda m: jnp.einsum("dgsp,gh->dgphs", m, eye).reshape(n_dir, n_grp * n_state, n_grp * n_ch)
    bmat = jnp.concatenate([blk_b(bbr), blk_b(bbi)], axis=-1)
    cmat = jnp.concatenate([blk_c(c_re.astype(f32)), blk_c(-c_im.astype(f32))], axis=1)
    amat = jnp.stack([abr.reshape(n_dir, -1), abi.reshape(n_dir, -1)], axis=1)
    amat = jnp.broadcast_to(amat[:, :, None, :], (n_dir, 2, bsz, n_grp * n_state))
    return bmat.astype(BF16), cmat.astype(BF16), amat


def _s5_kernel(*refs, bsz, backward):
    if backward:
        (u_ref, b_ref, c_ref, a_ref, yf_ref, d_ref, w_ref, bg_ref, o_ref, xs, st, tb) = refs
    else:
        (u_ref, b_ref, c_ref, a_ref, o_ref, xs, st, tb) = refs
    n_state = a_ref.shape[-1]

    @pl.when(pl.program_id(0) == 0)
    def _():
        st[...] = jnp.zeros_like(st)

    n_half = GROUP_W // LANES
    for b in range(bsz):
        for h in range(n_half):
            tb[h, pl.ds(b, S5_TL, stride=bsz), :] = u_ref[b, :, h * LANES:(h + 1) * LANES]
    for r0 in range(0, S5_TL * bsz, S5_ROW_BLOCK):
        rows = slice(r0, r0 + S5_ROW_BLOCK)
        u_tb = jnp.concatenate([tb[h, rows, :] for h in range(n_half)], axis=1)
        xs[rows, :] = jnp.dot(u_tb.astype(BF16), b_ref[...], preferred_element_type=F32)

    cw = n_state // S5_LANE_SPLIT
    for c in range(S5_LANE_SPLIT):
        re = slice(c * cw, (c + 1) * cw)
        im = slice(n_state + c * cw, n_state + (c + 1) * cw)
        a_r = a_ref[0, :, re]
        a_i = a_ref[1, :, re]

        def step(t, carry, re=re, im=im, a_r=a_r, a_i=a_i):
            s_r, s_i = carry
            tt = S5_TL - 1 - t if backward else t
            rows = pl.ds(pl.multiple_of(tt * bsz, bsz), bsz)
            n_r = a_r * s_r - a_i * s_i + xs[rows, re]
            n_i = a_r * s_i + a_i * s_r + xs[rows, im]
            xs[rows, re] = n_r
            xs[rows, im] = n_i
            return n_r, n_i

        s_r, s_i = lax.fori_loop(0, S5_TL, step, (st[:, re], st[:, im]), unroll=4)
        st[:, re] = s_r
        st[:, im] = s_i

    n_rows = S5_TL * bsz
    y_tb = jnp.concatenate(
        [jnp.dot(xs[r0:r0 + S5_ROW_BLOCK, :].astype(BF16), c_ref[...], preferred_element_type=F32)
         for r0 in range(0, n_rows, S5_ROW_BLOCK)], axis=0)
    if not backward:
        o_ref[...] = y_tb
        return
    y_sum = y_tb + yf_ref[...]
    for h in range(n_half):
        tb[h] = y_sum[:, h * LANES:(h + 1) * LANES]
    y_bt = jnp.concatenate(
        [jnp.concatenate([tb[h, pl.ds(b, S5_TL, stride=bsz), :] for h in range(n_half)], axis=1)
         for b in range(bsz)], axis=0)
    y = jax.nn.gelu(d_ref[...] * u_ref[...].reshape(S5_TL * bsz, GROUP_W) + y_bt)
    gate = jnp.dot(y.astype(BF16), w_ref[...], preferred_element_type=F32) + bg_ref[...]
    o_ref[...] = (y * jax.nn.sigmoid(gate)).reshape(bsz, S5_TL, GROUP_W)


def _s5_mixer(rest, ssm_col, bmat, cmat, amat, layer, d_skip, w_glu, b_glu, bsz, seq):
    n_chunks = seq // S5_TL
    n_rows = S5_TL * bsz
    n_st2 = bmat.shape[-1]
    rest3 = rest.reshape(bsz, seq, rest.shape[1])
    full = lambda a: pl.BlockSpec(a.shape, lambda i: (0,) * a.ndim)
    scratch = [pltpu.VMEM((n_rows, n_st2), F32),
               pltpu.VMEM((bsz, n_st2), F32),
               pltpu.VMEM((GROUP_W // LANES, n_rows, LANES), F32)]

    def direction(d, chunk_of):
        return [
            pl.BlockSpec((bsz, S5_TL, GROUP_W), lambda i: (0, chunk_of(i), ssm_col)),
            pl.BlockSpec((None, None, GROUP_W, n_st2), lambda i: (layer, d, 0, 0)),
            pl.BlockSpec((None, None, n_st2, GROUP_W), lambda i: (layer, d, 0, 0)),
            pl.BlockSpec((None, None, 2, bsz, n_st2 // 2), lambda i: (layer, d, 0, 0, 0)),
        ]

    y_fwd = pl.pallas_call(
        functools.partial(_s5_kernel, bsz=bsz, backward=False),
        grid=(n_chunks,),
        in_specs=direction(0, lambda i: i),
        out_specs=pl.BlockSpec((n_rows, GROUP_W), lambda i: (i, 0)),
        out_shape=jax.ShapeDtypeStruct((seq * bsz, GROUP_W), F32),
        scratch_shapes=scratch,
        compiler_params=_cparams(("arbitrary",)),
        name="s5_forward",
    )(rest3, bmat, cmat, amat)

    rev = lambda i: n_chunks - 1 - i
    d_skip, b_glu = d_skip.reshape(1, GROUP_W), b_glu.reshape(1, GROUP_W)
    out = pl.pallas_call(
        functools.partial(_s5_kernel, bsz=bsz, backward=True),
        grid=(n_chunks,),
        in_specs=direction(1, rev) + [
            pl.BlockSpec((n_rows, GROUP_W), lambda i: (rev(i), 0)),
            full(d_skip), full(w_glu), full(b_glu),
        ],
        out_specs=pl.BlockSpec((bsz, S5_TL, GROUP_W), lambda i: (0, rev(i), 0)),
        out_shape=jax.ShapeDtypeStruct((bsz, seq, GROUP_W), F32),
        scratch_shapes=scratch,
        compiler_params=_cparams(("arbitrary",)),
        name="s5_backward",
    )(rest3, bmat, cmat, amat, y_fwd, d_skip, w_glu, b_glu)
    return out.reshape(bsz * seq, GROUP_W)


FFN_CHUNK = 256
ROUTE_COLS = 8
ROUTER_LANES = 128
ROUTER_ROW_BLOCK = 128


def _route_records(h, wr_ref):
    h_hi = h.astype(BF16)
    h_lo = (h - h_hi.astype(F32)).astype(BF16)
    pieces = jnp.concatenate([h_hi, h_lo, h_hi], axis=1)
    logits = jnp.concatenate(
        [jnp.dot(pieces[r0:r0 + ROUTER_ROW_BLOCK], wr_ref[...], preferred_element_type=F32)
         for r0 in range(0, h.shape[0], ROUTER_ROW_BLOCK)], axis=0)
    lane = lax.broadcasted_iota(jnp.int32, logits.shape, 1)
    minus_inf = -jnp.inf
    l1 = jnp.where(lane < N_EXPERTS, logits, minus_inf)
    m1 = jnp.max(l1, axis=-1, keepdims=True)
    i1 = jnp.min(jnp.where(l1 == m1, lane, ROUTER_LANES), axis=-1, keepdims=True)
    l2 = jnp.where(lane == i1, minus_inf, l1)
    m2 = jnp.max(l2, axis=-1, keepdims=True)
    i2 = jnp.min(jnp.where(l2 == m2, lane, ROUTER_LANES), axis=-1, keepdims=True)
    e2 = jnp.exp(m2 - m1)
    den = 1.0 + e2
    rec = jnp.where(lane == 0, i1.astype(F32),
                    jnp.where(lane == 1, i2.astype(F32),
                              jnp.where(lane == 2, 1.0 / den,
                                        jnp.where(lane == 3, e2 / den, 0.0))))
    return rec[:, :ROUTE_COLS]


def _mixer_out_kernel(*refs, tail):
    y_refs, (gn_ref, w_ref, x_ref, g2_ref) = refs[:4], refs[4:8]
    acc = x_ref[...]
    for j, y_ref in enumerate(y_refs):
        yn = _rms(y_ref[...], gn_ref[j:j + 1, :]).astype(BF16)
        acc = acc + jnp.dot(yn, w_ref[j * GROUP_W:(j + 1) * GROUP_W, :], preferred_element_type=F32)
    h = _rms(acc, g2_ref[...])
    if tail == "route":
        wr_ref, o_ref, route_ref = refs[8:]
        _store_token_tiles(o_ref, acc)
        route_ref[...] = _route_records(h, wr_ref)
        return
    wg_hbm, wu_hbm, wd_hbm, o_ref, wg_ref, wu_ref, wd_ref, sem = refs[8:]

    @pl.when(pl.program_id(0) == 0)
    def _():
        copies = [pltpu.make_async_copy(src, dst, sem.at[k])
                  for k, (src, dst) in enumerate(((wg_hbm, wg_ref), (wu_hbm, wu_ref), (wd_hbm, wd_ref)))]
        for c in copies:
            c.start()
        for c in copies:
            c.wait()

    h = h.astype(BF16)
    acts = []
    for c in range(wg_ref.shape[0]):
        gate = jnp.dot(h, wg_ref[c], preferred_element_type=F32)
        up = jnp.dot(h, wu_ref[c], preferred_element_type=F32)
        acts.append((gate * jax.nn.sigmoid(gate) * up).astype(BF16))
    o_ref[...] = acc + jnp.dot(jnp.concatenate(acts, axis=1), wd_ref[...], preferred_element_type=F32)


def _mixer_out(ys, gn, w_out, x2, g2, tail, tail_weights, tm=512):
    t, d = x2.shape
    row = lambda n: pl.BlockSpec((tm, n), lambda i: (i, 0))
    whole = lambda a: pl.BlockSpec(a.shape, lambda i: (0, 0))
    if tail == "route":
        wr = jnp.zeros((d, ROUTER_LANES), F32).at[:, :N_EXPERTS].set(tail_weights[0].astype(F32))
        wr_hi = wr.astype(BF16)
        wr_lo = (wr - wr_hi.astype(F32)).astype(BF16)
        tail_weights = (jnp.concatenate([wr_hi, wr_hi, wr_lo], axis=0),)
        tail_specs = [whole(w) for w in tail_weights]
        out_specs = [pl.BlockSpec((tm * SUBLANES, LANES), lambda i: (i, 0)), row(ROUTE_COLS)]
        out_shape = [jax.ShapeDtypeStruct((t * SUBLANES, LANES), F32), jax.ShapeDtypeStruct((t, ROUTE_COLS), F32)]
        scratch = []
    else:
        w_gate, w_up, w_down = tail_weights
        n_chunks = w_gate.shape[1] // FFN_CHUNK
        by_cols = lambda w: jnp.transpose(w.reshape(d, n_chunks, FFN_CHUNK), (1, 0, 2))
        tail_weights = (by_cols(w_gate), by_cols(w_up), w_down)
        tail_specs = [pl.BlockSpec(memory_space=pl.ANY)] * 3
        out_specs, out_shape = row(d), jax.ShapeDtypeStruct((t, d), F32)
        scratch = [pltpu.VMEM(w.shape, w.dtype) for w in tail_weights] + [pltpu.SemaphoreType.DMA((3,))]
    return pl.pallas_call(
        functools.partial(_mixer_out_kernel, tail=tail),
        grid=(t // tm,),
        in_specs=[row(GROUP_W)] * 4 + [whole(gn), whole(w_out), row(d), whole(g2)] + tail_specs,
        out_specs=out_specs,
        out_shape=out_shape,
        scratch_shapes=scratch,
        compiler_params=_cparams(("arbitrary",)),
        name="mixer_out_" + tail,
    )(*ys, gn, w_out, x2, g2, *tail_weights)


def _moe_plan(route, tm, n_tiles):
    ids = route[:, :2]
    experts = jnp.where((ids >= 0) & (ids <= N_EXPERTS - 1), ids, 0).astype(jnp.int32).reshape(-1)
    onehot = (experts[:, None] == jnp.arange(N_EXPERTS, dtype=jnp.int32)[None, :]).astype(jnp.int32)
    csum = jnp.cumsum(onehot, axis=0)
    rank = jnp.sum((csum - onehot) * onehot, axis=1)
    count = csum[-1]
    padded = ((count + tm - 1) // tm) * tm
    end = jnp.cumsum(padded)
    start = end - padded
    slot = jnp.sum(onehot * start[None, :], axis=1) + rank
    n_active = end[-1] // tm
    tile_start = jnp.arange(n_tiles, dtype=jnp.int32) * tm
    tile_expert = jnp.sum((tile_start[:, None] >= end[None, :]).astype(jnp.int32), axis=1)
    tile_expert = jnp.minimum(tile_expert, N_EXPERTS - 1)
    last = tile_expert[jnp.maximum(n_active - 1, 0)]
    tile_expert = jnp.where(jnp.arange(n_tiles) < n_active, tile_expert, last)
    return slot.astype(jnp.int32), tile_expert.astype(jnp.int32), n_active.astype(jnp.int32).reshape(1)


def _token_copy(src, src_tok, dst, dst_tok, sem):
    rows = lambda tok: pl.ds(pl.multiple_of(tok * SUBLANES, SUBLANES), SUBLANES)
    return pltpu.make_async_copy(src.at[rows(src_tok)], dst.at[rows(dst_tok)], sem)


def _invert_kernel(slot_ref, word_ref, pad_hbm, plan_ref, sem):
    @pl.when(pl.program_id(0) == 0)
    def _():
        init = pltpu.make_async_copy(pad_hbm, plan_ref, sem)
        init.start()
        init.wait()

    def place(a, carry):
        plan_ref[slot_ref[0, 0, a]] = word_ref[0, 0, a]
        return carry

    lax.fori_loop(0, slot_ref.shape[-1], place, 0, unroll=8)


def _plan_bits(n_tok, tm):
    src_bits = (n_tok - 1).bit_length()
    assert (2 * n_tok + 2 * tm) << src_bits < 2 ** 31 and tm & (tm - 1) == 0
    return src_bits


def _invert(slot, n_slots, tm, n_tok, chunk=4096):
    src_bits = _plan_bits(n_tok, tm)
    a = jnp.arange(slot.shape[0], dtype=jnp.int32)
    tok = a >> 1
    word = tok + (((a & 1) * n_tok + tok) << src_bits)
    s = jnp.arange(n_slots, dtype=jnp.int32)
    pad_plan = (2 * n_tok + (s & (2 * tm - 1))) << src_bits
    n_chunks = slot.shape[0] // chunk
    chunked = pl.BlockSpec((1, 1, chunk), lambda c: (c, 0, 0), memory_space=pltpu.SMEM)
    return pl.pallas_call(
        _invert_kernel,
        grid=(n_chunks,),
        in_specs=[chunked, chunked, pl.BlockSpec(memory_space=pl.ANY)],
        out_specs=pl.BlockSpec((n_slots,), lambda c: (0,), memory_space=pltpu.SMEM),
        out_shape=jax.ShapeDtypeStruct((n_slots,), jnp.int32),
        scratch_shapes=[pltpu.SemaphoreType.DMA(())],
        compiler_params=pltpu.CompilerParams(dimension_semantics=("arbitrary",)),
        name="moe_invert",
    )(slot.reshape(n_chunks, 1, chunk), word.reshape(n_chunks, 1, chunk), pad_plan)


def _tokens_done(ref, n_tok, sem):
    rows = pl.ds(0, n_tok * SUBLANES)
    pltpu.make_async_copy(ref.at[rows], ref.at[rows], sem).wait()


def _moe_ffn_kernel(te_ref, nact_ref, plan_ref, x_hbm, g_ref, wg_ref, wu_ref, wd_ref, yk_hbm,
                    xbuf, xb16, ybuf, acc_ref, sem_in, sem_out, *, tm, n_tok, n_f, pad_tile, src_bits):
    del te_ref
    i = pl.program_id(0)
    f = pl.program_id(1)
    par = i % 2
    nact = nact_ref[0]

    def fetch(tile, p, j):
        src = plan_ref[tile * tm + j] & ((1 << src_bits) - 1)
        _token_copy(x_hbm, src, xbuf.at[p], j, sem_in.at[p]).start(priority=1)

    def send(tile, p, j):
        dst = plan_ref[tile * tm + j] >> src_bits
        _token_copy(ybuf.at[p], j, yk_hbm, dst, sem_out.at[p]).start()

    @pl.when((i == 0) & (f == 0))
    def _():
        def first(j, carry):
            fetch(0, 0, j)
            return carry
        lax.fori_loop(0, tm, first, 0)
        ybuf[...] = jnp.zeros_like(ybuf)
        for p in range(2):
            spare_rows = pl.ds((2 * n_tok + p * tm) * SUBLANES, tm * SUBLANES)
            spare = pltpu.make_async_copy(ybuf.at[p], yk_hbm.at[spare_rows], sem_out.at[p])
            spare.start()
            spare.wait()
        _tokens_done(xbuf.at[0], tm, sem_in.at[0])

    @pl.when((i >= 1) & (i <= nact) & (f == 0))
    def _():
        _tokens_done(xbuf.at[par], tm, sem_in.at[par])

    @pl.when(i < nact)
    def _():
        @pl.when(f == 0)
        def _():
            xb16[...] = _rms(_load_token_tiles(xbuf.at[par], tm), g_ref[...]).astype(BF16)
            acc_ref[...] = jnp.zeros_like(acc_ref)

            for j in range(tm):
                fetch(i + 1, 1 - par, j)

        @pl.when(f == n_f - 1)
        def _():
            prev = jnp.where(i > 0, i - 1, pad_tile)

            for j in range(tm):
                send(prev, 1 - par, j)

        x = xb16[...]
        gate = jnp.dot(x, wg_ref[...], preferred_element_type=F32)
        up = jnp.dot(x, wu_ref[...], preferred_element_type=F32)
        act = (gate * jax.nn.sigmoid(gate) * up).astype(BF16)
        acc_ref[...] += jnp.dot(act, wd_ref[...], preferred_element_type=F32)

        @pl.when(f == n_f - 1)
        def _():
            @pl.when(i >= 1)
            def _():
                _tokens_done(ybuf.at[par], tm, sem_out.at[par])

            _store_token_tiles(ybuf.at[par], acc_ref[...])

    @pl.when((i == nact) & (f == 0))
    def _():
        def last(j, carry):
            send(i - 1, 1 - par, j)
            return carry
        lax.fori_loop(0, tm, last, 0)
        _tokens_done(ybuf.at[0], tm, sem_out.at[0])
        _tokens_done(ybuf.at[1], tm, sem_out.at[1])


def _moe_ffn(xt, g, plan, tile_expert, n_active, wg, wu, wd, tm):
    n_tok = xt.shape[0] // SUBLANES
    d = SUBLANES * LANES
    n_tiles = tile_expert.shape[0]
    pad_tile = n_tiles - 1 if (n_tiles - 1) % 2 else n_tiles
    assert plan.shape[0] >= (pad_tile + 1) * tm
    dff = wg.shape[-1]
    n_f = 2
    tf = dff // n_f
    fidx = lambda i, f, na: jnp.where(i < na[0], f, n_f - 1)
    grid_spec = pltpu.PrefetchScalarGridSpec(
        num_scalar_prefetch=3,
        grid=(n_tiles, n_f),
        in_specs=[
            pl.BlockSpec(memory_space=pl.ANY),
            pl.BlockSpec((1, d), lambda i, f, te, na, plan: (0, 0)),
            pl.BlockSpec((None, d, tf), lambda i, f, te, na, plan: (te[i], 0, fidx(i, f, na))),
            pl.BlockSpec((None, d, tf), lambda i, f, te, na, plan: (te[i], 0, fidx(i, f, na))),
            pl.BlockSpec((None, tf, d), lambda i, f, te, na, plan: (te[i], fidx(i, f, na), 0)),
        ],
        out_specs=pl.BlockSpec(memory_space=pl.ANY),
        scratch_shapes=[
            pltpu.VMEM((2, tm * SUBLANES, LANES), F32),
            pltpu.VMEM((tm, d), BF16),
            pltpu.VMEM((2, tm * SUBLANES, LANES), F32),
            pltpu.VMEM((tm, d), F32),
            pltpu.SemaphoreType.DMA((2,)),
            pltpu.SemaphoreType.DMA((2,)),
        ],
    )
    return pl.pallas_call(
        functools.partial(_moe_ffn_kernel, tm=tm, n_tok=n_tok, n_f=n_f, pad_tile=pad_tile,
                          src_bits=_plan_bits(n_tok, tm)),
        grid_spec=grid_spec,
        out_shape=jax.ShapeDtypeStruct(((2 * n_tok + 2 * tm) * SUBLANES, LANES), F32),
        compiler_params=_cparams(("arbitrary", "arbitrary")),
        name="moe_ffn",
    )(tile_expert, n_active, plan, xt, g.reshape(1, d), wg, wu, wd)


def _combine_kernel(route_ref, x_ref, y0_ref, y1_ref, fg_ref, o_ref, *, final_norm):
    r = route_ref[...]
    tm = r.shape[0]
    y = (_load_token_tiles(x_ref, tm) + r[:, 2:3] * _load_token_tiles(y0_ref, tm)
         + r[:, 3:4] * _load_token_tiles(y1_ref, tm))
    o_ref[...] = _rms(y, fg_ref[...]) if final_norm else y


def _combine(yk, route, xt, final_g, final_norm, tm=512):
    t = xt.shape[0] // SUBLANES
    d = SUBLANES * LANES
    n_blk = t // tm
    tiles = lambda f: pl.BlockSpec((tm * SUBLANES, LANES), f)
    return pl.pallas_call(
        functools.partial(_combine_kernel, final_norm=final_norm),
        grid=(n_blk,),
        in_specs=[
            pl.BlockSpec((tm, ROUTE_COLS), lambda i: (i, 0)),
            tiles(lambda i: (i, 0)),
            tiles(lambda i: (i, 0)),
            tiles(lambda i: (n_blk + i, 0)),
            pl.BlockSpec((1, d), lambda i: (0, 0)),
        ],
        out_specs=pl.BlockSpec((tm, d), lambda i: (i, 0)),
        out_shape=jax.ShapeDtypeStruct((t, d), F32),
        compiler_params=_cparams(("arbitrary",)),
        name="moe_combine",
    )(route, xt, yk, yk, final_g.reshape(1, d))


def _final_norm_kernel(x_ref, g_ref, o_ref):
    o_ref[...] = _rms(x_ref[...], g_ref[...])


def _final_norm(x2, g, tm=512):
    t, d = x2.shape
    return pl.pallas_call(
        _final_norm_kernel,
        grid=(t // tm,),
        in_specs=[pl.BlockSpec((tm, d), lambda i: (i, 0)), pl.BlockSpec((1, d), lambda i: (0, 0))],
        out_specs=pl.BlockSpec((tm, d), lambda i: (i, 0)),
        out_shape=jax.ShapeDtypeStruct((t, d), F32),
        compiler_params=_cparams(("arbitrary",)),
        name="final_norm",
    )(x2, g.reshape(1, d))


MOE_TM = 512


def kernel(x, norm1_g, w_in, na_rpb, sc_conv_w, cf_conv_w, cf_conv_b, cf_ln_g, cf_ln_b, ssm_a_re, ssm_a_im,
           ssm_log_dt, ssm_b_re, ssm_b_im, ssm_c_re, ssm_c_im, ssm_d, ssm_w_glu, ssm_b_glu, grp_norm_g, w_out,
           norm2_g, ffn_w_gate, ffn_w_up, ffn_w_down, moe_w_router, moe_w_gate, moe_w_up, moe_w_down,
           final_norm_g):
    bsz, seq, d = x.shape
    depth = w_in.shape[0]
    t = bsz * seq
    rows = seq // GRID_W
    x2 = x.reshape(t, d).astype(F32)
    final_done = False
    na_bias = jax.vmap(lambda rpb: _na_bias_table(rpb, rows))(na_rpb)
    s5_mats = jax.vmap(lambda *p: _s5_discretise(*p, bsz))(ssm_a_re, ssm_a_im, ssm_log_dt, ssm_b_re, ssm_b_im,
                                                           ssm_c_re, ssm_c_im)
    for i in range(depth):
        qkv, rest = _in_proj(x2, norm1_g[i], w_in[i].astype(BF16))
        y_na = _na_attention(qkv, na_bias, i, bsz, seq)
        y_sc, y_cf = _conv_mixers(rest, sc_conv_w[i], cf_conv_w[i], cf_conv_b[i], cf_ln_g[i], cf_ln_b[i],
                                  bsz, seq)
        y_ssm = _s5_mixer(rest, SSM_COL, *s5_mats, i, ssm_d[i], ssm_w_glu[i].astype(BF16), ssm_b_glu[i],
                          bsz, seq)
        ys = (y_na, y_sc, y_cf, y_ssm)
        gn, g2 = grp_norm_g[i].reshape(4, GROUP_W), norm2_g[i].reshape(1, d)
        j = i // 2
        if i % 2 == 0:
            ffn_w = (ffn_w_gate[j].astype(BF16), ffn_w_up[j].astype(BF16), ffn_w_down[j].astype(BF16))
            x2 = _mixer_out(ys, gn, w_out[i].astype(BF16), x2, g2, "ffn", ffn_w)
        else:
            x2, route = _mixer_out(ys, gn, w_out[i].astype(BF16), x2, g2, "route", (moe_w_router[j],))
            n_tiles = (2 * t) // MOE_TM + N_EXPERTS + 1
            slot, tile_expert, n_active = _moe_plan(route, MOE_TM, n_tiles)
            plan = _invert(slot, (n_tiles + 1) * MOE_TM, MOE_TM, t)
            yk = _moe_ffn(x2, norm2_g[i], plan, tile_expert, n_active, moe_w_gate[j].astype(BF16),
                          moe_w_up[j].astype(BF16), moe_w_down[j].astype(BF16), MOE_TM)
            final_done = i == depth - 1
            x2 = _combine(yk, route, x2, final_norm_g, final_done)
    if not final_done:
        x2 = _final_norm(x2, final_norm_g)
    return x2.reshape(bsz, seq, d)
```

```python
import functools
import math

import jax
import jax.numpy as jnp
import numpy as np
from jax import lax
from jax.experimental import pallas as pl
from jax.experimental.pallas import tpu as pltpu

F32 = jnp.float32
BF16 = jnp.bfloat16
EPS = 1e-6
NEG_INF = -1e30

GRID_W = 64
NA_HEAD_DIM = 64
NA_WIN_H = 8
NA_WIN_W = 16
GROUP_W = 256
SSM_GROUP_CH = 16
SSM_STATE = 64
N_EXPERTS = 8
VMEM_LIMIT = 56 * 1024 * 1024


def _cparams(sem, vmem=VMEM_LIMIT):
    return pltpu.CompilerParams(dimension_semantics=sem, vmem_limit_bytes=vmem)


def _rms(x, g):
    return x * lax.rsqrt(jnp.mean(x * x, axis=-1, keepdims=True) + EPS) * g


SUBLANES = 8
LANES = 128


def _load_token_tiles(ref, n):
    return jnp.concatenate([ref[pl.ds(s, n, stride=SUBLANES), :] for s in range(SUBLANES)], axis=1)


def _store_token_tiles(ref, val):
    for s in range(SUBLANES):
        ref[pl.ds(s, val.shape[0], stride=SUBLANES), :] = val[:, s * LANES:(s + 1) * LANES]


N_QKV_COLS = 3 * GROUP_W
SSM_COL = 5


def _in_proj_kernel(x_ref, g_ref, w_ref, qkv_ref, rest_ref):
    h = _rms(x_ref[...], g_ref[...]).astype(BF16)
    n_qkv = qkv_ref.shape[1] // GROUP_W
    n_rest = rest_ref.shape[1] // GROUP_W
    for j in range(n_qkv + n_rest):
        z = jnp.dot(h, w_ref[:, j * GROUP_W:(j + 1) * GROUP_W], preferred_element_type=F32)
        if j < n_qkv:
            qkv_ref[:, j * GROUP_W:(j + 1) * GROUP_W] = z.astype(BF16)
        else:
            k = j - n_qkv
            rest_ref[:, k * GROUP_W:(k + 1) * GROUP_W] = z


def _in_proj(x2, g, w, tm=512):
    t, d = x2.shape
    n_cols = w.shape[1]
    n_rest = n_cols - N_QKV_COLS
    return pl.pallas_call(
        _in_proj_kernel,
        grid=(t // tm,),
        in_specs=[
            pl.BlockSpec((tm, d), lambda i: (i, 0)),
            pl.BlockSpec((1, d), lambda i: (0, 0)),
            pl.BlockSpec((d, n_cols), lambda i: (0, 0)),
        ],
        out_specs=[
            pl.BlockSpec((tm, N_QKV_COLS), lambda i: (i, 0)),
            pl.BlockSpec((tm, n_rest), lambda i: (i, 0)),
        ],
        out_shape=[
            jax.ShapeDtypeStruct((t, N_QKV_COLS), BF16),
            jax.ShapeDtypeStruct((t, n_rest), F32),
        ],
        compiler_params=_cparams(("arbitrary",)),
        name="in_proj",
    )(x2, g.reshape(1, d), w)


def _na_bias_table(rpb, rows):
    n_heads = rpb.shape[0]
    kh = NA_WIN_H
    c = np.arange(GRID_W)
    qcs = np.clip(c - NA_WIN_W // 2, 0, GRID_W - NA_WIN_W)
    kc = np.arange(GRID_W)
    in_win = (kc[None, :] >= qcs[:, None]) & (kc[None, :] < qcs[:, None] + NA_WIN_W)
    dc_idx = np.clip(kc[None, :] - c[:, None] + NA_WIN_W - 1, 0, 2 * NA_WIN_W - 2)
    onehot_dc = (dc_idx[:, :, None] == np.arange(2 * NA_WIN_W - 1)).astype(np.float32)
    rep_rows = np.array([0, 1, 2, 3, 4, rows - 3, rows - 2, rows - 1])
    row_start = np.clip(rep_rows - kh // 2, 0, rows - kh)
    dr_idx = (row_start - rep_rows + NA_WIN_H - 1)[:, None] + np.arange(kh)[None, :]
    onehot_dr = (dr_idx[:, :, None] == np.arange(2 * NA_WIN_H - 1)).astype(np.float32)
    tab = jnp.einsum("via,hab,ckb->vhcik", jnp.asarray(onehot_dr), rpb.astype(F32), jnp.asarray(onehot_dc),
                     precision=lax.Precision.HIGHEST)
    tab = jnp.where(jnp.asarray(in_win)[None, None, :, None, :], tab, NEG_INF)
    return tab.reshape(len(rep_rows), n_heads * GRID_W, kh * GRID_W)


def _na_kernel(q_ref, k_ref, v_ref, bias_ref, o_ref, *, rows, n_heads):
    kh = NA_WIN_H
    lane = lax.broadcasted_iota(jnp.int32, (GRID_W, n_heads * NA_HEAD_DIM), 1)
    masks = [(lane >= h * NA_HEAD_DIM) & (lane < (h + 1) * NA_HEAD_DIM) for h in range(n_heads)]
    scale = NA_HEAD_DIM ** -0.5

    def body(r, carry):
        q0 = pl.multiple_of(r * GRID_W, GRID_W)
        q = q_ref[pl.ds(q0, GRID_W), :].astype(F32) * scale
        qs = jnp.concatenate([jnp.where(m, q, 0.0) for m in masks], axis=0).astype(BF16)
        rs = jnp.clip(r - kh // 2, 0, rows - kh)
        k0 = pl.multiple_of(rs * GRID_W, GRID_W)
        ks = k_ref[pl.ds(k0, kh * GRID_W), :]
        vs = v_ref[pl.ds(k0, kh * GRID_W), :]
        s = lax.dot_general(qs, ks, (((1,), (1,)), ((), ())), preferred_element_type=F32)
        var = jnp.where(r < kh // 2, r, jnp.where(r > rows - kh // 2, r - (rows - 8), kh // 2))
        s = s + bias_ref[var]
        m = jnp.max(s, axis=-1, keepdims=True)
        p = jnp.exp(s - m)
        den = jnp.sum(p, axis=-1, keepdims=True)
        o = jnp.dot(p.astype(BF16), vs, preferred_element_type=F32) / den
        out = jnp.where(masks[0], o[0:GRID_W], 0.0)
        for h in range(1, n_heads):
            out = out + jnp.where(masks[h], o[h * GRID_W:(h + 1) * GRID_W], 0.0)
        o_ref[pl.ds(q0, GRID_W), :] = out
        return carry

    lax.fori_loop(0, rows, body, 0, unroll=8)


def _na_attention(qkv, bias, layer, bsz, seq):
    rows = seq // GRID_W
    n_heads = GROUP_W // NA_HEAD_DIM
    t = qkv.shape[0]
    blk = lambda j: pl.BlockSpec((seq, GROUP_W), lambda b, j=j: (b, j))
    return pl.pallas_call(
        functools.partial(_na_kernel, rows=rows, n_heads=n_heads),
        grid=(bsz,),
        in_specs=[blk(0), blk(1), blk(2),
                  pl.BlockSpec((None,) + bias.shape[1:], lambda b: (layer, 0, 0, 0))],
        out_specs=pl.BlockSpec((seq, GROUP_W), lambda b: (b, 0)),
        out_shape=jax.ShapeDtypeStruct((t, GROUP_W), F32),
        compiler_params=_cparams(("arbitrary",)),
        name="na_attention",
    )(qkv, qkv, qkv, bias)


CONV_PAD = 16
CONV_CHUNK = 256


def _conv_kernel(b_ref, c_ref, x_ref, a_ref, g_ref, scw_ref, cfw_ref, cfb_ref, lng_ref, lnb_ref,
                 ysc_ref, ycf_ref, psc, pcf):
    seq = b_ref.shape[0]
    n_chunks = seq // CONV_CHUNK
    zeros = jnp.zeros((CONV_PAD, GROUP_W), F32)
    psc[0:CONV_PAD, :] = zeros
    pcf[0:CONV_PAD, :] = zeros
    psc[CONV_PAD + seq:2 * CONV_PAD + seq, :] = zeros
    pcf[CONV_PAD + seq:2 * CONV_PAD + seq, :] = zeros

    def fill(i, carry):
        r0 = pl.multiple_of(i * CONV_CHUNK, CONV_CHUNK)
        rows = pl.ds(r0, CONV_CHUNK)
        dst = pl.ds(r0 + CONV_PAD, CONV_CHUNK)
        psc[dst, :] = c_ref[rows, :] * x_ref[rows, :]
        pcf[dst, :] = a_ref[rows, :] * jax.nn.sigmoid(g_ref[rows, :])
        return carry

    lax.fori_loop(0, n_chunks, fill, 0)

    def taps(win, w_ref):
        n_taps = w_ref.shape[0]
        offs = [CONV_PAD + k - n_taps // 2 for k in range(n_taps)]
        n_win = win.shape[0]
        acc = None
        for s in range(SUBLANES):
            ks = [k for k in range(n_taps) if offs[k] % SUBLANES == s]
            if not ks:
                continue
            shifted = pltpu.roll(win, n_win - s, axis=0) if s else win
            for k in ks:
                q = offs[k] - s
                term = shifted[q:q + CONV_CHUNK, :] * w_ref[k:k + 1, :]
                acc = term if acc is None else acc + term
        return acc

    def conv(i, carry):
        r0 = pl.multiple_of(i * CONV_CHUNK, CONV_CHUNK)
        rows = pl.ds(r0, CONV_CHUNK)
        window = pl.ds(r0, CONV_CHUNK + 2 * CONV_PAD)
        ysc_ref[rows, :] = b_ref[rows, :] * taps(psc[window, :], scw_ref)
        cf = taps(pcf[window, :], cfw_ref) + cfb_ref[...]
        mu = jnp.mean(cf, axis=-1, keepdims=True)
        xc = cf - mu
        var = jnp.mean(xc * xc, axis=-1, keepdims=True)
        ln = xc * lax.rsqrt(var + EPS) * lng_ref[...] + lnb_ref[...]
        ycf_ref[rows, :] = ln * jax.nn.sigmoid(ln)
        return carry

    lax.fori_loop(0, n_chunks, conv, 0)


def _conv_mixers(rest, sc_w, cf_w, cf_b, ln_g, ln_b, bsz, seq):
    t = rest.shape[0]
    blk = lambda j: pl.BlockSpec((seq, GROUP_W), lambda b, j=j: (b, j))
    full = lambda a: pl.BlockSpec(a.shape, lambda b: (0,) * a.ndim)
    cf_b, ln_g, ln_b = (a.reshape(1, GROUP_W) for a in (cf_b, ln_g, ln_b))
    out_spec = pl.BlockSpec((seq, GROUP_W), lambda b: (b, 0))
    return pl.pallas_call(
        _conv_kernel,
        grid=(bsz,),
        in_specs=[blk(0), blk(1), blk(2), blk(3), blk(4),
                  full(sc_w), full(cf_w), full(cf_b), full(ln_g), full(ln_b)],
        out_specs=[out_spec, out_spec],
        out_shape=[jax.ShapeDtypeStruct((t, GROUP_W), F32)] * 2,
        scratch_shapes=[pltpu.VMEM((seq + 2 * CONV_PAD, GROUP_W), F32)] * 2,
        compiler_params=_cparams(("arbitrary",)),
        name="conv_mixers",
    )(rest, rest, rest, rest, rest, sc_w, cf_w, cf_b, ln_g, ln_b)


S5_TL = 128
S5_LANE_SPLIT = 1
S5_ROW_BLOCK = 256


def _s5_discretise(a_re, a_im, log_dt, b_re, b_im, c_re, c_im, bsz):
    f32 = F32
    a_re, a_im, log_dt = a_re.astype(f32), a_im.astype(f32), log_dt.astype(f32)
    n_dir, n_grp, n_state = a_re.shape
    n_ch = b_re.shape[-1]
    dt = jnp.exp(log_dt)[..., None]
    mag = jnp.exp(a_re * dt)
    abr = mag * jnp.cos(a_im * dt)
    abi = mag * jnp.sin(a_im * dt)
    den = a_re * a_re + a_im * a_im
    fr = ((abr - 1.0) * a_re + abi * a_im) / den
    fi = (abi * a_re - (abr - 1.0) * a_im) / den
    bbr = fr[..., None] * b_re - fi[..., None] * b_im
    bbi = fr[..., None] * b_im + fi[..., None] * b_re
    eye = jnp.eye(n_grp, dtype=f32)
    blk_b = lambda m: jnp.einsum("dgps,gh->dgshp", m, eye).reshape(n_dir, n_grp * n_ch, n_grp * n_state)
    blk_c = lambda m: jnp.einsum("dgsp,gh->dgphs", m, eye).reshape(n_dir, n_grp * n_state, n_grp * n_ch)
    bmat = jnp.concatenate([blk_b(bbr), blk_b(bbi)], axis=-1)
    cmat = jnp.concatenate([blk_c(c_re.astype(f32)), blk_c(-c_im.astype(f32))], axis=1)
    amat = jnp.stack([abr.reshape(n_dir, -1), abi.reshape(n_dir, -1)], axis=1)
    amat = jnp.broadcast_to(amat[:, :, None, :], (n_dir, 2, bsz, n_grp * n_state))
    return bmat.astype(BF16), cmat.astype(BF16), amat


def _s5_kernel(*refs, bsz, backward):
    if backward:
        (u_ref, b_ref, c_ref, a_ref, yf_ref, d_ref, w_ref, bg_ref, o_ref, xs, st, tb) = refs
    else:
        (u_ref, b_ref, c_ref, a_ref, o_ref, xs, st, tb) = refs
    n_state = a_ref.shape[-1]

    @pl.when(pl.program_id(0) == 0)
    def _():
        st[...] = jnp.zeros_like(st)

    n_half = GROUP_W // LANES
    for b in range(bsz):
        for h in range(n_half):
            tb[h, pl.ds(b, S5_TL, stride=bsz), :] = u_ref[b, :, h * LANES:(h + 1) * LANES]
    for r0 in range(0, S5_TL * bsz, S5_ROW_BLOCK):
        rows = slice(r0, r0 + S5_ROW_BLOCK)
        u_tb = jnp.concatenate([tb[h, rows, :] for h in range(n_half)], axis=1)
        xs[rows, :] = jnp.dot(u_tb.astype(BF16), b_ref[...], preferred_element_type=F32)

    cw = n_state // S5_LANE_SPLIT
    for c in range(S5_LANE_SPLIT):
        re = slice(c * cw, (c + 1) * cw)
        im = slice(n_state + c * cw, n_state + (c + 1) * cw)
        a_r = a_ref[0, :, re]
        a_i = a_ref[1, :, re]

        def step(t, carry, re=re, im=im, a_r=a_r, a_i=a_i):
            s_r, s_i = carry
            tt = S5_TL - 1 - t if backward else t
            rows = pl.ds(pl.multiple_of(tt * bsz, bsz), bsz)
            n_r = a_r * s_r - a_i * s_i + xs[rows, re]
            n_i = a_r * s_i + a_i * s_r + xs[rows, im]
            xs[rows, re] = n_r
            xs[rows, im] = n_i
            return n_r, n_i

        s_r, s_i = lax.fori_loop(0, S5_TL, step, (st[:, re], st[:, im]), unroll=4)
        st[:, re] = s_r
        st[:, im] = s_i

    n_rows = S5_TL * bsz
    y_tb = jnp.concatenate(
        [jnp.dot(xs[r0:r0 + S5_ROW_BLOCK, :].astype(BF16), c_ref[...], preferred_element_type=F32)
         for r0 in range(0, n_rows, S5_ROW_BLOCK)], axis=0)
    if not backward:
        o_ref[...] = y_tb
        return
    y_sum = y_tb + yf_ref[...]
    for h in range(n_half):
        tb[h] = y_sum[:, h * LANES:(h + 1) * LANES]
    y_bt = jnp.concatenate(
        [jnp.concatenate([tb[h, pl.ds(b, S5_TL, stride=bsz), :] for h in range(n_half)], axis=1)
         for b in range(bsz)], axis=0)
    y = jax.nn.gelu(d_ref[...] * u_ref[...].reshape(S5_TL * bsz, GROUP_W) + y_bt)
    gate = jnp.dot(y.astype(BF16), w_ref[...], preferred_element_type=F32) + bg_ref[...]
    o_ref[...] = (y * jax.nn.sigmoid(gate)).reshape(bsz, S5_TL, GROUP_W)


def _s5_mixer(rest, ssm_col, bmat, cmat, amat, layer, d_skip, w_glu, b_glu, bsz, seq):
    n_chunks = seq // S5_TL
    n_rows = S5_TL * bsz
    n_st2 = bmat.shape[-1]
    rest3 = rest.reshape(bsz, seq, rest.shape[1])
    full = lambda a: pl.BlockSpec(a.shape, lambda i: (0,) * a.ndim)
    scratch = [pltpu.VMEM((n_rows, n_st2), F32),
               pltpu.VMEM((bsz, n_st2), F32),
               pltpu.VMEM((GROUP_W // LANES, n_rows, LANES), F32)]

    def direction(d, chunk_of):
        return [
            pl.BlockSpec((bsz, S5_TL, GROUP_W), lambda i: (0, chunk_of(i), ssm_col)),
            pl.BlockSpec((None, None, GROUP_W, n_st2), lambda i: (layer, d, 0, 0)),
            pl.BlockSpec((None, None, n_st2, GROUP_W), lambda i: (layer, d, 0, 0)),
            pl.BlockSpec((None, None, 2, bsz, n_st2 // 2), lambda i: (layer, d, 0, 0, 0)),
        ]

    y_fwd = pl.pallas_call(
        functools.partial(_s5_kernel, bsz=bsz, backward=False),
        grid=(n_chunks,),
        in_specs=direction(0, lambda i: i),
        out_specs=pl.BlockSpec((n_rows, GROUP_W), lambda i: (i, 0)),
        out_shape=jax.ShapeDtypeStruct((seq * bsz, GROUP_W), F32),
        scratch_shapes=scratch,
        compiler_params=_cparams(("arbitrary",)),
        name="s5_forward",
    )(rest3, bmat, cmat, amat)

    rev = lambda i: n_chunks - 1 - i
    d_skip, b_glu = d_skip.reshape(1, GROUP_W), b_glu.reshape(1, GROUP_W)
    out = pl.pallas_call(
        functools.partial(_s5_kernel, bsz=bsz, backward=True),
        grid=(n_chunks,),
        in_specs=direction(1, rev) + [
            pl.BlockSpec((n_rows, GROUP_W), lambda i: (rev(i), 0)),
            full(d_skip), full(w_glu), full(b_glu),
        ],
        out_specs=pl.BlockSpec((bsz, S5_TL, GROUP_W), lambda i: (0, rev(i), 0)),
        out_shape=jax.ShapeDtypeStruct((bsz, seq, GROUP_W), F32),
        scratch_shapes=scratch,
        compiler_params=_cparams(("arbitrary",)),
        name="s5_backward",
    )(rest3, bmat, cmat, amat, y_fwd, d_skip, w_glu, b_glu)
    return out.reshape(bsz * seq, GROUP_W)


FFN_CHUNK = 256
ROUTE_COLS = 8
ROUTER_LANES = 128
ROUTER_ROW_BLOCK = 128


def _route_records(h, wr_ref):
    h_hi = h.astype(BF16)
    h_lo = (h - h_hi.astype(F32)).astype(BF16)
    pieces = jnp.concatenate([h_hi, h_lo, h_hi], axis=1)
    logits = jnp.concatenate(
        [jnp.dot(pieces[r0:r0 + ROUTER_ROW_BLOCK], wr_ref[...], preferred_element_type=F32)
         for r0 in range(0, h.shape[0], ROUTER_ROW_BLOCK)], axis=0)
    lane = lax.broadcasted_iota(jnp.int32, logits.shape, 1)
    minus_inf = -jnp.inf
    l1 = jnp.where(lane < N_EXPERTS, logits, minus_inf)
    m1 = jnp.max(l1, axis=-1, keepdims=True)
    i1 = jnp.min(jnp.where(l1 == m1, lane, ROUTER_LANES), axis=-1, keepdims=True)
    l2 = jnp.where(lane == i1, minus_inf, l1)
    m2 = jnp.max(l2, axis=-1, keepdims=True)
    i2 = jnp.min(jnp.where(l2 == m2, lane, ROUTER_LANES), axis=-1, keepdims=True)
    e2 = jnp.exp(m2 - m1)
    den = 1.0 + e2
    rec = jnp.where(lane == 0, i1.astype(F32),
                    jnp.where(lane == 1, i2.astype(F32),
                              jnp.where(lane == 2, 1.0 / den,
                                        jnp.where(lane == 3, e2 / den, 0.0))))
    return rec[:, :ROUTE_COLS]


def _mixer_out_kernel(*refs, tail):
    y_refs, (gn_ref, w_ref, x_ref, g2_ref) = refs[:4], refs[4:8]
    acc = x_ref[...]
    for j, y_ref in enumerate(y_refs):
        yn = _rms(y_ref[...], gn_ref[j:j + 1, :]).astype(BF16)
        acc = acc + jnp.dot(yn, w_ref[j * GROUP_W:(j + 1) * GROUP_W, :], preferred_element_type=F32)
    h = _rms(acc, g2_ref[...])
    if tail == "route":
        wr_ref, o_ref, route_ref = refs[8:]
        _store_token_tiles(o_ref, acc)
        route_ref[...] = _route_records(h, wr_ref)
        return
    wg_hbm, wu_hbm, wd_hbm, o_ref, wg_ref, wu_ref, wd_ref, sem = refs[8:]

    @pl.when(pl.program_id(0) == 0)
    def _():
        copies = [pltpu.make_async_copy(wd_hbm, wd_ref, sem.at[2])]
        for k, (src, dst) in enumerate(((wg_hbm, wg_ref), (wu_hbm, wu_ref))):
            copies += [pltpu.make_async_copy(src.at[:, pl.ds(c * FFN_CHUNK, FFN_CHUNK)], dst.at[c], sem.at[k])
                       for c in range(dst.shape[0])]
        for c in copies:
            c.start()
        for c in copies:
            c.wait()

    h = h.astype(BF16)
    acts = []
    for c in range(wg_ref.shape[0]):
        gate = jnp.dot(h, wg_ref[c], preferred_element_type=F32)
        up = jnp.dot(h, wu_ref[c], preferred_element_type=F32)
        acts.append((gate * jax.nn.sigmoid(gate) * up).astype(BF16))
    o_ref[...] = acc + jnp.dot(jnp.concatenate(acts, axis=1), wd_ref[...], preferred_element_type=F32)


def _mixer_out(ys, gn, w_out, x2, g2, tail, tail_weights, tm=512):
    t, d = x2.shape
    row = lambda n: pl.BlockSpec((tm, n), lambda i: (i, 0))
    whole = lambda a: pl.BlockSpec(a.shape, lambda i: (0, 0))
    if tail == "route":
        wr = jnp.zeros((d, ROUTER_LANES), F32).at[:, :N_EXPERTS].set(tail_weights[0].astype(F32))
        wr_hi = wr.astype(BF16)
        wr_lo = (wr - wr_hi.astype(F32)).astype(BF16)
        tail_weights = (jnp.concatenate([wr_hi, wr_hi, wr_lo], axis=0),)
        tail_specs = [whole(w) for w in tail_weights]
        out_specs = [pl.BlockSpec((tm * SUBLANES, LANES), lambda i: (i, 0)), row(ROUTE_COLS)]
        out_shape = [jax.ShapeDtypeStruct((t * SUBLANES, LANES), F32), jax.ShapeDtypeStruct((t, ROUTE_COLS), F32)]
        scratch = []
    else:
        w_gate, w_up, w_down = tail_weights
        chunk_major = pltpu.VMEM((w_gate.shape[1] // FFN_CHUNK, d, FFN_CHUNK), w_gate.dtype)
        tail_specs = [pl.BlockSpec(memory_space=pl.ANY)] * 3
        out_specs, out_shape = row(d), jax.ShapeDtypeStruct((t, d), F32)
        scratch = [chunk_major, chunk_major, pltpu.VMEM(w_down.shape, w_down.dtype), pltpu.SemaphoreType.DMA((3,))]
    return pl.pallas_call(
        functools.partial(_mixer_out_kernel, tail=tail),
        grid=(t // tm,),
        in_specs=[row(GROUP_W)] * 4 + [whole(gn), whole(w_out), row(d), whole(g2)] + tail_specs,
        out_specs=out_specs,
        out_shape=out_shape,
        scratch_shapes=scratch,
        compiler_params=_cparams(("arbitrary",)),
        name="mixer_out_" + tail,
    )(*ys, gn, w_out, x2, g2, *tail_weights)


def _moe_plan(route, tm, n_tiles):
    ids = route[:, :2]
    experts = jnp.where((ids >= 0) & (ids <= N_EXPERTS - 1), ids, 0).astype(jnp.int32).reshape(-1)
    onehot = (experts[:, None] == jnp.arange(N_EXPERTS, dtype=jnp.int32)[None, :]).astype(jnp.int32)
    csum = jnp.cumsum(onehot, axis=0)
    rank = jnp.sum((csum - onehot) * onehot, axis=1)
    count = csum[-1]
    padded = ((count + tm - 1) // tm) * tm
    end = jnp.cumsum(padded)
    start = end - padded
    slot = jnp.sum(onehot * start[None, :], axis=1) + rank
    n_active = end[-1] // tm
    tile_start = jnp.arange(n_tiles, dtype=jnp.int32) * tm
    tile_expert = jnp.sum((tile_start[:, None] >= end[None, :]).astype(jnp.int32), axis=1)
    tile_expert = jnp.minimum(tile_expert, N_EXPERTS - 1)
    last = tile_expert[jnp.maximum(n_active - 1, 0)]
    tile_expert = jnp.where(jnp.arange(n_tiles) < n_active, tile_expert, last)
    return slot.astype(jnp.int32), tile_expert.astype(jnp.int32), n_active.astype(jnp.int32).reshape(1)


def _token_copy(src, src_tok, dst, dst_tok, sem):
    rows = lambda tok: pl.ds(pl.multiple_of(tok * SUBLANES, SUBLANES), SUBLANES)
    return pltpu.make_async_copy(src.at[rows(src_tok)], dst.at[rows(dst_tok)], sem)


def _invert_kernel(slot_ref, word_ref, pad_hbm, plan_ref, sem):
    @pl.when(pl.program_id(0) == 0)
    def _():
        init = pltpu.make_async_copy(pad_hbm, plan_ref, sem)
        init.start()
        init.wait()

    def place(a, carry):
        plan_ref[slot_ref[0, 0, a]] = word_ref[0, 0, a]
        return carry

    lax.fori_loop(0, slot_ref.shape[-1], place, 0, unroll=8)


def _plan_bits(n_tok, tm):
    src_bits = (n_tok - 1).bit_length()
    assert (2 * n_tok + 2 * tm) << src_bits < 2 ** 31 and tm & (tm - 1) == 0
    return src_bits


def _invert(slot, n_slots, tm, n_tok, chunk=4096):
    src_bits = _plan_bits(n_tok, tm)
    a = jnp.arange(slot.shape[0], dtype=jnp.int32)
    tok = a >> 1
    word = tok + (((a & 1) * n_tok + tok) << src_bits)
    s = jnp.arange(n_slots, dtype=jnp.int32)
    pad_plan = (2 * n_tok + (s & (2 * tm - 1))) << src_bits
    n_chunks = slot.shape[0] // chunk
    chunked = pl.BlockSpec((1, 1, chunk), lambda c: (c, 0, 0), memory_space=pltpu.SMEM)
    return pl.pallas_call(
        _invert_kernel,
        grid=(n_chunks,),
        in_specs=[chunked, chunked, pl.BlockSpec(memory_space=pl.ANY)],
        out_specs=pl.BlockSpec((n_slots,), lambda c: (0,), memory_space=pltpu.SMEM),
        out_shape=jax.ShapeDtypeStruct((n_slots,), jnp.int32),
        scratch_shapes=[pltpu.SemaphoreType.DMA(())],
        compiler_params=pltpu.CompilerParams(dimension_semantics=("arbitrary",)),
        name="moe_invert",
    )(slot.reshape(n_chunks, 1, chunk), word.reshape(n_chunks, 1, chunk), pad_plan)


def _tokens_done(ref, n_tok, sem):
    rows = pl.ds(0, n_tok * SUBLANES)
    pltpu.make_async_copy(ref.at[rows], ref.at[rows], sem).wait()


def _moe_ffn_kernel(te_ref, nact_ref, plan_ref, x_hbm, g_ref, wg_ref, wu_ref, wd_ref, yk_hbm,
                    xbuf, xb16, ybuf, acc_ref, sem_in, sem_out, *, tm, n_tok, n_f, pad_tile, src_bits):
    del te_ref
    i = pl.program_id(0)
    f = pl.program_id(1)
    par = i % 2
    nact = nact_ref[0]

    def fetch(tile, p, j):
        src = plan_ref[tile * tm + j] & ((1 << src_bits) - 1)
        _token_copy(x_hbm, src, xbuf.at[p], j, sem_in.at[p]).start(priority=1)

    def send(tile, p, j):
        dst = plan_ref[tile * tm + j] >> src_bits
        _token_copy(ybuf.at[p], j, yk_hbm, dst, sem_out.at[p]).start()

    @pl.when((i == 0) & (f == 0))
    def _():
        def first(j, carry):
            fetch(0, 0, j)
            return carry
        lax.fori_loop(0, tm, first, 0)
        ybuf[...] = jnp.zeros_like(ybuf)
        for p in range(2):
            spare_rows = pl.ds((2 * n_tok + p * tm) * SUBLANES, tm * SUBLANES)
            spare = pltpu.make_async_copy(ybuf.at[p], yk_hbm.at[spare_rows], sem_out.at[p])
            spare.start()
            spare.wait()
        _tokens_done(xbuf.at[0], tm, sem_in.at[0])

    @pl.when((i >= 1) & (i <= nact) & (f == 0))
    def _():
        _tokens_done(xbuf.at[par], tm, sem_in.at[par])

    @pl.when(i < nact)
    def _():
        @pl.when(f == 0)
        def _():
            xb16[...] = _rms(_load_token_tiles(xbuf.at[par], tm), g_ref[...]).astype(BF16)
            acc_ref[...] = jnp.zeros_like(acc_ref)

            for j in range(tm):
                fetch(i + 1, 1 - par, j)

        @pl.when(f == n_f - 1)
        def _():
            prev = jnp.where(i > 0, i - 1, pad_tile)

            for j in range(tm):
                send(prev, 1 - par, j)

        x = xb16[...]
        gate = jnp.dot(x, wg_ref[...], preferred_element_type=F32)
        up = jnp.dot(x, wu_ref[...], preferred_element_type=F32)
        act = (gate * jax.nn.sigmoid(gate) * up).astype(BF16)
        acc_ref[...] += jnp.dot(act, wd_ref[...], preferred_element_type=F32)

        @pl.when(f == n_f - 1)
        def _():
            @pl.when(i >= 1)
            def _():
                _tokens_done(ybuf.at[par], tm, sem_out.at[par])

            _store_token_tiles(ybuf.at[par], acc_ref[...])

    @pl.when((i == nact) & (f == 0))
    def _():
        def last(j, carry):
            send(i - 1, 1 - par, j)
            return carry
        lax.fori_loop(0, tm, last, 0)
        _tokens_done(ybuf.at[0], tm, sem_out.at[0])
        _tokens_done(ybuf.at[1], tm, sem_out.at[1])


def _moe_ffn(xt, g, plan, tile_expert, n_active, wg, wu, wd, tm):
    n_tok = xt.shape[0] // SUBLANES
    d = SUBLANES * LANES
    n_tiles = tile_expert.shape[0]
    pad_tile = n_tiles - 1 if (n_tiles - 1) % 2 else n_tiles
    assert plan.shape[0] >= (pad_tile + 1) * tm
    dff = wg.shape[-1]
    n_f = 2
    tf = dff // n_f
    fidx = lambda i, f, na: jnp.where(i < na[0], f, n_f - 1)
    grid_spec = pltpu.PrefetchScalarGridSpec(
        num_scalar_prefetch=3,
        grid=(n_tiles, n_f),
        in_specs=[
            pl.BlockSpec(memory_space=pl.ANY),
            pl.BlockSpec((1, d), lambda i, f, te, na, plan: (0, 0)),
            pl.BlockSpec((None, d, tf), lambda i, f, te, na, plan: (te[i], 0, fidx(i, f, na))),
            pl.BlockSpec((None, d, tf), lambda i, f, te, na, plan: (te[i], 0, fidx(i, f, na))),
            pl.BlockSpec((None, tf, d), lambda i, f, te, na, plan: (te[i], fidx(i, f, na), 0)),
        ],
        out_specs=pl.BlockSpec(memory_space=pl.ANY),
        scratch_shapes=[
            pltpu.VMEM((2, tm * SUBLANES, LANES), F32),
            pltpu.VMEM((tm, d), BF16),
            pltpu.VMEM((2, tm * SUBLANES, LANES), F32),
            pltpu.VMEM((tm, d), F32),
            pltpu.SemaphoreType.DMA((2,)),
            pltpu.SemaphoreType.DMA((2,)),
        ],
    )
    return pl.pallas_call(
        functools.partial(_moe_ffn_kernel, tm=tm, n_tok=n_tok, n_f=n_f, pad_tile=pad_tile,
                          src_bits=_plan_bits(n_tok, tm)),
        grid_spec=grid_spec,
        out_shape=jax.ShapeDtypeStruct(((2 * n_tok + 2 * tm) * SUBLANES, LANES), F32),
        compiler_params=_cparams(("arbitrary", "arbitrary")),
        name="moe_ffn",
    )(tile_expert, n_active, plan, xt, g.reshape(1, d), wg, wu, wd)


def _combine_kernel(route_ref, x_ref, y0_ref, y1_ref, fg_ref, o_ref, *, final_norm):
    r = route_ref[...]
    tm = r.shape[0]
    y = (_load_token_tiles(x_ref, tm) + r[:, 2:3] * _load_token_tiles(y0_ref, tm)
         + r[:, 3:4] * _load_token_tiles(y1_ref, tm))
    o_ref[...] = _rms(y, fg_ref[...]) if final_norm else y


def _combine(yk, route, xt, final_g, final_norm, tm=512):
    t = xt.shape[0] // SUBLANES
    d = SUBLANES * LANES
    n_blk = t // tm
    tiles = lambda f: pl.BlockSpec((tm * SUBLANES, LANES), f)
    return pl.pallas_call(
        functools.partial(_combine_kernel, final_norm=final_norm),
        grid=(n_blk,),
        in_specs=[
            pl.BlockSpec((tm, ROUTE_COLS), lambda i: (i, 0)),
            tiles(lambda i: (i, 0)),
            tiles(lambda i: (i, 0)),
            tiles(lambda i: (n_blk + i, 0)),
            pl.BlockSpec((1, d), lambda i: (0, 0)),
        ],
        out_specs=pl.BlockSpec((tm, d), lambda i: (i, 0)),
        out_shape=jax.ShapeDtypeStruct((t, d), F32),
        compiler_params=_cparams(("arbitrary",)),
        name="moe_combine",
    )(route, xt, yk, yk, final_g.reshape(1, d))


def _final_norm_kernel(x_ref, g_ref, o_ref):
    o_ref[...] = _rms(x_ref[...], g_ref[...])


def _final_norm(x2, g, tm=512):
    t, d = x2.shape
    return pl.pallas_call(
        _final_norm_kernel,
        grid=(t // tm,),
        in_specs=[pl.BlockSpec((tm, d), lambda i: (i, 0)), pl.BlockSpec((1, d), lambda i: (0, 0))],
        out_specs=pl.BlockSpec((tm, d), lambda i: (i, 0)),
        out_shape=jax.ShapeDtypeStruct((t, d), F32),
        compiler_params=_cparams(("arbitrary",)),
        name="final_norm",
    )(x2, g.reshape(1, d))


MOE_TM = 512


def kernel(x, norm1_g, w_in, na_rpb, sc_conv_w, cf_conv_w, cf_conv_b, cf_ln_g, cf_ln_b, ssm_a_re, ssm_a_im,
           ssm_log_dt, ssm_b_re, ssm_b_im, ssm_c_re, ssm_c_im, ssm_d, ssm_w_glu, ssm_b_glu, grp_norm_g, w_out,
           norm2_g, ffn_w_gate, ffn_w_up, ffn_w_down, moe_w_router, moe_w_gate, moe_w_up, moe_w_down,
           final_norm_g):
    bsz, seq, d = x.shape
    depth = w_in.shape[0]
    t = bsz * seq
    rows = seq // GRID_W
    x2 = x.reshape(t, d).astype(F32)
    final_done = False
    na_bias = jax.vmap(lambda rpb: _na_bias_table(rpb, rows))(na_rpb)
    s5_mats = jax.vmap(lambda *p: _s5_discretise(*p, bsz))(ssm_a_re, ssm_a_im, ssm_log_dt, ssm_b_re, ssm_b_im,
                                                           ssm_c_re, ssm_c_im)
    for i in range(depth):
        qkv, rest = _in_proj(x2, norm1_g[i], w_in[i].astype(BF16))
        y_na = _na_attention(qkv, na_bias, i, bsz, seq)
        y_sc, y_cf = _conv_mixers(rest, sc_conv_w[i], cf_conv_w[i], cf_conv_b[i], cf_ln_g[i], cf_ln_b[i],
                                  bsz, seq)
        y_ssm = _s5_mixer(rest, SSM_COL, *s5_mats, i, ssm_d[i], ssm_w_glu[i].astype(BF16), ssm_b_glu[i],
                          bsz, seq)
        ys = (y_na, y_sc, y_cf, y_ssm)
        gn, g2 = grp_norm_g[i].reshape(4, GROUP_W), norm2_g[i].reshape(1, d)
        j = i // 2
        if i % 2 == 0:
            ffn_w = (ffn_w_gate[j].astype(BF16), ffn_w_up[j].astype(BF16), ffn_w_down[j].astype(BF16))
            x2 = _mixer_out(ys, gn, w_out[i].astype(BF16), x2, g2, "ffn", ffn_w)
        else:
            x2, route = _mixer_out(ys, gn, w_out[i].astype(BF16), x2, g2, "route", (moe_w_router[j],))
            n_tiles = (2 * t) // MOE_TM + N_EXPERTS + 1
            slot, tile_expert, n_active = _moe_plan(route, MOE_TM, n_tiles)
            plan = _invert(slot, (n_tiles + 1) * MOE_TM, MOE_TM, t)
            yk = _moe_ffn(x2, norm2_g[i], plan, tile_expert, n_active, moe_w_gate[j].astype(BF16),
                          moe_w_up[j].astype(BF16), moe_w_down[j].astype(BF16), MOE_TM)
            final_done = i == depth - 1
            x2 = _combine(yk, route, x2, final_norm_g, final_done)
    if not final_done:
        x2 = _final_norm(x2, final_norm_g)
    return x2.reshape(bsz, seq, d)
```

```python
import functools
import math

import jax
import jax.numpy as jnp
import numpy as np
from jax import lax
from jax.experimental import pallas as pl
from jax.experimental.pallas import tpu as pltpu

F32 = jnp.float32
BF16 = jnp.bfloat16
EPS = 1e-6
NEG_INF = -1e30

GRID_W = 64
NA_HEAD_DIM = 64
NA_WIN_H = 8
NA_WIN_W = 16
GROUP_W = 256
SSM_GROUP_CH = 16
SSM_STATE = 64
N_EXPERTS = 8
VMEM_LIMIT = 56 * 1024 * 1024


def _cparams(sem, vmem=VMEM_LIMIT):
    return pltpu.CompilerParams(dimension_semantics=sem, vmem_limit_bytes=vmem)


def _rms(x, g):
    return x * lax.rsqrt(jnp.mean(x * x, axis=-1, keepdims=True) + EPS) * g


SUBLANES = 8
LANES = 128


def _load_token_tiles(ref, n):
    return jnp.concatenate([ref[pl.ds(s, n, stride=SUBLANES), :] for s in range(SUBLANES)], axis=1)


def _store_token_tiles(ref, val):
    for s in range(SUBLANES):
        ref[pl.ds(s, val.shape[0], stride=SUBLANES), :] = val[:, s * LANES:(s + 1) * LANES]


N_QKV_COLS = 3 * GROUP_W
SSM_COL = 5


def _in_proj_kernel(x_ref, g_ref, w_ref, qkv_ref, rest_ref):
    h = _rms(x_ref[...], g_ref[...]).astype(BF16)
    n_qkv = qkv_ref.shape[1] // GROUP_W
    n_rest = rest_ref.shape[1] // GROUP_W
    for j in range(n_qkv + n_rest):
        z = jnp.dot(h, w_ref[:, j * GROUP_W:(j + 1) * GROUP_W], preferred_element_type=F32)
        if j < n_qkv:
            qkv_ref[:, j * GROUP_W:(j + 1) * GROUP_W] = z.astype(BF16)
        else:
            k = j - n_qkv
            rest_ref[:, k * GROUP_W:(k + 1) * GROUP_W] = z


def _in_proj(x2, g, w, tm=512):
    t, d = x2.shape
    n_cols = w.shape[1]
    n_rest = n_cols - N_QKV_COLS
    return pl.pallas_call(
        _in_proj_kernel,
        grid=(t // tm,),
        in_specs=[
            pl.BlockSpec((tm, d), lambda i: (i, 0)),
            pl.BlockSpec((1, d), lambda i: (0, 0)),
            pl.BlockSpec((d, n_cols), lambda i: (0, 0)),
        ],
        out_specs=[
            pl.BlockSpec((tm, N_QKV_COLS), lambda i: (i, 0)),
            pl.BlockSpec((tm, n_rest), lambda i: (i, 0)),
        ],
        out_shape=[
            jax.ShapeDtypeStruct((t, N_QKV_COLS), BF16),
            jax.ShapeDtypeStruct((t, n_rest), F32),
        ],
        compiler_params=_cparams(("arbitrary",)),
        name="in_proj",
    )(x2, g.reshape(1, d), w)


def _na_bias_table(rpb, rows):
    n_heads = rpb.shape[0]
    kh = NA_WIN_H
    c = np.arange(GRID_W)
    qcs = np.clip(c - NA_WIN_W // 2, 0, GRID_W - NA_WIN_W)
    kc = np.arange(GRID_W)
    in_win = (kc[None, :] >= qcs[:, None]) & (kc[None, :] < qcs[:, None] + NA_WIN_W)
    dc_idx = np.clip(kc[None, :] - c[:, None] + NA_WIN_W - 1, 0, 2 * NA_WIN_W - 2)
    onehot_dc = (dc_idx[:, :, None] == np.arange(2 * NA_WIN_W - 1)).astype(np.float32)
    rep_rows = np.array([0, 1, 2, 3, 4, rows - 3, rows - 2, rows - 1])
    row_start = np.clip(rep_rows - kh // 2, 0, rows - kh)
    dr_idx = (row_start - rep_rows + NA_WIN_H - 1)[:, None] + np.arange(kh)[None, :]
    onehot_dr = (dr_idx[:, :, None] == np.arange(2 * NA_WIN_H - 1)).astype(np.float32)
    tab = jnp.einsum("via,hab,ckb->vhcik", jnp.asarray(onehot_dr), rpb.astype(F32), jnp.asarray(onehot_dc),
                     precision=lax.Precision.HIGHEST)
    tab = jnp.where(jnp.asarray(in_win)[None, None, :, None, :], tab, NEG_INF)
    return tab.reshape(len(rep_rows), n_heads * GRID_W, kh * GRID_W)


def _na_kernel(q_ref, k_ref, v_ref, bias_ref, o_ref, *, rows, n_heads):
    kh = NA_WIN_H
    lane = lax.broadcasted_iota(jnp.int32, (GRID_W, n_heads * NA_HEAD_DIM), 1)
    masks = [(lane >= h * NA_HEAD_DIM) & (lane < (h + 1) * NA_HEAD_DIM) for h in range(n_heads)]
    scale = NA_HEAD_DIM ** -0.5

    def body(r, carry):
        q0 = pl.multiple_of(r * GRID_W, GRID_W)
        q = q_ref[pl.ds(q0, GRID_W), :].astype(F32) * scale
        qs = jnp.concatenate([jnp.where(m, q, 0.0) for m in masks], axis=0).astype(BF16)
        rs = jnp.clip(r - kh // 2, 0, rows - kh)
        k0 = pl.multiple_of(rs * GRID_W, GRID_W)
        ks = k_ref[pl.ds(k0, kh * GRID_W), :]
        vs = v_ref[pl.ds(k0, kh * GRID_W), :]
        s = lax.dot_general(qs, ks, (((1,), (1,)), ((), ())), preferred_element_type=F32)
        var = jnp.where(r < kh // 2, r, jnp.where(r > rows - kh // 2, r - (rows - 8), kh // 2))
        s = s + bias_ref[var]
        m = jnp.max(s, axis=-1, keepdims=True)
        p = jnp.exp(s - m)
        den = jnp.sum(p, axis=-1, keepdims=True)
        o = jnp.dot(p.astype(BF16), vs, preferred_element_type=F32) / den
        out = jnp.where(masks[0], o[0:GRID_W], 0.0)
        for h in range(1, n_heads):
            out = out + jnp.where(masks[h], o[h * GRID_W:(h + 1) * GRID_W], 0.0)
        o_ref[pl.ds(q0, GRID_W), :] = out
        return carry

    lax.fori_loop(0, rows, body, 0, unroll=8)


def _na_attention(qkv, bias, layer, bsz, seq):
    rows = seq // GRID_W
    n_heads = GROUP_W // NA_HEAD_DIM
    t = qkv.shape[0]
    blk = lambda j: pl.BlockSpec((seq, GROUP_W), lambda b, j=j: (b, j))
    return pl.pallas_call(
        functools.partial(_na_kernel, rows=rows, n_heads=n_heads),
        grid=(bsz,),
        in_specs=[blk(0), blk(1), blk(2),
                  pl.BlockSpec((None,) + bias.shape[1:], lambda b: (layer, 0, 0, 0))],
        out_specs=pl.BlockSpec((seq, GROUP_W), lambda b: (b, 0)),
        out_shape=jax.ShapeDtypeStruct((t, GROUP_W), F32),
        compiler_params=_cparams(("arbitrary",)),
        name="na_attention",
    )(qkv, qkv, qkv, bias)


CONV_PAD = 16
CONV_CHUNK = 256


def _conv_kernel(b_ref, c_ref, x_ref, a_ref, g_ref, scw_ref, cfw_ref, cfb_ref, lng_ref, lnb_ref,
                 ysc_ref, ycf_ref, psc, pcf):
    seq = b_ref.shape[0]
    n_chunks = seq // CONV_CHUNK
    zeros = jnp.zeros((CONV_PAD, GROUP_W), F32)
    psc[0:CONV_PAD, :] = zeros
    pcf[0:CONV_PAD, :] = zeros
    psc[CONV_PAD + seq:2 * CONV_PAD + seq, :] = zeros
    pcf[CONV_PAD + seq:2 * CONV_PAD + seq, :] = zeros

    def fill(i, carry):
        r0 = pl.multiple_of(i * CONV_CHUNK, CONV_CHUNK)
        rows = pl.ds(r0, CONV_CHUNK)
        dst = pl.ds(r0 + CONV_PAD, CONV_CHUNK)
        psc[dst, :] = c_ref[rows, :] * x_ref[rows, :]
        pcf[dst, :] = a_ref[rows, :] * jax.nn.sigmoid(g_ref[rows, :])
        return carry

    lax.fori_loop(0, n_chunks, fill, 0)

    def taps(win, w_ref):
        n_taps = w_ref.shape[0]
        offs = [CONV_PAD + k - n_taps // 2 for k in range(n_taps)]
        n_win = win.shape[0]
        acc = None
        for s in range(SUBLANES):
            ks = [k for k in range(n_taps) if offs[k] % SUBLANES == s]
            if not ks:
                continue
            shifted = pltpu.roll(win, n_win - s, axis=0) if s else win
            for k in ks:
                q = offs[k] - s
                term = shifted[q:q + CONV_CHUNK, :] * w_ref[k:k + 1, :]
                acc = term if acc is None else acc + term
        return acc

    def conv(i, carry):
        r0 = pl.multiple_of(i * CONV_CHUNK, CONV_CHUNK)
        rows = pl.ds(r0, CONV_CHUNK)
        window = pl.ds(r0, CONV_CHUNK + 2 * CONV_PAD)
        ysc_ref[rows, :] = b_ref[rows, :] * taps(psc[window, :], scw_ref)
        cf = taps(pcf[window, :], cfw_ref) + cfb_ref[...]
        mu = jnp.mean(cf, axis=-1, keepdims=True)
        xc = cf - mu
        var = jnp.mean(xc * xc, axis=-1, keepdims=True)
        ln = xc * lax.rsqrt(var + EPS) * lng_ref[...] + lnb_ref[...]
        ycf_ref[rows, :] = ln * jax.nn.sigmoid(ln)
        return carry

    lax.fori_loop(0, n_chunks, conv, 0)


def _conv_mixers(rest, sc_w, cf_w, cf_b, ln_g, ln_b, bsz, seq):
    t = rest.shape[0]
    blk = lambda j: pl.BlockSpec((seq, GROUP_W), lambda b, j=j: (b, j))
    full = lambda a: pl.BlockSpec(a.shape, lambda b: (0,) * a.ndim)
    cf_b, ln_g, ln_b = (a.reshape(1, GROUP_W) for a in (cf_b, ln_g, ln_b))
    out_spec = pl.BlockSpec((seq, GROUP_W), lambda b: (b, 0))
    return pl.pallas_call(
        _conv_kernel,
        grid=(bsz,),
        in_specs=[blk(0), blk(1), blk(2), blk(3), blk(4),
                  full(sc_w), full(cf_w), full(cf_b), full(ln_g), full(ln_b)],
        out_specs=[out_spec, out_spec],
        out_shape=[jax.ShapeDtypeStruct((t, GROUP_W), F32)] * 2,
        scratch_shapes=[pltpu.VMEM((seq + 2 * CONV_PAD, GROUP_W), F32)] * 2,
        compiler_params=_cparams(("arbitrary",)),
        name="conv_mixers",
    )(rest, rest, rest, rest, rest, sc_w, cf_w, cf_b, ln_g, ln_b)


S5_TL = 256
S5_LANE_SPLIT = 1
S5_ROW_BLOCK = 256


def _s5_discretise(a_re, a_im, log_dt, b_re, b_im, c_re, c_im, bsz):
    f32 = F32
    a_re, a_im, log_dt = a_re.astype(f32), a_im.astype(f32), log_dt.astype(f32)
    n_dir, n_grp, n_state = a_re.shape
    n_ch = b_re.shape[-1]
    dt = jnp.exp(log_dt)[..., None]
    mag = jnp.exp(a_re * dt)
    abr = mag * jnp.cos(a_im * dt)
    abi = mag * jnp.sin(a_im * dt)
    den = a_re * a_re + a_im * a_im
    fr = ((abr - 1.0) * a_re + abi * a_im) / den
    fi = (abi * a_re - (abr - 1.0) * a_im) / den
    bbr = fr[..., None] * b_re - fi[..., None] * b_im
    bbi = fr[..., None] * b_im + fi[..., None] * b_re
    eye = jnp.eye(n_grp, dtype=f32)
    blk_b = lambda m: jnp.einsum("dgps,gh->dgshp", m, eye).reshape(n_dir, n_grp * n_ch, n_grp * n_state)
    blk_c = lambda m: jnp.einsum("dgsp,gh->dgphs", m, eye).reshape(n_dir, n_grp * n_state, n_grp * n_ch)
    bmat = jnp.concatenate([blk_b(bbr), blk_b(bbi)], axis=-1)
    cmat = jnp.concatenate([blk_c(c_re.astype(f32)), blk_c(-c_im.astype(f32))], axis=1)
    amat = jnp.stack([abr.reshape(n_dir, -1), abi.reshape(n_dir, -1)], axis=1)
    amat = jnp.broadcast_to(amat[:, :, None, :], (n_dir, 2, bsz, n_grp * n_state))
    return bmat.astype(BF16), cmat.astype(BF16), amat


def _s5_kernel(*refs, bsz, backward):
    if backward:
        (u_ref, b_ref, c_ref, a_ref, yf_ref, d_ref, w_ref, bg_ref, o_ref, xs, st, tb) = refs
    else:
        (u_ref, b_ref, c_ref, a_ref, o_ref, xs, st, tb) = refs
    n_state = a_ref.shape[-1]

    @pl.when(pl.program_id(0) == 0)
    def _():
        st[...] = jnp.zeros_like(st)

    n_half = GROUP_W // LANES
    for b in range(bsz):
        for h in range(n_half):
            tb[h, pl.ds(b, S5_TL, stride=bsz), :] = u_ref[b, :, h * LANES:(h + 1) * LANES]
    for r0 in range(0, S5_TL * bsz, S5_ROW_BLOCK):
        rows = slice(r0, r0 + S5_ROW_BLOCK)
        u_tb = jnp.concatenate([tb[h, rows, :] for h in range(n_half)], axis=1)
        xs[rows, :] = jnp.dot(u_tb.astype(BF16), b_ref[...], preferred_element_type=F32)

    cw = n_state // S5_LANE_SPLIT
    for c in range(S5_LANE_SPLIT):
        re = slice(c * cw, (c + 1) * cw)
        im = slice(n_state + c * cw, n_state + (c + 1) * cw)
        a_r = a_ref[0, :, re]
        a_i = a_ref[1, :, re]

        def step(t, carry, re=re, im=im, a_r=a_r, a_i=a_i):
            s_r, s_i = carry
            tt = S5_TL - 1 - t if backward else t
            rows = pl.ds(pl.multiple_of(tt * bsz, bsz), bsz)
            n_r = a_r * s_r - a_i * s_i + xs[rows, re]
            n_i = a_r * s_i + a_i * s_r + xs[rows, im]
            xs[rows, re] = n_r
            xs[rows, im] = n_i
            return n_r, n_i

        s_r, s_i = lax.fori_loop(0, S5_TL, step, (st[:, re], st[:, im]), unroll=4)
        st[:, re] = s_r
        st[:, im] = s_i

    n_rows = S5_TL * bsz
    y_tb = jnp.concatenate(
        [jnp.dot(xs[r0:r0 + S5_ROW_BLOCK, :].astype(BF16), c_ref[...], preferred_element_type=F32)
         for r0 in range(0, n_rows, S5_ROW_BLOCK)], axis=0)
    if not backward:
        o_ref[...] = y_tb
        return
    y_sum = y_tb + yf_ref[...]
    for h in range(n_half):
        tb[h] = y_sum[:, h * LANES:(h + 1) * LANES]
    y_bt = jnp.concatenate(
        [jnp.concatenate([tb[h, pl.ds(b, S5_TL, stride=bsz), :] for h in range(n_half)], axis=1)
         for b in range(bsz)], axis=0)
    y = jax.nn.gelu(d_ref[...] * u_ref[...].reshape(S5_TL * bsz, GROUP_W) + y_bt)
    gate = jnp.dot(y.astype(BF16), w_ref[...], preferred_element_type=F32) + bg_ref[...]
    o_ref[...] = (y * jax.nn.sigmoid(gate)).reshape(bsz, S5_TL, GROUP_W)


def _s5_mixer(rest, ssm_col, bmat, cmat, amat, layer, d_skip, w_glu, b_glu, bsz, seq):
    n_chunks = seq // S5_TL
    n_rows = S5_TL * bsz
    n_st2 = bmat.shape[-1]
    rest3 = rest.reshape(bsz, seq, rest.shape[1])
    full = lambda a: pl.BlockSpec(a.shape, lambda i: (0,) * a.ndim)
    scratch = [pltpu.VMEM((n_rows, n_st2), F32),
               pltpu.VMEM((bsz, n_st2), F32),
               pltpu.VMEM((GROUP_W // LANES, n_rows, LANES), F32)]

    def direction(d, chunk_of):
        return [
            pl.BlockSpec((bsz, S5_TL, GROUP_W), lambda i: (0, chunk_of(i), ssm_col)),
            pl.BlockSpec((None, None, GROUP_W, n_st2), lambda i: (layer, d, 0, 0)),
            pl.BlockSpec((None, None, n_st2, GROUP_W), lambda i: (layer, d, 0, 0)),
            pl.BlockSpec((None, None, 2, bsz, n_st2 // 2), lambda i: (layer, d, 0, 0, 0)),
        ]

    y_fwd = pl.pallas_call(
        functools.partial(_s5_kernel, bsz=bsz, backward=False),
        grid=(n_chunks,),
        in_specs=direction(0, lambda i: i),
        out_specs=pl.BlockSpec((n_rows, GROUP_W), lambda i: (i, 0)),
        out_shape=jax.ShapeDtypeStruct((seq * bsz, GROUP_W), F32),
        scratch_shapes=scratch,
        compiler_params=_cparams(("arbitrary",)),
        name="s5_forward",
    )(rest3, bmat, cmat, amat)

    rev = lambda i: n_chunks - 1 - i
    d_skip, b_glu = d_skip.reshape(1, GROUP_W), b_glu.reshape(1, GROUP_W)
    out = pl.pallas_call(
        functools.partial(_s5_kernel, bsz=bsz, backward=True),
        grid=(n_chunks,),
        in_specs=direction(1, rev) + [
            pl.BlockSpec((n_rows, GROUP_W), lambda i: (rev(i), 0)),
            full(d_skip), full(w_glu), full(b_glu),
        ],
        out_specs=pl.BlockSpec((bsz, S5_TL, GROUP_W), lambda i: (0, rev(i), 0)),
        out_shape=jax.ShapeDtypeStruct((bsz, seq, GROUP_W), F32),
        scratch_shapes=scratch,
        compiler_params=_cparams(("arbitrary",)),
        name="s5_backward",
    )(rest3, bmat, cmat, amat, y_fwd, d_skip, w_glu, b_glu)
    return out.reshape(bsz * seq, GROUP_W)


FFN_CHUNK = 256
ROUTE_COLS = 8
ROUTER_LANES = 128
ROUTER_ROW_BLOCK = 128


def _route_records(h, wr_ref):
    h_hi = h.astype(BF16)
    h_lo = (h - h_hi.astype(F32)).astype(BF16)
    pieces = jnp.concatenate([h_hi, h_lo, h_hi], axis=1)
    logits = jnp.concatenate(
        [jnp.dot(pieces[r0:r0 + ROUTER_ROW_BLOCK], wr_ref[...], preferred_element_type=F32)
         for r0 in range(0, h.shape[0], ROUTER_ROW_BLOCK)], axis=0)
    lane = lax.broadcasted_iota(jnp.int32, logits.shape, 1)
    minus_inf = -jnp.inf
    l1 = jnp.where(lane < N_EXPERTS, logits, minus_inf)
    m1 = jnp.max(l1, axis=-1, keepdims=True)
    i1 = jnp.min(jnp.where(l1 == m1, lane, ROUTER_LANES), axis=-1, keepdims=True)
    l2 = jnp.where(lane == i1, minus_inf, l1)
    m2 = jnp.max(l2, axis=-1, keepdims=True)
    i2 = jnp.min(jnp.where(l2 == m2, lane, ROUTER_LANES), axis=-1, keepdims=True)
    e2 = jnp.exp(m2 - m1)
    den = 1.0 + e2
    rec = jnp.where(lane == 0, i1.astype(F32),
                    jnp.where(lane == 1, i2.astype(F32),
                              jnp.where(lane == 2, 1.0 / den,
                                        jnp.where(lane == 3, e2 / den, 0.0))))
    return rec[:, :ROUTE_COLS]


def _mixer_out_kernel(*refs, tail):
    y_refs, (gn_ref, w_ref, x_ref, g2_ref) = refs[:4], refs[4:8]
    acc = x_ref[...]
    for j, y_ref in enumerate(y_refs):
        yn = _rms(y_ref[...], gn_ref[j:j + 1, :]).astype(BF16)
        acc = acc + jnp.dot(yn, w_ref[j * GROUP_W:(j + 1) * GROUP_W, :], preferred_element_type=F32)
    h = _rms(acc, g2_ref[...])
    if tail == "route":
        wr_ref, o_ref, route_ref = refs[8:]
        _store_token_tiles(o_ref, acc)
        route_ref[...] = _route_records(h, wr_ref)
        return
    wg_hbm, wu_hbm, wd_hbm, o_ref, wg_ref, wu_ref, wd_ref, sem = refs[8:]

    @pl.when(pl.program_id(0) == 0)
    def _():
        copies = [pltpu.make_async_copy(wd_hbm, wd_ref, sem.at[2])]
        for k, (src, dst) in enumerate(((wg_hbm, wg_ref), (wu_hbm, wu_ref))):
            copies += [pltpu.make_async_copy(src.at[:, pl.ds(c * FFN_CHUNK, FFN_CHUNK)], dst.at[c], sem.at[k])
                       for c in range(dst.shape[0])]
        for c in copies:
            c.start()
        for c in copies:
            c.wait()

    h = h.astype(BF16)
    acts = []
    for c in range(wg_ref.shape[0]):
        gate = jnp.dot(h, wg_ref[c], preferred_element_type=F32)
        up = jnp.dot(h, wu_ref[c], preferred_element_type=F32)
        acts.append((gate * jax.nn.sigmoid(gate) * up).astype(BF16))
    o_ref[...] = acc + jnp.dot(jnp.concatenate(acts, axis=1), wd_ref[...], preferred_element_type=F32)


def _mixer_out(ys, gn, w_out, x2, g2, tail, tail_weights, tm=512):
    t, d = x2.shape
    row = lambda n: pl.BlockSpec((tm, n), lambda i: (i, 0))
    whole = lambda a: pl.BlockSpec(a.shape, lambda i: (0, 0))
    if tail == "route":
        wr = jnp.zeros((d, ROUTER_LANES), F32).at[:, :N_EXPERTS].set(tail_weights[0].astype(F32))
        wr_hi = wr.astype(BF16)
        wr_lo = (wr - wr_hi.astype(F32)).astype(BF16)
        tail_weights = (jnp.concatenate([wr_hi, wr_hi, wr_lo], axis=0),)
        tail_specs = [whole(w) for w in tail_weights]
        out_specs = [pl.BlockSpec((tm * SUBLANES, LANES), lambda i: (i, 0)), row(ROUTE_COLS)]
        out_shape = [jax.ShapeDtypeStruct((t * SUBLANES, LANES), F32), jax.ShapeDtypeStruct((t, ROUTE_COLS), F32)]
        scratch = []
    else:
        w_gate, w_up, w_down = tail_weights
        chunk_major = pltpu.VMEM((w_gate.shape[1] // FFN_CHUNK, d, FFN_CHUNK), w_gate.dtype)
        tail_specs = [pl.BlockSpec(memory_space=pl.ANY)] * 3
        out_specs, out_shape = row(d), jax.ShapeDtypeStruct((t, d), F32)
        scratch = [chunk_major, chunk_major, pltpu.VMEM(w_down.shape, w_down.dtype), pltpu.SemaphoreType.DMA((3,))]
    return pl.pallas_call(
        functools.partial(_mixer_out_kernel, tail=tail),
        grid=(t // tm,),
        in_specs=[row(GROUP_W)] * 4 + [whole(gn), whole(w_out), row(d), whole(g2)] + tail_specs,
        out_specs=out_specs,
        out_shape=out_shape,
        scratch_shapes=scratch,
        compiler_params=_cparams(("arbitrary",)),
        name="mixer_out_" + tail,
    )(*ys, gn, w_out, x2, g2, *tail_weights)


def _moe_plan(route, tm, n_tiles):
    ids = route[:, :2]
    experts = jnp.where((ids >= 0) & (ids <= N_EXPERTS - 1), ids, 0).astype(jnp.int32).reshape(-1)
    onehot = (experts[:, None] == jnp.arange(N_EXPERTS, dtype=jnp.int32)[None, :]).astype(jnp.int32)
    csum = jnp.cumsum(onehot, axis=0)
    rank = jnp.sum((csum - onehot) * onehot, axis=1)
    count = csum[-1]
    padded = ((count + tm - 1) // tm) * tm
    end = jnp.cumsum(padded)
    start = end - padded
    slot = jnp.sum(onehot * start[None, :], axis=1) + rank
    n_active = end[-1] // tm
    tile_start = jnp.arange(n_tiles, dtype=jnp.int32) * tm
    tile_expert = jnp.sum((tile_start[:, None] >= end[None, :]).astype(jnp.int32), axis=1)
    tile_expert = jnp.minimum(tile_expert, N_EXPERTS - 1)
    last = tile_expert[jnp.maximum(n_active - 1, 0)]
    tile_expert = jnp.where(jnp.arange(n_tiles) < n_active, tile_expert, last)
    return slot.astype(jnp.int32), tile_expert.astype(jnp.int32), n_active.astype(jnp.int32).reshape(1)


def _token_copy(src, src_tok, dst, dst_tok, sem):
    rows = lambda tok: pl.ds(pl.multiple_of(tok * SUBLANES, SUBLANES), SUBLANES)
    return pltpu.make_async_copy(src.at[rows(src_tok)], dst.at[rows(dst_tok)], sem)


def _invert_kernel(slot_ref, word_ref, pad_hbm, plan_ref, sem):
    @pl.when(pl.program_id(0) == 0)
    def _():
        init = pltpu.make_async_copy(pad_hbm, plan_ref, sem)
        init.start()
        init.wait()

    def place(a, carry):
        plan_ref[slot_ref[0, 0, a]] = word_ref[0, 0, a]
        return carry

    lax.fori_loop(0, slot_ref.shape[-1], place, 0, unroll=8)


def _plan_bits(n_tok, tm):
    src_bits = (n_tok - 1).bit_length()
    assert (2 * n_tok + 2 * tm) << src_bits < 2 ** 31 and tm & (tm - 1) == 0
    return src_bits


def _invert(slot, n_slots, tm, n_tok, chunk=4096):
    src_bits = _plan_bits(n_tok, tm)
    a = jnp.arange(slot.shape[0], dtype=jnp.int32)
    tok = a >> 1
    word = tok + (((a & 1) * n_tok + tok) << src_bits)
    s = jnp.arange(n_slots, dtype=jnp.int32)
    pad_plan = (2 * n_tok + (s & (2 * tm - 1))) << src_bits
    n_chunks = slot.shape[0] // chunk
    chunked = pl.BlockSpec((1, 1, chunk), lambda c: (c, 0, 0), memory_space=pltpu.SMEM)
    return pl.pallas_call(
        _invert_kernel,
        grid=(n_chunks,),
        in_specs=[chunked, chunked, pl.BlockSpec(memory_space=pl.ANY)],
        out_specs=pl.BlockSpec((n_slots,), lambda c: (0,), memory_space=pltpu.SMEM),
        out_shape=jax.ShapeDtypeStruct((n_slots,), jnp.int32),
        scratch_shapes=[pltpu.SemaphoreType.DMA(())],
        compiler_params=pltpu.CompilerParams(dimension_semantics=("arbitrary",)),
        name="moe_invert",
    )(slot.reshape(n_chunks, 1, chunk), word.reshape(n_chunks, 1, chunk), pad_plan)


def _tokens_done(ref, n_tok, sem):
    rows = pl.ds(0, n_tok * SUBLANES)
    pltpu.make_async_copy(ref.at[rows], ref.at[rows], sem).wait()


def _moe_ffn_kernel(te_ref, nact_ref, plan_ref, x_hbm, g_ref, wg_ref, wu_ref, wd_ref, yk_hbm,
                    xbuf, xb16, ybuf, acc_ref, sem_in, sem_out, *, tm, n_tok, n_f, pad_tile, src_bits):
    del te_ref
    i = pl.program_id(0)
    f = pl.program_id(1)
    par = i % 2
    nact = nact_ref[0]

    def fetch(tile, p, j):
        src = plan_ref[tile * tm + j] & ((1 << src_bits) - 1)
        _token_copy(x_hbm, src, xbuf.at[p], j, sem_in.at[p]).start(priority=1)

    def send(tile, p, j):
        dst = plan_ref[tile * tm + j] >> src_bits
        _token_copy(ybuf.at[p], j, yk_hbm, dst, sem_out.at[p]).start()

    @pl.when((i == 0) & (f == 0))
    def _():
        def first(j, carry):
            fetch(0, 0, j)
            return carry
        lax.fori_loop(0, tm, first, 0)
        ybuf[...] = jnp.zeros_like(ybuf)
        for p in range(2):
            spare_rows = pl.ds((2 * n_tok + p * tm) * SUBLANES, tm * SUBLANES)
            spare = pltpu.make_async_copy(ybuf.at[p], yk_hbm.at[spare_rows], sem_out.at[p])
            spare.start()
            spare.wait()
        _tokens_done(xbuf.at[0], tm, sem_in.at[0])

    @pl.when((i >= 1) & (i <= nact) & (f == 0))
    def _():
        _tokens_done(xbuf.at[par], tm, sem_in.at[par])

    @pl.when(i < nact)
    def _():
        @pl.when(f == 0)
        def _():
            xb16[...] = _rms(_load_token_tiles(xbuf.at[par], tm), g_ref[...]).astype(BF16)
            acc_ref[...] = jnp.zeros_like(acc_ref)

            for j in range(tm):
                fetch(i + 1, 1 - par, j)

        @pl.when(f == n_f - 1)
        def _():
            prev = jnp.where(i > 0, i - 1, pad_tile)

            for j in range(tm):
                send(prev, 1 - par, j)

        x = xb16[...]
        gate = jnp.dot(x, wg_ref[...], preferred_element_type=F32)
        up = jnp.dot(x, wu_ref[...], preferred_element_type=F32)
        act = (gate * jax.nn.sigmoid(gate) * up).astype(BF16)
        acc_ref[...] += jnp.dot(act, wd_ref[...], preferred_element_type=F32)

        @pl.when(f == n_f - 1)
        def _():
            @pl.when(i >= 1)
            def _():
                _tokens_done(ybuf.at[par], tm, sem_out.at[par])

            _store_token_tiles(ybuf.at[par], acc_ref[...])

    @pl.when((i == nact) & (f == 0))
    def _():
        def last(j, carry):
            send(i - 1, 1 - par, j)
            return carry
        lax.fori_loop(0, tm, last, 0)
        _tokens_done(ybuf.at[0], tm, sem_out.at[0])
        _tokens_done(ybuf.at[1], tm, sem_out.at[1])


def _moe_ffn(xt, g, plan, tile_expert, n_active, wg, wu, wd, tm):
    n_tok = xt.shape[0] // SUBLANES
    d = SUBLANES * LANES
    n_tiles = tile_expert.shape[0]
    pad_tile = n_tiles - 1 if (n_tiles - 1) % 2 else n_tiles
    assert plan.shape[0] >= (pad_tile + 1) * tm
    dff = wg.shape[-1]
    n_f = 2
    tf = dff // n_f
    fidx = lambda i, f, na: jnp.where(i < na[0], f, n_f - 1)
    grid_spec = pltpu.PrefetchScalarGridSpec(
        num_scalar_prefetch=3,
        grid=(n_tiles, n_f),
        in_specs=[
            pl.BlockSpec(memory_space=pl.ANY),
            pl.BlockSpec((1, d), lambda i, f, te, na, plan: (0, 0)),
            pl.BlockSpec((None, d, tf), lambda i, f, te, na, plan: (te[i], 0, fidx(i, f, na))),
            pl.BlockSpec((None, d, tf), lambda i, f, te, na, plan: (te[i], 0, fidx(i, f, na))),
            pl.BlockSpec((None, tf, d), lambda i, f, te, na, plan: (te[i], fidx(i, f, na), 0)),
        ],
        out_specs=pl.BlockSpec(memory_space=pl.ANY),
        scratch_shapes=[
            pltpu.VMEM((2, tm * SUBLANES, LANES), F32),
            pltpu.VMEM((tm, d), BF16),
            pltpu.VMEM((2, tm * SUBLANES, LANES), F32),
            pltpu.VMEM((tm, d), F32),
            pltpu.SemaphoreType.DMA((2,)),
            pltpu.SemaphoreType.DMA((2,)),
        ],
    )
    return pl.pallas_call(
        functools.partial(_moe_ffn_kernel, tm=tm, n_tok=n_tok, n_f=n_f, pad_tile=pad_tile,
                          src_bits=_plan_bits(n_tok, tm)),
        grid_spec=grid_spec,
        out_shape=jax.ShapeDtypeStruct(((2 * n_tok + 2 * tm) * SUBLANES, LANES), F32),
        compiler_params=_cparams(("arbitrary", "arbitrary")),
        name="moe_ffn",
    )(tile_expert, n_active, plan, xt, g.reshape(1, d), wg, wu, wd)


def _combine_kernel(route_ref, x_ref, y0_ref, y1_ref, fg_ref, o_ref, *, final_norm):
    r = route_ref[...]
    tm = r.shape[0]
    y = (_load_token_tiles(x_ref, tm) + r[:, 2:3] * _load_token_tiles(y0_ref, tm)
         + r[:, 3:4] * _load_token_tiles(y1_ref, tm))
    o_ref[...] = _rms(y, fg_ref[...]) if final_norm else y


def _combine(yk, route, xt, final_g, final_norm, tm=512):
    t = xt.shape[0] // SUBLANES
    d = SUBLANES * LANES
    n_blk = t // tm
    tiles = lambda f: pl.BlockSpec((tm * SUBLANES, LANES), f)
    return pl.pallas_call(
        functools.partial(_combine_kernel, final_norm=final_norm),
        grid=(n_blk,),
        in_specs=[
            pl.BlockSpec((tm, ROUTE_COLS), lambda i: (i, 0)),
            tiles(lambda i: (i, 0)),
            tiles(lambda i: (i, 0)),
            tiles(lambda i: (n_blk + i, 0)),
            pl.BlockSpec((1, d), lambda i: (0, 0)),
        ],
        out_specs=pl.BlockSpec((tm, d), lambda i: (i, 0)),
        out_shape=jax.ShapeDtypeStruct((t, d), F32),
        compiler_params=_cparams(("arbitrary",)),
        name="moe_combine",
    )(route, xt, yk, yk, final_g.reshape(1, d))


def _final_norm_kernel(x_ref, g_ref, o_ref):
    o_ref[...] = _rms(x_ref[...], g_ref[...])


def _final_norm(x2, g, tm=512):
    t, d = x2.shape
    return pl.pallas_call(
        _final_norm_kernel,
        grid=(t // tm,),
        in_specs=[pl.BlockSpec((tm, d), lambda i: (i, 0)), pl.BlockSpec((1, d), lambda i: (0, 0))],
        out_specs=pl.BlockSpec((tm, d), lambda i: (i, 0)),
        out_shape=jax.ShapeDtypeStruct((t, d), F32),
        compiler_params=_cparams(("arbitrary",)),
        name="final_norm",
    )(x2, g.reshape(1, d))


MOE_TM = 512


def kernel(x, norm1_g, w_in, na_rpb, sc_conv_w, cf_conv_w, cf_conv_b, cf_ln_g, cf_ln_b, ssm_a_re, ssm_a_im,
           ssm_log_dt, ssm_b_re, ssm_b_im, ssm_c_re, ssm_c_im, ssm_d, ssm_w_glu, ssm_b_glu, grp_norm_g, w_out,
           norm2_g, ffn_w_gate, ffn_w_up, ffn_w_down, moe_w_router, moe_w_gate, moe_w_up, moe_w_down,
           final_norm_g):
    bsz, seq, d = x.shape
    depth = w_in.shape[0]
    t = bsz * seq
    rows = seq // GRID_W
    x2 = x.reshape(t, d).astype(F32)
    final_done = False
    na_bias = jax.vmap(lambda rpb: _na_bias_table(rpb, rows))(na_rpb)
    s5_mats = jax.vmap(lambda *p: _s5_discretise(*p, bsz))(ssm_a_re, ssm_a_im, ssm_log_dt, ssm_b_re, ssm_b_im,
                                                           ssm_c_re, ssm_c_im)
    for i in range(depth):
        qkv, rest = _in_proj(x2, norm1_g[i], w_in[i].astype(BF16))
        y_na = _na_attention(qkv, na_bias, i, bsz, seq)
        y_sc, y_cf = _conv_mixers(rest, sc_conv_w[i], cf_conv_w[i], cf_conv_b[i], cf_ln_g[i], cf_ln_b[i],
                                  bsz, seq)
        y_ssm = _s5_mixer(rest, SSM_COL, *s5_mats, i, ssm_d[i], ssm_w_glu[i].astype(BF16), ssm_b_glu[i],
                          bsz, seq)
        ys = (y_na, y_sc, y_cf, y_ssm)
        gn, g2 = grp_norm_g[i].reshape(4, GROUP_W), norm2_g[i].reshape(1, d)
        j = i // 2
        if i % 2 == 0:
            ffn_w = (ffn_w_gate[j].astype(BF16), ffn_w_up[j].astype(BF16), ffn_w_down[j].astype(BF16))
            x2 = _mixer_out(ys, gn, w_out[i].astype(BF16), x2, g2, "ffn", ffn_w)
        else:
            x2, route = _mixer_out(ys, gn, w_out[i].astype(BF16), x2, g2, "route", (moe_w_router[j],))
            n_tiles = (2 * t) // MOE_TM + N_EXPERTS + 1
            slot, tile_expert, n_active = _moe_plan(route, MOE_TM, n_tiles)
            plan = _invert(slot, (n_tiles + 1) * MOE_TM, MOE_TM, t)
            yk = _moe_ffn(x2, norm2_g[i], plan, tile_expert, n_active, moe_w_gate[j].astype(BF16),
                          moe_w_up[j].astype(BF16), moe_w_down[j].astype(BF16), MOE_TM)
            final_done = i == depth - 1
            x2 = _combine(yk, route, x2, final_norm_g, final_done)
    if not final_done:
        x2 = _final_norm(x2, final_norm_g)
    return x2.reshape(bsz, seq, d)
```

```python
import functools
import math

import jax
import jax.numpy as jnp
import numpy as np
from jax import lax
from jax.experimental import pallas as pl
from jax.experimental.pallas import tpu as pltpu

F32 = jnp.float32
BF16 = jnp.bfloat16
EPS = 1e-6
NEG_INF = -1e30

GRID_W = 64
NA_HEAD_DIM = 64
NA_WIN_H = 8
NA_WIN_W = 16
GROUP_W = 256
SSM_GROUP_CH = 16
SSM_STATE = 64
N_EXPERTS = 8
VMEM_LIMIT = 56 * 1024 * 1024


def _cparams(sem, vmem=VMEM_LIMIT):
    return pltpu.CompilerParams(dimension_semantics=sem, vmem_limit_bytes=vmem)


def _rms(x, g):
    return x * lax.rsqrt(jnp.mean(x * x, axis=-1, keepdims=True) + EPS) * g


SUBLANES = 8
LANES = 128


def _load_token_tiles(ref, n):
    return jnp.concatenate([ref[pl.ds(s, n, stride=SUBLANES), :] for s in range(SUBLANES)], axis=1)


def _store_token_tiles(ref, val):
    for s in range(SUBLANES):
        ref[pl.ds(s, val.shape[0], stride=SUBLANES), :] = val[:, s * LANES:(s + 1) * LANES]


N_QKV_COLS = 3 * GROUP_W
SSM_COL = 5


def _in_proj_kernel(x_ref, g_ref, w_ref, qkv_ref, rest_ref):
    h = _rms(x_ref[...], g_ref[...]).astype(BF16)
    n_qkv = qkv_ref.shape[1] // GROUP_W
    n_rest = rest_ref.shape[1] // GROUP_W
    for j in range(n_qkv + n_rest):
        z = jnp.dot(h, w_ref[:, j * GROUP_W:(j + 1) * GROUP_W], preferred_element_type=F32)
        if j < n_qkv:
            qkv_ref[:, j * GROUP_W:(j + 1) * GROUP_W] = z.astype(BF16)
        else:
            k = j - n_qkv
            rest_ref[:, k * GROUP_W:(k + 1) * GROUP_W] = z


def _in_proj(x2, g, w, tm=1024):
    t, d = x2.shape
    n_cols = w.shape[1]
    n_rest = n_cols - N_QKV_COLS
    return pl.pallas_call(
        _in_proj_kernel,
        grid=(t // tm,),
        in_specs=[
            pl.BlockSpec((tm, d), lambda i: (i, 0)),
            pl.BlockSpec((1, d), lambda i: (0, 0)),
            pl.BlockSpec((d, n_cols), lambda i: (0, 0)),
        ],
        out_specs=[
            pl.BlockSpec((tm, N_QKV_COLS), lambda i: (i, 0)),
            pl.BlockSpec((tm, n_rest), lambda i: (i, 0)),
        ],
        out_shape=[
            jax.ShapeDtypeStruct((t, N_QKV_COLS), BF16),
            jax.ShapeDtypeStruct((t, n_rest), F32),
        ],
        compiler_params=_cparams(("arbitrary",)),
        name="in_proj",
    )(x2, g.reshape(1, d), w)


def _na_bias_table(rpb, rows):
    n_heads = rpb.shape[0]
    kh = NA_WIN_H
    c = np.arange(GRID_W)
    qcs = np.clip(c - NA_WIN_W // 2, 0, GRID_W - NA_WIN_W)
    kc = np.arange(GRID_W)
    in_win = (kc[None, :] >= qcs[:, None]) & (kc[None, :] < qcs[:, None] + NA_WIN_W)
    dc_idx = np.clip(kc[None, :] - c[:, None] + NA_WIN_W - 1, 0, 2 * NA_WIN_W - 2)
    onehot_dc = (dc_idx[:, :, None] == np.arange(2 * NA_WIN_W - 1)).astype(np.float32)
    rep_rows = np.array([0, 1, 2, 3, 4, rows - 3, rows - 2, rows - 1])
    row_start = np.clip(rep_rows - kh // 2, 0, rows - kh)
    dr_idx = (row_start - rep_rows + NA_WIN_H - 1)[:, None] + np.arange(kh)[None, :]
    onehot_dr = (dr_idx[:, :, None] == np.arange(2 * NA_WIN_H - 1)).astype(np.float32)
    tab = jnp.einsum("via,hab,ckb->vhcik", jnp.asarray(onehot_dr), rpb.astype(F32), jnp.asarray(onehot_dc),
                     precision=lax.Precision.HIGHEST)
    tab = jnp.where(jnp.asarray(in_win)[None, None, :, None, :], tab, NEG_INF)
    return tab.reshape(len(rep_rows), n_heads * GRID_W, kh * GRID_W)


def _na_kernel(q_ref, k_ref, v_ref, bias_ref, o_ref, *, rows, n_heads):
    kh = NA_WIN_H
    lane = lax.broadcasted_iota(jnp.int32, (GRID_W, n_heads * NA_HEAD_DIM), 1)
    masks = [(lane >= h * NA_HEAD_DIM) & (lane < (h + 1) * NA_HEAD_DIM) for h in range(n_heads)]
    scale = NA_HEAD_DIM ** -0.5

    def body(r, carry):
        q0 = pl.multiple_of(r * GRID_W, GRID_W)
        q = q_ref[pl.ds(q0, GRID_W), :].astype(F32) * scale
        qs = jnp.concatenate([jnp.where(m, q, 0.0) for m in masks], axis=0).astype(BF16)
        rs = jnp.clip(r - kh // 2, 0, rows - kh)
        k0 = pl.multiple_of(rs * GRID_W, GRID_W)
        ks = k_ref[pl.ds(k0, kh * GRID_W), :]
        vs = v_ref[pl.ds(k0, kh * GRID_W), :]
        s = lax.dot_general(qs, ks, (((1,), (1,)), ((), ())), preferred_element_type=F32)
        var = jnp.where(r < kh // 2, r, jnp.where(r > rows - kh // 2, r - (rows - 8), kh // 2))
        s = s + bias_ref[var]
        m = jnp.max(s, axis=-1, keepdims=True)
        p = jnp.exp(s - m)
        den = jnp.sum(p, axis=-1, keepdims=True)
        o = jnp.dot(p.astype(BF16), vs, preferred_element_type=F32) / den
        out = jnp.where(masks[0], o[0:GRID_W], 0.0)
        for h in range(1, n_heads):
            out = out + jnp.where(masks[h], o[h * GRID_W:(h + 1) * GRID_W], 0.0)
        o_ref[pl.ds(q0, GRID_W), :] = out
        return carry

    lax.fori_loop(0, rows, body, 0, unroll=8)


def _na_attention(qkv, bias, layer, bsz, seq):
    rows = seq // GRID_W
    n_heads = GROUP_W // NA_HEAD_DIM
    t = qkv.shape[0]
    blk = lambda j: pl.BlockSpec((seq, GROUP_W), lambda b, j=j: (b, j))
    return pl.pallas_call(
        functools.partial(_na_kernel, rows=rows, n_heads=n_heads),
        grid=(bsz,),
        in_specs=[blk(0), blk(1), blk(2),
                  pl.BlockSpec((None,) + bias.shape[1:], lambda b: (layer, 0, 0, 0))],
        out_specs=pl.BlockSpec((seq, GROUP_W), lambda b: (b, 0)),
        out_shape=jax.ShapeDtypeStruct((t, GROUP_W), F32),
        compiler_params=_cparams(("arbitrary",)),
        name="na_attention",
    )(qkv, qkv, qkv, bias)


CONV_PAD = 16
CONV_CHUNK = 512


def _conv_kernel(b_ref, c_ref, x_ref, a_ref, g_ref, scw_ref, cfw_ref, cfb_ref, lng_ref, lnb_ref,
                 ysc_ref, ycf_ref, psc, pcf):
    seq = b_ref.shape[0]
    n_chunks = seq // CONV_CHUNK
    zeros = jnp.zeros((CONV_PAD, GROUP_W), F32)
    psc[0:CONV_PAD, :] = zeros
    pcf[0:CONV_PAD, :] = zeros
    psc[CONV_PAD + seq:2 * CONV_PAD + seq, :] = zeros
    pcf[CONV_PAD + seq:2 * CONV_PAD + seq, :] = zeros

    def fill(i, carry):
        r0 = pl.multiple_of(i * CONV_CHUNK, CONV_CHUNK)
        rows = pl.ds(r0, CONV_CHUNK)
        dst = pl.ds(r0 + CONV_PAD, CONV_CHUNK)
        psc[dst, :] = c_ref[rows, :] * x_ref[rows, :]
        pcf[dst, :] = a_ref[rows, :] * jax.nn.sigmoid(g_ref[rows, :])
        return carry

    lax.fori_loop(0, n_chunks, fill, 0)

    def taps(win, w_ref):
        n_taps = w_ref.shape[0]
        offs = [CONV_PAD + k - n_taps // 2 for k in range(n_taps)]
        n_win = win.shape[0]
        acc = None
        for s in range(SUBLANES):
            ks = [k for k in range(n_taps) if offs[k] % SUBLANES == s]
            if not ks:
                continue
            shifted = pltpu.roll(win, n_win - s, axis=0) if s else win
            for k in ks:
                q = offs[k] - s
                term = shifted[q:q + CONV_CHUNK, :] * w_ref[k:k + 1, :]
                acc = term if acc is None else acc + term
        return acc

    def conv(i, carry):
        r0 = pl.multiple_of(i * CONV_CHUNK, CONV_CHUNK)
        rows = pl.ds(r0, CONV_CHUNK)
        window = pl.ds(r0, CONV_CHUNK + 2 * CONV_PAD)
        ysc_ref[rows, :] = b_ref[rows, :] * taps(psc[window, :], scw_ref)
        cf = taps(pcf[window, :], cfw_ref) + cfb_ref[...]
        mu = jnp.mean(cf, axis=-1, keepdims=True)
        xc = cf - mu
        var = jnp.mean(xc * xc, axis=-1, keepdims=True)
        ln = xc * lax.rsqrt(var + EPS) * lng_ref[...] + lnb_ref[...]
        ycf_ref[rows, :] = ln * jax.nn.sigmoid(ln)
        return carry

    lax.fori_loop(0, n_chunks, conv, 0)


def _conv_mixers(rest, sc_w, cf_w, cf_b, ln_g, ln_b, bsz, seq):
    t = rest.shape[0]
    blk = lambda j: pl.BlockSpec((seq, GROUP_W), lambda b, j=j: (b, j))
    full = lambda a: pl.BlockSpec(a.shape, lambda b: (0,) * a.ndim)
    cf_b, ln_g, ln_b = (a.reshape(1, GROUP_W) for a in (cf_b, ln_g, ln_b))
    out_spec = pl.BlockSpec((seq, GROUP_W), lambda b: (b, 0))
    return pl.pallas_call(
        _conv_kernel,
        grid=(bsz,),
        in_specs=[blk(0), blk(1), blk(2), blk(3), blk(4),
                  full(sc_w), full(cf_w), full(cf_b), full(ln_g), full(ln_b)],
        out_specs=[out_spec, out_spec],
        out_shape=[jax.ShapeDtypeStruct((t, GROUP_W), F32)] * 2,
        scratch_shapes=[pltpu.VMEM((seq + 2 * CONV_PAD, GROUP_W), F32)] * 2,
        compiler_params=_cparams(("arbitrary",)),
        name="conv_mixers",
    )(rest, rest, rest, rest, rest, sc_w, cf_w, cf_b, ln_g, ln_b)


S5_TL = 256
S5_LANE_SPLIT = 1
S5_ROW_BLOCK = 256


def _s5_discretise(a_re, a_im, log_dt, b_re, b_im, c_re, c_im, bsz):
    f32 = F32
    a_re, a_im, log_dt = a_re.astype(f32), a_im.astype(f32), log_dt.astype(f32)
    n_dir, n_grp, n_state = a_re.shape
    n_ch = b_re.shape[-1]
    dt = jnp.exp(log_dt)[..., None]
    mag = jnp.exp(a_re * dt)
    abr = mag * jnp.cos(a_im * dt)
    abi = mag * jnp.sin(a_im * dt)
    den = a_re * a_re + a_im * a_im
    fr = ((abr - 1.0) * a_re + abi * a_im) / den
    fi = (abi * a_re - (abr - 1.0) * a_im) / den
    bbr = fr[..., None] * b_re - fi[..., None] * b_im
    bbi = fr[..., None] * b_im + fi[..., None] * b_re
    eye = jnp.eye(n_grp, dtype=f32)
    blk_b = lambda m: jnp.einsum("dgps,gh->dgshp", m, eye).reshape(n_dir, n_grp * n_ch, n_grp * n_state)
    blk_c = lambda m: jnp.einsum("dgsp,gh->dgphs", m, eye).reshape(n_dir, n_grp * n_state, n_grp * n_ch)
    bmat = jnp.concatenate([blk_b(bbr), blk_b(bbi)], axis=-1)
    cmat = jnp.concatenate([blk_c(c_re.astype(f32)), blk_c(-c_im.astype(f32))], axis=1)
    amat = jnp.stack([abr.reshape(n_dir, -1), abi.reshape(n_dir, -1)], axis=1)
    amat = jnp.broadcast_to(amat[:, :, None, :], (n_dir, 2, bsz, n_grp * n_state))
    return bmat.astype(BF16), cmat.astype(BF16), amat


def _s5_kernel(*refs, bsz, backward):
    if backward:
        (u_ref, b_ref, c_ref, a_ref, yf_ref, d_ref, w_ref, bg_ref, o_ref, xs, st, tb) = refs
    else:
        (u_ref, b_ref, c_ref, a_ref, o_ref, xs, st, tb) = refs
    n_state = a_ref.shape[-1]

    @pl.when(pl.program_id(0) == 0)
    def _():
        st[...] = jnp.zeros_like(st)

    n_half = GROUP_W // LANES
    for b in range(bsz):
        for h in range(n_half):
            tb[h, pl.ds(b, S5_TL, stride=bsz), :] = u_ref[b, :, h * LANES:(h + 1) * LANES]
    for r0 in range(0, S5_TL * bsz, S5_ROW_BLOCK):
        rows = slice(r0, r0 + S5_ROW_BLOCK)
        u_tb = jnp.concatenate([tb[h, rows, :] for h in range(n_half)], axis=1)
        xs[rows, :] = jnp.dot(u_tb.astype(BF16), b_ref[...], preferred_element_type=F32)

    cw = n_state // S5_LANE_SPLIT
    for c in range(S5_LANE_SPLIT):
        re = slice(c * cw, (c + 1) * cw)
        im = slice(n_state + c * cw, n_state + (c + 1) * cw)
        a_r = a_ref[0, :, re]
        a_i = a_ref[1, :, re]

        def step(t, carry, re=re, im=im, a_r=a_r, a_i=a_i):
            s_r, s_i = carry
            tt = S5_TL - 1 - t if backward else t
            rows = pl.ds(pl.multiple_of(tt * bsz, bsz), bsz)
            n_r = a_r * s_r - a_i * s_i + xs[rows, re]
            n_i = a_r * s_i + a_i * s_r + xs[rows, im]
            xs[rows, re] = n_r
            xs[rows, im] = n_i
            return n_r, n_i

        s_r, s_i = lax.fori_loop(0, S5_TL, step, (st[:, re], st[:, im]), unroll=4)
        st[:, re] = s_r
        st[:, im] = s_i

    n_rows = S5_TL * bsz
    y_tb = jnp.concatenate(
        [jnp.dot(xs[r0:r0 + S5_ROW_BLOCK, :].astype(BF16), c_ref[...], preferred_element_type=F32)
         for r0 in range(0, n_rows, S5_ROW_BLOCK)], axis=0)
    if not backward:
        o_ref[...] = y_tb
        return
    y_sum = y_tb + yf_ref[...]
    for h in range(n_half):
        tb[h] = y_sum[:, h * LANES:(h + 1) * LANES]
    y_bt = jnp.concatenate(
        [jnp.concatenate([tb[h, pl.ds(b, S5_TL, stride=bsz), :] for h in range(n_half)], axis=1)
         for b in range(bsz)], axis=0)
    y = jax.nn.gelu(d_ref[...] * u_ref[...].reshape(S5_TL * bsz, GROUP_W) + y_bt)
    gate = jnp.dot(y.astype(BF16), w_ref[...], preferred_element_type=F32) + bg_ref[...]
    o_ref[...] = (y * jax.nn.sigmoid(gate)).reshape(bsz, S5_TL, GROUP_W)


def _s5_mixer(rest, ssm_col, bmat, cmat, amat, layer, d_skip, w_glu, b_glu, bsz, seq):
    n_chunks = seq // S5_TL
    n_rows = S5_TL * bsz
    n_st2 = bmat.shape[-1]
    rest3 = rest.reshape(bsz, seq, rest.shape[1])
    full = lambda a: pl.BlockSpec(a.shape, lambda i: (0,) * a.ndim)
    scratch = [pltpu.VMEM((n_rows, n_st2), F32),
               pltpu.VMEM((bsz, n_st2), F32),
               pltpu.VMEM((GROUP_W // LANES, n_rows, LANES), F32)]

    def direction(d, chunk_of):
        return [
            pl.BlockSpec((bsz, S5_TL, GROUP_W), lambda i: (0, chunk_of(i), ssm_col)),
            pl.BlockSpec((None, None, GROUP_W, n_st2), lambda i: (layer, d, 0, 0)),
            pl.BlockSpec((None, None, n_st2, GROUP_W), lambda i: (layer, d, 0, 0)),
            pl.BlockSpec((None, None, 2, bsz, n_st2 // 2), lambda i: (layer, d, 0, 0, 0)),
        ]

    y_fwd = pl.pallas_call(
        functools.partial(_s5_kernel, bsz=bsz, backward=False),
        grid=(n_chunks,),
        in_specs=direction(0, lambda i: i),
        out_specs=pl.BlockSpec((n_rows, GROUP_W), lambda i: (i, 0)),
        out_shape=jax.ShapeDtypeStruct((seq * bsz, GROUP_W), F32),
        scratch_shapes=scratch,
        compiler_params=_cparams(("arbitrary",)),
        name="s5_forward",
    )(rest3, bmat, cmat, amat)

    rev = lambda i: n_chunks - 1 - i
    d_skip, b_glu = d_skip.reshape(1, GROUP_W), b_glu.reshape(1, GROUP_W)
    out = pl.pallas_call(
        functools.partial(_s5_kernel, bsz=bsz, backward=True),
        grid=(n_chunks,),
        in_specs=direction(1, rev) + [
            pl.BlockSpec((n_rows, GROUP_W), lambda i: (rev(i), 0)),
            full(d_skip), full(w_glu), full(b_glu),
        ],
        out_specs=pl.BlockSpec((bsz, S5_TL, GROUP_W), lambda i: (0, rev(i), 0)),
        out_shape=jax.ShapeDtypeStruct((bsz, seq, GROUP_W), F32),
        scratch_shapes=scratch,
        compiler_params=_cparams(("arbitrary",)),
        name="s5_backward",
    )(rest3, bmat, cmat, amat, y_fwd, d_skip, w_glu, b_glu)
    return out.reshape(bsz * seq, GROUP_W)


FFN_CHUNK = 256
ROUTE_COLS = 8
ROUTER_LANES = 128
ROUTER_ROW_BLOCK = 128


def _route_records(h, wr_ref):
    h_hi = h.astype(BF16)
    h_lo = (h - h_hi.astype(F32)).astype(BF16)
    pieces = jnp.concatenate([h_hi, h_lo, h_hi], axis=1)
    logits = jnp.concatenate(
        [jnp.dot(pieces[r0:r0 + ROUTER_ROW_BLOCK], wr_ref[...], preferred_element_type=F32)
         for r0 in range(0, h.shape[0], ROUTER_ROW_BLOCK)], axis=0)
    lane = lax.broadcasted_iota(jnp.int32, logits.shape, 1)
    minus_inf = -jnp.inf
    l1 = jnp.where(lane < N_EXPERTS, logits, minus_inf)
    m1 = jnp.max(l1, axis=-1, keepdims=True)
    i1 = jnp.min(jnp.where(l1 == m1, lane, ROUTER_LANES), axis=-1, keepdims=True)
    l2 = jnp.where(lane == i1, minus_inf, l1)
    m2 = jnp.max(l2, axis=-1, keepdims=True)
    i2 = jnp.min(jnp.where(l2 == m2, lane, ROUTER_LANES), axis=-1, keepdims=True)
    e2 = jnp.exp(m2 - m1)
    den = 1.0 + e2
    rec = jnp.where(lane == 0, i1.astype(F32),
                    jnp.where(lane == 1, i2.astype(F32),
                              jnp.where(lane == 2, 1.0 / den,
                                        jnp.where(lane == 3, e2 / den, 0.0))))
    return rec[:, :ROUTE_COLS]


def _mixer_out_kernel(*refs, tail):
    y_refs, (gn_ref, w_ref, x_ref, g2_ref) = refs[:4], refs[4:8]
    acc = x_ref[...]
    for j, y_ref in enumerate(y_refs):
        yn = _rms(y_ref[...], gn_ref[j:j + 1, :]).astype(BF16)
        acc = acc + jnp.dot(yn, w_ref[j * GROUP_W:(j + 1) * GROUP_W, :], preferred_element_type=F32)
    h = _rms(acc, g2_ref[...])
    if tail == "route":
        wr_ref, o_ref, route_ref = refs[8:]
        _store_token_tiles(o_ref, acc)
        route_ref[...] = _route_records(h, wr_ref)
        return
    wg_hbm, wu_hbm, wd_hbm, o_ref, wg_ref, wu_ref, wd_ref, sem = refs[8:]

    @pl.when(pl.program_id(0) == 0)
    def _():
        copies = [pltpu.make_async_copy(wd_hbm, wd_ref, sem.at[2])]
        for k, (src, dst) in enumerate(((wg_hbm, wg_ref), (wu_hbm, wu_ref))):
            copies += [pltpu.make_async_copy(src.at[:, pl.ds(c * FFN_CHUNK, FFN_CHUNK)], dst.at[c], sem.at[k])
                       for c in range(dst.shape[0])]
        for c in copies:
            c.start()
        for c in copies:
            c.wait()

    h = h.astype(BF16)
    acts = []
    for c in range(wg_ref.shape[0]):
        gate = jnp.dot(h, wg_ref[c], preferred_element_type=F32)
        up = jnp.dot(h, wu_ref[c], preferred_element_type=F32)
        acts.append((gate * jax.nn.sigmoid(gate) * up).astype(BF16))
    o_ref[...] = acc + jnp.dot(jnp.concatenate(acts, axis=1), wd_ref[...], preferred_element_type=F32)


def _mixer_out(ys, gn, w_out, x2, g2, tail, tail_weights, tm=512):
    t, d = x2.shape
    row = lambda n: pl.BlockSpec((tm, n), lambda i: (i, 0))
    whole = lambda a: pl.BlockSpec(a.shape, lambda i: (0, 0))
    if tail == "route":
        wr = jnp.zeros((d, ROUTER_LANES), F32).at[:, :N_EXPERTS].set(tail_weights[0].astype(F32))
        wr_hi = wr.astype(BF16)
        wr_lo = (wr - wr_hi.astype(F32)).astype(BF16)
        tail_weights = (jnp.concatenate([wr_hi, wr_hi, wr_lo], axis=0),)
        tail_specs = [whole(w) for w in tail_weights]
        out_specs = [pl.BlockSpec((tm * SUBLANES, LANES), lambda i: (i, 0)), row(ROUTE_COLS)]
        out_shape = [jax.ShapeDtypeStruct((t * SUBLANES, LANES), F32), jax.ShapeDtypeStruct((t, ROUTE_COLS), F32)]
        scratch = []
    else:
        w_gate, w_up, w_down = tail_weights
        chunk_major = pltpu.VMEM((w_gate.shape[1] // FFN_CHUNK, d, FFN_CHUNK), w_gate.dtype)
        tail_specs = [pl.BlockSpec(memory_space=pl.ANY)] * 3
        out_specs, out_shape = row(d), jax.ShapeDtypeStruct((t, d), F32)
        scratch = [chunk_major, chunk_major, pltpu.VMEM(w_down.shape, w_down.dtype), pltpu.SemaphoreType.DMA((3,))]
    return pl.pallas_call(
        functools.partial(_mixer_out_kernel, tail=tail),
        grid=(t // tm,),
        in_specs=[row(GROUP_W)] * 4 + [whole(gn), whole(w_out), row(d), whole(g2)] + tail_specs,
        out_specs=out_specs,
        out_shape=out_shape,
        scratch_shapes=scratch,
        compiler_params=_cparams(("arbitrary",)),
        name="mixer_out_" + tail,
    )(*ys, gn, w_out, x2, g2, *tail_weights)


def _moe_plan(route, tm, n_tiles):
    ids = route[:, :2]
    experts = jnp.where((ids >= 0) & (ids <= N_EXPERTS - 1), ids, 0).astype(jnp.int32).reshape(-1)
    onehot = (experts[:, None] == jnp.arange(N_EXPERTS, dtype=jnp.int32)[None, :]).astype(jnp.int32)
    csum = jnp.cumsum(onehot, axis=0)
    rank = jnp.sum((csum - onehot) * onehot, axis=1)
    count = csum[-1]
    padded = ((count + tm - 1) // tm) * tm
    end = jnp.cumsum(padded)
    start = end - padded
    slot = jnp.sum(onehot * start[None, :], axis=1) + rank
    n_active = end[-1] // tm
    tile_start = jnp.arange(n_tiles, dtype=jnp.int32) * tm
    tile_expert = jnp.sum((tile_start[:, None] >= end[None, :]).astype(jnp.int32), axis=1)
    tile_expert = jnp.minimum(tile_expert, N_EXPERTS - 1)
    last = tile_expert[jnp.maximum(n_active - 1, 0)]
    tile_expert = jnp.where(jnp.arange(n_tiles) < n_active, tile_expert, last)
    return slot.astype(jnp.int32), tile_expert.astype(jnp.int32), n_active.astype(jnp.int32).reshape(1)


def _token_copy(src, src_tok, dst, dst_tok, sem):
    rows = lambda tok: pl.ds(pl.multiple_of(tok * SUBLANES, SUBLANES), SUBLANES)
    return pltpu.make_async_copy(src.at[rows(src_tok)], dst.at[rows(dst_tok)], sem)


def _invert_kernel(slot_ref, word_ref, pad_hbm, plan_ref, sem):
    @pl.when(pl.program_id(0) == 0)
    def _():
        init = pltpu.make_async_copy(pad_hbm, plan_ref, sem)
        init.start()
        init.wait()

    def place(a, carry):
        plan_ref[slot_ref[0, 0, a]] = word_ref[0, 0, a]
        return carry

    lax.fori_loop(0, slot_ref.shape[-1], place, 0, unroll=8)


def _plan_bits(n_tok, tm):
    src_bits = (n_tok - 1).bit_length()
    assert (2 * n_tok + 2 * tm) << src_bits < 2 ** 31 and tm & (tm - 1) == 0
    return src_bits


def _invert(slot, n_slots, tm, n_tok, chunk=4096):
    src_bits = _plan_bits(n_tok, tm)
    a = jnp.arange(slot.shape[0], dtype=jnp.int32)
    tok = a >> 1
    word = tok + (((a & 1) * n_tok + tok) << src_bits)
    s = jnp.arange(n_slots, dtype=jnp.int32)
    pad_plan = (2 * n_tok + (s & (2 * tm - 1))) << src_bits
    n_chunks = slot.shape[0] // chunk
    chunked = pl.BlockSpec((1, 1, chunk), lambda c: (c, 0, 0), memory_space=pltpu.SMEM)
    return pl.pallas_call(
        _invert_kernel,
        grid=(n_chunks,),
        in_specs=[chunked, chunked, pl.BlockSpec(memory_space=pl.ANY)],
        out_specs=pl.BlockSpec((n_slots,), lambda c: (0,), memory_space=pltpu.SMEM),
        out_shape=jax.ShapeDtypeStruct((n_slots,), jnp.int32),
        scratch_shapes=[pltpu.SemaphoreType.DMA(())],
        compiler_params=pltpu.CompilerParams(dimension_semantics=("arbitrary",)),
        name="moe_invert",
    )(slot.reshape(n_chunks, 1, chunk), word.reshape(n_chunks, 1, chunk), pad_plan)


def _tokens_done(ref, n_tok, sem):
    rows = pl.ds(0, n_tok * SUBLANES)
    pltpu.make_async_copy(ref.at[rows], ref.at[rows], sem).wait()


def _moe_ffn_kernel(te_ref, nact_ref, plan_ref, x_hbm, g_ref, wg_ref, wu_ref, wd_ref, yk_hbm,
                    xbuf, xb16, ybuf, acc_ref, sem_in, sem_out, *, tm, n_tok, n_f, pad_tile, src_bits):
    del te_ref
    i = pl.program_id(0)
    f = pl.program_id(1)
    par = i % 2
    nact = nact_ref[0]

    def fetch(tile, p, j):
        src = plan_ref[tile * tm + j] & ((1 << src_bits) - 1)
        _token_copy(x_hbm, src, xbuf.at[p], j, sem_in.at[p]).start(priority=1)

    def send(tile, p, j):
        dst = plan_ref[tile * tm + j] >> src_bits
        _token_copy(ybuf.at[p], j, yk_hbm, dst, sem_out.at[p]).start()

    @pl.when((i == 0) & (f == 0))
    def _():
        def first(j, carry):
            fetch(0, 0, j)
            return carry
        lax.fori_loop(0, tm, first, 0)
        ybuf[...] = jnp.zeros_like(ybuf)
        for p in range(2):
            spare_rows = pl.ds((2 * n_tok + p * tm) * SUBLANES, tm * SUBLANES)
            spare = pltpu.make_async_copy(ybuf.at[p], yk_hbm.at[spare_rows], sem_out.at[p])
            spare.start()
            spare.wait()
        _tokens_done(xbuf.at[0], tm, sem_in.at[0])

    @pl.when((i >= 1) & (i <= nact) & (f == 0))
    def _():
        _tokens_done(xbuf.at[par], tm, sem_in.at[par])

    @pl.when(i < nact)
    def _():
        @pl.when(f == 0)
        def _():
            xb16[...] = _rms(_load_token_tiles(xbuf.at[par], tm), g_ref[...]).astype(BF16)
            acc_ref[...] = jnp.zeros_like(acc_ref)

            for j in range(tm):
                fetch(i + 1, 1 - par, j)

        @pl.when(f == n_f - 1)
        def _():
            prev = jnp.where(i > 0, i - 1, pad_tile)

            for j in range(tm):
                send(prev, 1 - par, j)

        x = xb16[...]
        gate = jnp.dot(x, wg_ref[...], preferred_element_type=F32)
        up = jnp.dot(x, wu_ref[...], preferred_element_type=F32)
        act = (gate * jax.nn.sigmoid(gate) * up).astype(BF16)
        acc_ref[...] += jnp.dot(act, wd_ref[...], preferred_element_type=F32)

        @pl.when(f == n_f - 1)
        def _():
            @pl.when(i >= 1)
            def _():
                _tokens_done(ybuf.at[par], tm, sem_out.at[par])

            _store_token_tiles(ybuf.at[par], acc_ref[...])

    @pl.when((i == nact) & (f == 0))
    def _():
        def last(j, carry):
            send(i - 1, 1 - par, j)
            return carry
        lax.fori_loop(0, tm, last, 0)
        _tokens_done(ybuf.at[0], tm, sem_out.at[0])
        _tokens_done(ybuf.at[1], tm, sem_out.at[1])


def _moe_ffn(xt, g, plan, tile_expert, n_active, wg, wu, wd, tm):
    n_tok = xt.shape[0] // SUBLANES
    d = SUBLANES * LANES
    n_tiles = tile_expert.shape[0]
    pad_tile = n_tiles - 1 if (n_tiles - 1) % 2 else n_tiles
    assert plan.shape[0] >= (pad_tile + 1) * tm
    dff = wg.shape[-1]
    n_f = 2
    tf = dff // n_f
    fidx = lambda i, f, na: jnp.where(i < na[0], f, n_f - 1)
    grid_spec = pltpu.PrefetchScalarGridSpec(
        num_scalar_prefetch=3,
        grid=(n_tiles, n_f),
        in_specs=[
            pl.BlockSpec(memory_space=pl.ANY),
            pl.BlockSpec((1, d), lambda i, f, te, na, plan: (0, 0)),
            pl.BlockSpec((None, d, tf), lambda i, f, te, na, plan: (te[i], 0, fidx(i, f, na))),
            pl.BlockSpec((None, d, tf), lambda i, f, te, na, plan: (te[i], 0, fidx(i, f, na))),
            pl.BlockSpec((None, tf, d), lambda i, f, te, na, plan: (te[i], fidx(i, f, na), 0)),
        ],
        out_specs=pl.BlockSpec(memory_space=pl.ANY),
        scratch_shapes=[
            pltpu.VMEM((2, tm * SUBLANES, LANES), F32),
            pltpu.VMEM((tm, d), BF16),
            pltpu.VMEM((2, tm * SUBLANES, LANES), F32),
            pltpu.VMEM((tm, d), F32),
            pltpu.SemaphoreType.DMA((2,)),
            pltpu.SemaphoreType.DMA((2,)),
        ],
    )
    return pl.pallas_call(
        functools.partial(_moe_ffn_kernel, tm=tm, n_tok=n_tok, n_f=n_f, pad_tile=pad_tile,
                          src_bits=_plan_bits(n_tok, tm)),
        grid_spec=grid_spec,
        out_shape=jax.ShapeDtypeStruct(((2 * n_tok + 2 * tm) * SUBLANES, LANES), F32),
        compiler_params=_cparams(("arbitrary", "arbitrary")),
        name="moe_ffn",
    )(tile_expert, n_active, plan, xt, g.reshape(1, d), wg, wu, wd)


def _combine_kernel(route_ref, x_ref, y0_ref, y1_ref, fg_ref, o_ref, *, final_norm):
    r = route_ref[...]
    tm = r.shape[0]
    y = (_load_token_tiles(x_ref, tm) + r[:, 2:3] * _load_token_tiles(y0_ref, tm)
         + r[:, 3:4] * _load_token_tiles(y1_ref, tm))
    o_ref[...] = _rms(y, fg_ref[...]) if final_norm else y


def _combine(yk, route, xt, final_g, final_norm, tm=512):
    t = xt.shape[0] // SUBLANES
    d = SUBLANES * LANES
    n_blk = t // tm
    tiles = lambda f: pl.BlockSpec((tm * SUBLANES, LANES), f)
    return pl.pallas_call(
        functools.partial(_combine_kernel, final_norm=final_norm),
        grid=(n_blk,),
        in_specs=[
            pl.BlockSpec((tm, ROUTE_COLS), lambda i: (i, 0)),
            tiles(lambda i: (i, 0)),
            tiles(lambda i: (i, 0)),
            tiles(lambda i: (n_blk + i, 0)),
            pl.BlockSpec((1, d), lambda i: (0, 0)),
        ],
        out_specs=pl.BlockSpec((tm, d), lambda i: (i, 0)),
        out_shape=jax.ShapeDtypeStruct((t, d), F32),
        compiler_params=_cparams(("arbitrary",)),
        name="moe_combine",
    )(route, xt, yk, yk, final_g.reshape(1, d))


def _final_norm_kernel(x_ref, g_ref, o_ref):
    o_ref[...] = _rms(x_ref[...], g_ref[...])


def _final_norm(x2, g, tm=512):
    t, d = x2.shape
    return pl.pallas_call(
        _final_norm_kernel,
        grid=(t // tm,),
        in_specs=[pl.BlockSpec((tm, d), lambda i: (i, 0)), pl.BlockSpec((1, d), lambda i: (0, 0))],
        out_specs=pl.BlockSpec((tm, d), lambda i: (i, 0)),
        out_shape=jax.ShapeDtypeStruct((t, d), F32),
        compiler_params=_cparams(("arbitrary",)),
        name="final_norm",
    )(x2, g.reshape(1, d))


MOE_TM = 512


def kernel(x, norm1_g, w_in, na_rpb, sc_conv_w, cf_conv_w, cf_conv_b, cf_ln_g, cf_ln_b, ssm_a_re, ssm_a_im,
           ssm_log_dt, ssm_b_re, ssm_b_im, ssm_c_re, ssm_c_im, ssm_d, ssm_w_glu, ssm_b_glu, grp_norm_g, w_out,
           norm2_g, ffn_w_gate, ffn_w_up, ffn_w_down, moe_w_router, moe_w_gate, moe_w_up, moe_w_down,
           final_norm_g):
    bsz, seq, d = x.shape
    depth = w_in.shape[0]
    t = bsz * seq
    rows = seq // GRID_W
    x2 = x.reshape(t, d).astype(F32)
    final_done = False
    na_bias = jax.vmap(lambda rpb: _na_bias_table(rpb, rows))(na_rpb)
    s5_mats = jax.vmap(lambda *p: _s5_discretise(*p, bsz))(ssm_a_re, ssm_a_im, ssm_log_dt, ssm_b_re, ssm_b_im,
                                                           ssm_c_re, ssm_c_im)
    for i in range(depth):
        qkv, rest = _in_proj(x2, norm1_g[i], w_in[i].astype(BF16))
        y_na = _na_attention(qkv, na_bias, i, bsz, seq)
        y_sc, y_cf = _conv_mixers(rest, sc_conv_w[i], cf_conv_w[i], cf_conv_b[i], cf_ln_g[i], cf_ln_b[i],
                                  bsz, seq)
        y_ssm = _s5_mixer(rest, SSM_COL, *s5_mats, i, ssm_d[i], ssm_w_glu[i].astype(BF16), ssm_b_glu[i],
                          bsz, seq)
        ys = (y_na, y_sc, y_cf, y_ssm)
        gn, g2 = grp_norm_g[i].reshape(4, GROUP_W), norm2_g[i].reshape(1, d)
        j = i // 2
        if i % 2 == 0:
            ffn_w = (ffn_w_gate[j].astype(BF16), ffn_w_up[j].astype(BF16), ffn_w_down[j].astype(BF16))
            x2 = _mixer_out(ys, gn, w_out[i].astype(BF16), x2, g2, "ffn", ffn_w)
        else:
            x2, route = _mixer_out(ys, gn, w_out[i].astype(BF16), x2, g2, "route", (moe_w_router[j],))
            n_tiles = (2 * t) // MOE_TM + N_EXPERTS + 1
            slot, tile_expert, n_active = _moe_plan(route, MOE_TM, n_tiles)
            plan = _invert(slot, (n_tiles + 1) * MOE_TM, MOE_TM, t)
            yk = _moe_ffn(x2, norm2_g[i], plan, tile_expert, n_active, moe_w_gate[j].astype(BF16),
                          moe_w_up[j].astype(BF16), moe_w_down[j].astype(BF16), MOE_TM)
            final_done = i == depth - 1
            x2 = _combine(yk, route, x2, final_norm_g, final_done)
    if not final_done:
        x2 = _final_norm(x2, final_norm_g)
    return x2.reshape(bsz, seq, d)
```

```python
import functools
import math

import jax
import jax.numpy as jnp
import numpy as np
from jax import lax
from jax.experimental import pallas as pl
from jax.experimental.pallas import tpu as pltpu

F32 = jnp.float32
BF16 = jnp.bfloat16
EPS = 1e-6
NEG_INF = -1e30

GRID_W = 64
NA_HEAD_DIM = 64
NA_WIN_H = 8
NA_WIN_W = 16
GROUP_W = 256
SSM_GROUP_CH = 16
SSM_STATE = 64
N_EXPERTS = 8
VMEM_LIMIT = 56 * 1024 * 1024


def _cparams(sem, vmem=VMEM_LIMIT):
    return pltpu.CompilerParams(dimension_semantics=sem, vmem_limit_bytes=vmem)


def _rms(x, g):
    return x * lax.rsqrt(jnp.mean(x * x, axis=-1, keepdims=True) + EPS) * g


SUBLANES = 8
LANES = 128


def _load_token_tiles(ref, n):
    return jnp.concatenate([ref[pl.ds(s, n, stride=SUBLANES), :] for s in range(SUBLANES)], axis=1)


def _store_token_tiles(ref, val):
    for s in range(SUBLANES):
        ref[pl.ds(s, val.shape[0], stride=SUBLANES), :] = val[:, s * LANES:(s + 1) * LANES]


N_QKV_COLS = 3 * GROUP_W
SSM_COL = 5


def _in_proj_kernel(x_ref, g_ref, w_ref, qkv_ref, rest_ref):
    h = _rms(x_ref[...], g_ref[...]).astype(BF16)
    n_qkv = qkv_ref.shape[1] // GROUP_W
    n_rest = rest_ref.shape[1] // GROUP_W
    for j in range(n_qkv + n_rest):
        z = jnp.dot(h, w_ref[:, j * GROUP_W:(j + 1) * GROUP_W], preferred_element_type=F32)
        if j < n_qkv:
            qkv_ref[:, j * GROUP_W:(j + 1) * GROUP_W] = z.astype(BF16)
        else:
            k = j - n_qkv
            rest_ref[:, k * GROUP_W:(k + 1) * GROUP_W] = z


def _in_proj(x2, g, w, tm=1024):
    t, d = x2.shape
    n_cols = w.shape[1]
    n_rest = n_cols - N_QKV_COLS
    return pl.pallas_call(
        _in_proj_kernel,
        grid=(t // tm,),
        in_specs=[
            pl.BlockSpec((tm, d), lambda i: (i, 0)),
            pl.BlockSpec((1, d), lambda i: (0, 0)),
            pl.BlockSpec((d, n_cols), lambda i: (0, 0)),
        ],
        out_specs=[
            pl.BlockSpec((tm, N_QKV_COLS), lambda i: (i, 0)),
            pl.BlockSpec((tm, n_rest), lambda i: (i, 0)),
        ],
        out_shape=[
            jax.ShapeDtypeStruct((t, N_QKV_COLS), BF16),
            jax.ShapeDtypeStruct((t, n_rest), F32),
        ],
        compiler_params=_cparams(("arbitrary",)),
        name="in_proj",
    )(x2, g.reshape(1, d), w)


def _na_bias_table(rpb, rows):
    n_heads = rpb.shape[0]
    kh = NA_WIN_H
    c = np.arange(GRID_W)
    qcs = np.clip(c - NA_WIN_W // 2, 0, GRID_W - NA_WIN_W)
    kc = np.arange(GRID_W)
    in_win = (kc[None, :] >= qcs[:, None]) & (kc[None, :] < qcs[:, None] + NA_WIN_W)
    dc_idx = np.clip(kc[None, :] - c[:, None] + NA_WIN_W - 1, 0, 2 * NA_WIN_W - 2)
    onehot_dc = (dc_idx[:, :, None] == np.arange(2 * NA_WIN_W - 1)).astype(np.float32)
    rep_rows = np.array([0, 1, 2, 3, 4, rows - 3, rows - 2, rows - 1])
    row_start = np.clip(rep_rows - kh // 2, 0, rows - kh)
    dr_idx = (row_start - rep_rows + NA_WIN_H - 1)[:, None] + np.arange(kh)[None, :]
    onehot_dr = (dr_idx[:, :, None] == np.arange(2 * NA_WIN_H - 1)).astype(np.float32)
    tab = jnp.einsum("via,hab,ckb->vhcik", jnp.asarray(onehot_dr), rpb.astype(F32), jnp.asarray(onehot_dc),
                     precision=lax.Precision.HIGHEST)
    tab = jnp.where(jnp.asarray(in_win)[None, None, :, None, :], tab, NEG_INF)
    return tab.reshape(len(rep_rows), n_heads * GRID_W, kh * GRID_W)


def _na_kernel(q_ref, k_ref, v_ref, bias_ref, o_ref, *, rows, n_heads):
    kh = NA_WIN_H
    lane = lax.broadcasted_iota(jnp.int32, (GRID_W, n_heads * NA_HEAD_DIM), 1)
    masks = [(lane >= h * NA_HEAD_DIM) & (lane < (h + 1) * NA_HEAD_DIM) for h in range(n_heads)]
    scale = NA_HEAD_DIM ** -0.5

    def body(r, carry):
        q0 = pl.multiple_of(r * GRID_W, GRID_W)
        q = q_ref[pl.ds(q0, GRID_W), :].astype(F32) * scale
        qs = jnp.concatenate([jnp.where(m, q, 0.0) for m in masks], axis=0).astype(BF16)
        rs = jnp.clip(r - kh // 2, 0, rows - kh)
        k0 = pl.multiple_of(rs * GRID_W, GRID_W)
        ks = k_ref[pl.ds(k0, kh * GRID_W), :]
        vs = v_ref[pl.ds(k0, kh * GRID_W), :]
        s = lax.dot_general(qs, ks, (((1,), (1,)), ((), ())), preferred_element_type=F32)
        var = jnp.where(r < kh // 2, r, jnp.where(r > rows - kh // 2, r - (rows - 8), kh // 2))
        s = s + bias_ref[var]
        m = jnp.max(s, axis=-1, keepdims=True)
        p = jnp.exp(s - m)
        den = jnp.sum(p, axis=-1, keepdims=True)
        o = jnp.dot(p.astype(BF16), vs, preferred_element_type=F32) / den
        out = jnp.where(masks[0], o[0:GRID_W], 0.0)
        for h in range(1, n_heads):
            out = out + jnp.where(masks[h], o[h * GRID_W:(h + 1) * GRID_W], 0.0)
        o_ref[pl.ds(q0, GRID_W), :] = out
        return carry

    lax.fori_loop(0, rows, body, 0, unroll=16)


def _na_attention(qkv, bias, layer, bsz, seq):
    rows = seq // GRID_W
    n_heads = GROUP_W // NA_HEAD_DIM
    t = qkv.shape[0]
    blk = lambda j: pl.BlockSpec((seq, GROUP_W), lambda b, j=j: (b, j))
    return pl.pallas_call(
        functools.partial(_na_kernel, rows=rows, n_heads=n_heads),
        grid=(bsz,),
        in_specs=[blk(0), blk(1), blk(2),
                  pl.BlockSpec((None,) + bias.shape[1:], lambda b: (layer, 0, 0, 0))],
        out_specs=pl.BlockSpec((seq, GROUP_W), lambda b: (b, 0)),
        out_shape=jax.ShapeDtypeStruct((t, GROUP_W), F32),
        compiler_params=_cparams(("arbitrary",)),
        name="na_attention",
    )(qkv, qkv, qkv, bias)


CONV_PAD = 16
CONV_CHUNK = 512


def _conv_kernel(b_ref, c_ref, x_ref, a_ref, g_ref, scw_ref, cfw_ref, cfb_ref, lng_ref, lnb_ref,
                 ysc_ref, ycf_ref, psc, pcf):
    seq = b_ref.shape[0]
    n_chunks = seq // CONV_CHUNK
    zeros = jnp.zeros((CONV_PAD, GROUP_W), F32)
    psc[0:CONV_PAD, :] = zeros
    pcf[0:CONV_PAD, :] = zeros
    psc[CONV_PAD + seq:2 * CONV_PAD + seq, :] = zeros
    pcf[CONV_PAD + seq:2 * CONV_PAD + seq, :] = zeros

    def fill(i, carry):
        r0 = pl.multiple_of(i * CONV_CHUNK, CONV_CHUNK)
        rows = pl.ds(r0, CONV_CHUNK)
        dst = pl.ds(r0 + CONV_PAD, CONV_CHUNK)
        psc[dst, :] = c_ref[rows, :] * x_ref[rows, :]
        pcf[dst, :] = a_ref[rows, :] * jax.nn.sigmoid(g_ref[rows, :])
        return carry

    lax.fori_loop(0, n_chunks, fill, 0)

    def taps(win, w_ref):
        n_taps = w_ref.shape[0]
        offs = [CONV_PAD + k - n_taps // 2 for k in range(n_taps)]
        n_win = win.shape[0]
        acc = None
        for s in range(SUBLANES):
            ks = [k for k in range(n_taps) if offs[k] % SUBLANES == s]
            if not ks:
                continue
            shifted = pltpu.roll(win, n_win - s, axis=0) if s else win
            for k in ks:
                q = offs[k] - s
                term = shifted[q:q + CONV_CHUNK, :] * w_ref[k:k + 1, :]
                acc = term if acc is None else acc + term
        return acc

    def conv(i, carry):
        r0 = pl.multiple_of(i * CONV_CHUNK, CONV_CHUNK)
        rows = pl.ds(r0, CONV_CHUNK)
        window = pl.ds(r0, CONV_CHUNK + 2 * CONV_PAD)
        ysc_ref[rows, :] = b_ref[rows, :] * taps(psc[window, :], scw_ref)
        cf = taps(pcf[window, :], cfw_ref) + cfb_ref[...]
        mu = jnp.mean(cf, axis=-1, keepdims=True)
        xc = cf - mu
        var = jnp.mean(xc * xc, axis=-1, keepdims=True)
        ln = xc * lax.rsqrt(var + EPS) * lng_ref[...] + lnb_ref[...]
        ycf_ref[rows, :] = ln * jax.nn.sigmoid(ln)
        return carry

    lax.fori_loop(0, n_chunks, conv, 0)


def _conv_mixers(rest, sc_w, cf_w, cf_b, ln_g, ln_b, bsz, seq):
    t = rest.shape[0]
    blk = lambda j: pl.BlockSpec((seq, GROUP_W), lambda b, j=j: (b, j))
    full = lambda a: pl.BlockSpec(a.shape, lambda b: (0,) * a.ndim)
    cf_b, ln_g, ln_b = (a.reshape(1, GROUP_W) for a in (cf_b, ln_g, ln_b))
    out_spec = pl.BlockSpec((seq, GROUP_W), lambda b: (b, 0))
    return pl.pallas_call(
        _conv_kernel,
        grid=(bsz,),
        in_specs=[blk(0), blk(1), blk(2), blk(3), blk(4),
                  full(sc_w), full(cf_w), full(cf_b), full(ln_g), full(ln_b)],
        out_specs=[out_spec, out_spec],
        out_shape=[jax.ShapeDtypeStruct((t, GROUP_W), F32)] * 2,
        scratch_shapes=[pltpu.VMEM((seq + 2 * CONV_PAD, GROUP_W), F32)] * 2,
        compiler_params=_cparams(("arbitrary",)),
        name="conv_mixers",
    )(rest, rest, rest, rest, rest, sc_w, cf_w, cf_b, ln_g, ln_b)


S5_TL = 256
S5_LANE_SPLIT = 1
S5_ROW_BLOCK = 256


def _s5_discretise(a_re, a_im, log_dt, b_re, b_im, c_re, c_im, bsz):
    f32 = F32
    a_re, a_im, log_dt = a_re.astype(f32), a_im.astype(f32), log_dt.astype(f32)
    n_dir, n_grp, n_state = a_re.shape
    n_ch = b_re.shape[-1]
    dt = jnp.exp(log_dt)[..., None]
    mag = jnp.exp(a_re * dt)
    abr = mag * jnp.cos(a_im * dt)
    abi = mag * jnp.sin(a_im * dt)
    den = a_re * a_re + a_im * a_im
    fr = ((abr - 1.0) * a_re + abi * a_im) / den
    fi = (abi * a_re - (abr - 1.0) * a_im) / den
    bbr = fr[..., None] * b_re - fi[..., None] * b_im
    bbi = fr[..., None] * b_im + fi[..., None] * b_re
    eye = jnp.eye(n_grp, dtype=f32)
    blk_b = lambda m: jnp.einsum("dgps,gh->dgshp", m, eye).reshape(n_dir, n_grp * n_ch, n_grp * n_state)
    blk_c = lambda m: jnp.einsum("dgsp,gh->dgphs", m, eye).reshape(n_dir, n_grp * n_state, n_grp * n_ch)
    bmat = jnp.concatenate([blk_b(bbr), blk_b(bbi)], axis=-1)
    cmat = jnp.concatenate([blk_c(c_re.astype(f32)), blk_c(-c_im.astype(f32))], axis=1)
    amat = jnp.stack([abr.reshape(n_dir, -1), abi.reshape(n_dir, -1)], axis=1)
    amat = jnp.broadcast_to(amat[:, :, None, :], (n_dir, 2, bsz, n_grp * n_state))
    return bmat.astype(BF16), cmat.astype(BF16), amat


def _s5_kernel(*refs, bsz, backward):
    if backward:
        (u_ref, b_ref, c_ref, a_ref, yf_ref, d_ref, w_ref, bg_ref, o_ref, xs, st, tb) = refs
    else:
        (u_ref, b_ref, c_ref, a_ref, o_ref, xs, st, tb) = refs
    n_state = a_ref.shape[-1]

    @pl.when(pl.program_id(0) == 0)
    def _():
        st[...] = jnp.zeros_like(st)

    n_half = GROUP_W // LANES
    for b in range(bsz):
        for h in range(n_half):
            tb[h, pl.ds(b, S5_TL, stride=bsz), :] = u_ref[b, :, h * LANES:(h + 1) * LANES]
    for r0 in range(0, S5_TL * bsz, S5_ROW_BLOCK):
        rows = slice(r0, r0 + S5_ROW_BLOCK)
        u_tb = jnp.concatenate([tb[h, rows, :] for h in range(n_half)], axis=1)
        xs[rows, :] = jnp.dot(u_tb.astype(BF16), b_ref[...], preferred_element_type=F32)

    cw = n_state // S5_LANE_SPLIT
    for c in range(S5_LANE_SPLIT):
        re = slice(c * cw, (c + 1) * cw)
        im = slice(n_state + c * cw, n_state + (c + 1) * cw)
        a_r = a_ref[0, :, re]
        a_i = a_ref[1, :, re]

        def step(t, carry, re=re, im=im, a_r=a_r, a_i=a_i):
            s_r, s_i = carry
            tt = S5_TL - 1 - t if backward else t
            rows = pl.ds(pl.multiple_of(tt * bsz, bsz), bsz)
            n_r = a_r * s_r - a_i * s_i + xs[rows, re]
            n_i = a_r * s_i + a_i * s_r + xs[rows, im]
            xs[rows, re] = n_r
            xs[rows, im] = n_i
            return n_r, n_i

        s_r, s_i = lax.fori_loop(0, S5_TL, step, (st[:, re], st[:, im]), unroll=4)
        st[:, re] = s_r
        st[:, im] = s_i

    n_rows = S5_TL * bsz
    y_tb = jnp.concatenate(
        [jnp.dot(xs[r0:r0 + S5_ROW_BLOCK, :].astype(BF16), c_ref[...], preferred_element_type=F32)
         for r0 in range(0, n_rows, S5_ROW_BLOCK)], axis=0)
    if not backward:
        o_ref[...] = y_tb
        return
    y_sum = y_tb + yf_ref[...]
    for h in range(n_half):
        tb[h] = y_sum[:, h * LANES:(h + 1) * LANES]
    y_bt = jnp.concatenate(
        [jnp.concatenate([tb[h, pl.ds(b, S5_TL, stride=bsz), :] for h in range(n_half)], axis=1)
         for b in range(bsz)], axis=0)
    y = jax.nn.gelu(d_ref[...] * u_ref[...].reshape(S5_TL * bsz, GROUP_W) + y_bt)
    gate = jnp.dot(y.astype(BF16), w_ref[...], preferred_element_type=F32) + bg_ref[...]
    o_ref[...] = (y * jax.nn.sigmoid(gate)).reshape(bsz, S5_TL, GROUP_W)


def _s5_mixer(rest, ssm_col, bmat, cmat, amat, layer, d_skip, w_glu, b_glu, bsz, seq):
    n_chunks = seq // S5_TL
    n_rows = S5_TL * bsz
    n_st2 = bmat.shape[-1]
    rest3 = rest.reshape(bsz, seq, rest.shape[1])
    full = lambda a: pl.BlockSpec(a.shape, lambda i: (0,) * a.ndim)
    scratch = [pltpu.VMEM((n_rows, n_st2), F32),
               pltpu.VMEM((bsz, n_st2), F32),
               pltpu.VMEM((GROUP_W // LANES, n_rows, LANES), F32)]

    def direction(d, chunk_of):
        return [
            pl.BlockSpec((bsz, S5_TL, GROUP_W), lambda i: (0, chunk_of(i), ssm_col)),
            pl.BlockSpec((None, None, GROUP_W, n_st2), lambda i: (layer, d, 0, 0)),
            pl.BlockSpec((None, None, n_st2, GROUP_W), lambda i: (layer, d, 0, 0)),
            pl.BlockSpec((None, None, 2, bsz, n_st2 // 2), lambda i: (layer, d, 0, 0, 0)),
        ]

    y_fwd = pl.pallas_call(
        functools.partial(_s5_kernel, bsz=bsz, backward=False),
        grid=(n_chunks,),
        in_specs=direction(0, lambda i: i),
        out_specs=pl.BlockSpec((n_rows, GROUP_W), lambda i: (i, 0)),
        out_shape=jax.ShapeDtypeStruct((seq * bsz, GROUP_W), F32),
        scratch_shapes=scratch,
        compiler_params=_cparams(("arbitrary",)),
        name="s5_forward",
    )(rest3, bmat, cmat, amat)

    rev = lambda i: n_chunks - 1 - i
    d_skip, b_glu = d_skip.reshape(1, GROUP_W), b_glu.reshape(1, GROUP_W)
    out = pl.pallas_call(
        functools.partial(_s5_kernel, bsz=bsz, backward=True),
        grid=(n_chunks,),
        in_specs=direction(1, rev) + [
            pl.BlockSpec((n_rows, GROUP_W), lambda i: (rev(i), 0)),
            full(d_skip), full(w_glu), full(b_glu),
        ],
        out_specs=pl.BlockSpec((bsz, S5_TL, GROUP_W), lambda i: (0, rev(i), 0)),
        out_shape=jax.ShapeDtypeStruct((bsz, seq, GROUP_W), F32),
        scratch_shapes=scratch,
        compiler_params=_cparams(("arbitrary",)),
        name="s5_backward",
    )(rest3, bmat, cmat, amat, y_fwd, d_skip, w_glu, b_glu)
    return out.reshape(bsz * seq, GROUP_W)


FFN_CHUNK = 256
ROUTE_COLS = 8
ROUTER_LANES = 128
ROUTER_ROW_BLOCK = 128


def _route_records(h, wr_ref):
    h_hi = h.astype(BF16)
    h_lo = (h - h_hi.astype(F32)).astype(BF16)
    pieces = jnp.concatenate([h_hi, h_lo, h_hi], axis=1)
    logits = jnp.concatenate(
        [jnp.dot(pieces[r0:r0 + ROUTER_ROW_BLOCK], wr_ref[...], preferred_element_type=F32)
         for r0 in range(0, h.shape[0], ROUTER_ROW_BLOCK)], axis=0)
    lane = lax.broadcasted_iota(jnp.int32, logits.shape, 1)
    minus_inf = -jnp.inf
    l1 = jnp.where(lane < N_EXPERTS, logits, minus_inf)
    m1 = jnp.max(l1, axis=-1, keepdims=True)
    i1 = jnp.min(jnp.where(l1 == m1, lane, ROUTER_LANES), axis=-1, keepdims=True)
    l2 = jnp.where(lane == i1, minus_inf, l1)
    m2 = jnp.max(l2, axis=-1, keepdims=True)
    i2 = jnp.min(jnp.where(l2 == m2, lane, ROUTER_LANES), axis=-1, keepdims=True)
    e2 = jnp.exp(m2 - m1)
    den = 1.0 + e2
    rec = jnp.where(lane == 0, i1.astype(F32),
                    jnp.where(lane == 1, i2.astype(F32),
                              jnp.where(lane == 2, 1.0 / den,
                                        jnp.where(lane == 3, e2 / den, 0.0))))
    return rec[:, :ROUTE_COLS]


def _mixer_out_kernel(*refs, tail):
    y_refs, (gn_ref, w_ref, x_ref, g2_ref) = refs[:4], refs[4:8]
    acc = x_ref[...]
    for j, y_ref in enumerate(y_refs):
        yn = _rms(y_ref[...], gn_ref[j:j + 1, :]).astype(BF16)
        acc = acc + jnp.dot(yn, w_ref[j * GROUP_W:(j + 1) * GROUP_W, :], preferred_element_type=F32)
    h = _rms(acc, g2_ref[...])
    if tail == "route":
        wr_ref, o_ref, route_ref = refs[8:]
        _store_token_tiles(o_ref, acc)
        route_ref[...] = _route_records(h, wr_ref)
        return
    wg_hbm, wu_hbm, wd_hbm, o_ref, wg_ref, wu_ref, wd_ref, sem = refs[8:]

    @pl.when(pl.program_id(0) == 0)
    def _():
        copies = [pltpu.make_async_copy(wd_hbm, wd_ref, sem.at[2])]
        for k, (src, dst) in enumerate(((wg_hbm, wg_ref), (wu_hbm, wu_ref))):
            copies += [pltpu.make_async_copy(src.at[:, pl.ds(c * FFN_CHUNK, FFN_CHUNK)], dst.at[c], sem.at[k])
                       for c in range(dst.shape[0])]
        for c in copies:
            c.start()
        for c in copies:
            c.wait()

    h = h.astype(BF16)
    acts = []
    for c in range(wg_ref.shape[0]):
        gate = jnp.dot(h, wg_ref[c], preferred_element_type=F32)
        up = jnp.dot(h, wu_ref[c], preferred_element_type=F32)
        acts.append((gate * jax.nn.sigmoid(gate) * up).astype(BF16))
    o_ref[...] = acc + jnp.dot(jnp.concatenate(acts, axis=1), wd_ref[...], preferred_element_type=F32)


def _mixer_out(ys, gn, w_out, x2, g2, tail, tail_weights, tm=512):
    t, d = x2.shape
    row = lambda n: pl.BlockSpec((tm, n), lambda i: (i, 0))
    whole = lambda a: pl.BlockSpec(a.shape, lambda i: (0, 0))
    if tail == "route":
        wr = jnp.zeros((d, ROUTER_LANES), F32).at[:, :N_EXPERTS].set(tail_weights[0].astype(F32))
        wr_hi = wr.astype(BF16)
        wr_lo = (wr - wr_hi.astype(F32)).astype(BF16)
        tail_weights = (jnp.concatenate([wr_hi, wr_hi, wr_lo], axis=0),)
        tail_specs = [whole(w) for w in tail_weights]
        out_specs = [pl.BlockSpec((tm * SUBLANES, LANES), lambda i: (i, 0)), row(ROUTE_COLS)]
        out_shape = [jax.ShapeDtypeStruct((t * SUBLANES, LANES), F32), jax.ShapeDtypeStruct((t, ROUTE_COLS), F32)]
        scratch = []
    else:
        w_gate, w_up, w_down = tail_weights
        chunk_major = pltpu.VMEM((w_gate.shape[1] // FFN_CHUNK, d, FFN_CHUNK), w_gate.dtype)
        tail_specs = [pl.BlockSpec(memory_space=pl.ANY)] * 3
        out_specs, out_shape = row(d), jax.ShapeDtypeStruct((t, d), F32)
        scratch = [chunk_major, chunk_major, pltpu.VMEM(w_down.shape, w_down.dtype), pltpu.SemaphoreType.DMA((3,))]
    return pl.pallas_call(
        functools.partial(_mixer_out_kernel, tail=tail),
        grid=(t // tm,),
        in_specs=[row(GROUP_W)] * 4 + [whole(gn), whole(w_out), row(d), whole(g2)] + tail_specs,
        out_specs=out_specs,
        out_shape=out_shape,
        scratch_shapes=scratch,
        compiler_params=_cparams(("arbitrary",)),
        name="mixer_out_" + tail,
    )(*ys, gn, w_out, x2, g2, *tail_weights)


def _moe_plan(route, tm, n_tiles):
    ids = route[:, :2]
    experts = jnp.where((ids >= 0) & (ids <= N_EXPERTS - 1), ids, 0).astype(jnp.int32).reshape(-1)
    onehot = (experts[:, None] == jnp.arange(N_EXPERTS, dtype=jnp.int32)[None, :]).astype(jnp.int32)
    csum = jnp.cumsum(onehot, axis=0)
    rank = jnp.sum((csum - onehot) * onehot, axis=1)
    count = csum[-1]
    padded = ((count + tm - 1) // tm) * tm
    end = jnp.cumsum(padded)
    start = end - padded
    slot = jnp.sum(onehot * start[None, :], axis=1) + rank
    n_active = end[-1] // tm
    tile_start = jnp.arange(n_tiles, dtype=jnp.int32) * tm
    tile_expert = jnp.sum((tile_start[:, None] >= end[None, :]).astype(jnp.int32), axis=1)
    tile_expert = jnp.minimum(tile_expert, N_EXPERTS - 1)
    last = tile_expert[jnp.maximum(n_active - 1, 0)]
    tile_expert = jnp.where(jnp.arange(n_tiles) < n_active, tile_expert, last)
    return slot.astype(jnp.int32), tile_expert.astype(jnp.int32), n_active.astype(jnp.int32).reshape(1)


def _token_copy(src, src_tok, dst, dst_tok, sem):
    rows = lambda tok: pl.ds(pl.multiple_of(tok * SUBLANES, SUBLANES), SUBLANES)
    return pltpu.make_async_copy(src.at[rows(src_tok)], dst.at[rows(dst_tok)], sem)


def _invert_kernel(slot_ref, word_ref, pad_hbm, plan_ref, sem):
    @pl.when(pl.program_id(0) == 0)
    def _():
        init = pltpu.make_async_copy(pad_hbm, plan_ref, sem)
        init.start()
        init.wait()

    def place(a, carry):
        plan_ref[slot_ref[0, 0, a]] = word_ref[0, 0, a]
        return carry

    lax.fori_loop(0, slot_ref.shape[-1], place, 0, unroll=8)


def _plan_bits(n_tok, tm):
    src_bits = (n_tok - 1).bit_length()
    assert (2 * n_tok + 2 * tm) << src_bits < 2 ** 31 and tm & (tm - 1) == 0
    return src_bits


def _invert(slot, n_slots, tm, n_tok, chunk=4096):
    src_bits = _plan_bits(n_tok, tm)
    a = jnp.arange(slot.shape[0], dtype=jnp.int32)
    tok = a >> 1
    word = tok + (((a & 1) * n_tok + tok) << src_bits)
    s = jnp.arange(n_slots, dtype=jnp.int32)
    pad_plan = (2 * n_tok + (s & (2 * tm - 1))) << src_bits
    n_chunks = slot.shape[0] // chunk
    chunked = pl.BlockSpec((1, 1, chunk), lambda c: (c, 0, 0), memory_space=pltpu.SMEM)
    return pl.pallas_call(
        _invert_kernel,
        grid=(n_chunks,),
        in_specs=[chunked, chunked, pl.BlockSpec(memory_space=pl.ANY)],
        out_specs=pl.BlockSpec((n_slots,), lambda c: (0,), memory_space=pltpu.SMEM),
        out_shape=jax.ShapeDtypeStruct((n_slots,), jnp.int32),
        scratch_shapes=[pltpu.SemaphoreType.DMA(())],
        compiler_params=pltpu.CompilerParams(dimension_semantics=("arbitrary",)),
        name="moe_invert",
    )(slot.reshape(n_chunks, 1, chunk), word.reshape(n_chunks, 1, chunk), pad_plan)


def _tokens_done(ref, n_tok, sem):
    rows = pl.ds(0, n_tok * SUBLANES)
    pltpu.make_async_copy(ref.at[rows], ref.at[rows], sem).wait()


def _moe_ffn_kernel(te_ref, nact_ref, plan_ref, x_hbm, g_ref, wg_ref, wu_ref, wd_ref, yk_hbm,
                    xbuf, xb16, ybuf, acc_ref, sem_in, sem_out, *, tm, n_tok, n_f, pad_tile, src_bits):
    del te_ref
    i = pl.program_id(0)
    f = pl.program_id(1)
    par = i % 2
    nact = nact_ref[0]

    def fetch(tile, p, j):
        src = plan_ref[tile * tm + j] & ((1 << src_bits) - 1)
        _token_copy(x_hbm, src, xbuf.at[p], j, sem_in.at[p]).start(priority=1)

    def send(tile, p, j):
        dst = plan_ref[tile * tm + j] >> src_bits
        _token_copy(ybuf.at[p], j, yk_hbm, dst, sem_out.at[p]).start()

    @pl.when((i == 0) & (f == 0))
    def _():
        def first(j, carry):
            fetch(0, 0, j)
            return carry
        lax.fori_loop(0, tm, first, 0)
        ybuf[...] = jnp.zeros_like(ybuf)
        for p in range(2):
            spare_rows = pl.ds((2 * n_tok + p * tm) * SUBLANES, tm * SUBLANES)
            spare = pltpu.make_async_copy(ybuf.at[p], yk_hbm.at[spare_rows], sem_out.at[p])
            spare.start()
            spare.wait()
        _tokens_done(xbuf.at[0], tm, sem_in.at[0])

    @pl.when((i >= 1) & (i <= nact) & (f == 0))
    def _():
        _tokens_done(xbuf.at[par], tm, sem_in.at[par])

    @pl.when(i < nact)
    def _():
        @pl.when(f == 0)
        def _():
            xb16[...] = _rms(_load_token_tiles(xbuf.at[par], tm), g_ref[...]).astype(BF16)
            acc_ref[...] = jnp.zeros_like(acc_ref)

            for j in range(tm):
                fetch(i + 1, 1 - par, j)

        @pl.when(f == n_f - 1)
        def _():
            prev = jnp.where(i > 0, i - 1, pad_tile)

            for j in range(tm):
                send(prev, 1 - par, j)

        x = xb16[...]
        gate = jnp.dot(x, wg_ref[...], preferred_element_type=F32)
        up = jnp.dot(x, wu_ref[...], preferred_element_type=F32)
        act = (gate * jax.nn.sigmoid(gate) * up).astype(BF16)
        acc_ref[...] += jnp.dot(act, wd_ref[...], preferred_element_type=F32)

        @pl.when(f == n_f - 1)
        def _():
            @pl.when(i >= 1)
            def _():
                _tokens_done(ybuf.at[par], tm, sem_out.at[par])

            _store_token_tiles(ybuf.at[par], acc_ref[...])

    @pl.when((i == nact) & (f == 0))
    def _():
        def last(j, carry):
            send(i - 1, 1 - par, j)
            return carry
        lax.fori_loop(0, tm, last, 0)
        _tokens_done(ybuf.at[0], tm, sem_out.at[0])
        _tokens_done(ybuf.at[1], tm, sem_out.at[1])


def _moe_ffn(xt, g, plan, tile_expert, n_active, wg, wu, wd, tm):
    n_tok = xt.shape[0] // SUBLANES
    d = SUBLANES * LANES
    n_tiles = tile_expert.shape[0]
    pad_tile = n_tiles - 1 if (n_tiles - 1) % 2 else n_tiles
    assert plan.shape[0] >= (pad_tile + 1) * tm
    dff = wg.shape[-1]
    n_f = 2
    tf = dff // n_f
    fidx = lambda i, f, na: jnp.where(i < na[0], f, n_f - 1)
    grid_spec = pltpu.PrefetchScalarGridSpec(
        num_scalar_prefetch=3,
        grid=(n_tiles, n_f),
        in_specs=[
            pl.BlockSpec(memory_space=pl.ANY),
            pl.BlockSpec((1, d), lambda i, f, te, na, plan: (0, 0)),
            pl.BlockSpec((None, d, tf), lambda i, f, te, na, plan: (te[i], 0, fidx(i, f, na))),
            pl.BlockSpec((None, d, tf), lambda i, f, te, na, plan: (te[i], 0, fidx(i, f, na))),
            pl.BlockSpec((None, tf, d), lambda i, f, te, na, plan: (te[i], fidx(i, f, na), 0)),
        ],
        out_specs=pl.BlockSpec(memory_space=pl.ANY),
        scratch_shapes=[
            pltpu.VMEM((2, tm * SUBLANES, LANES), F32),
            pltpu.VMEM((tm, d), BF16),
            pltpu.VMEM((2, tm * SUBLANES, LANES), F32),
            pltpu.VMEM((tm, d), F32),
            pltpu.SemaphoreType.DMA((2,)),
            pltpu.SemaphoreType.DMA((2,)),
        ],
    )
    return pl.pallas_call(
        functools.partial(_moe_ffn_kernel, tm=tm, n_tok=n_tok, n_f=n_f, pad_tile=pad_tile,
                          src_bits=_plan_bits(n_tok, tm)),
        grid_spec=grid_spec,
        out_shape=jax.ShapeDtypeStruct(((2 * n_tok + 2 * tm) * SUBLANES, LANES), F32),
        compiler_params=_cparams(("arbitrary", "arbitrary")),
        name="moe_ffn",
    )(tile_expert, n_active, plan, xt, g.reshape(1, d), wg, wu, wd)


def _combine_kernel(route_ref, x_ref, y0_ref, y1_ref, fg_ref, o_ref, *, final_norm):
    r = route_ref[...]
    tm = r.shape[0]
    y = (_load_token_tiles(x_ref, tm) + r[:, 2:3] * _load_token_tiles(y0_ref, tm)
         + r[:, 3:4] * _load_token_tiles(y1_ref, tm))
    o_ref[...] = _rms(y, fg_ref[...]) if final_norm else y


def _combine(yk, route, xt, final_g, final_norm, tm=512):
    t = xt.shape[0] // SUBLANES
    d = SUBLANES * LANES
    n_blk = t // tm
    tiles = lambda f: pl.BlockSpec((tm * SUBLANES, LANES), f)
    return pl.pallas_call(
        functools.partial(_combine_kernel, final_norm=final_norm),
        grid=(n_blk,),
        in_specs=[
            pl.BlockSpec((tm, ROUTE_COLS), lambda i: (i, 0)),
            tiles(lambda i: (i, 0)),
            tiles(lambda i: (i, 0)),
            tiles(lambda i: (n_blk + i, 0)),
            pl.BlockSpec((1, d), lambda i: (0, 0)),
        ],
        out_specs=pl.BlockSpec((tm, d), lambda i: (i, 0)),
        out_shape=jax.ShapeDtypeStruct((t, d), F32),
        compiler_params=_cparams(("arbitrary",)),
        name="moe_combine",
    )(route, xt, yk, yk, final_g.reshape(1, d))


def _final_norm_kernel(x_ref, g_ref, o_ref):
    o_ref[...] = _rms(x_ref[...], g_ref[...])


def _final_norm(x2, g, tm=512):
    t, d = x2.shape
    return pl.pallas_call(
        _final_norm_kernel,
        grid=(t // tm,),
        in_specs=[pl.BlockSpec((tm, d), lambda i: (i, 0)), pl.BlockSpec((1, d), lambda i: (0, 0))],
        out_specs=pl.BlockSpec((tm, d), lambda i: (i, 0)),
        out_shape=jax.ShapeDtypeStruct((t, d), F32),
        compiler_params=_cparams(("arbitrary",)),
        name="final_norm",
    )(x2, g.reshape(1, d))


MOE_TM = 512


def kernel(x, norm1_g, w_in, na_rpb, sc_conv_w, cf_conv_w, cf_conv_b, cf_ln_g, cf_ln_b, ssm_a_re, ssm_a_im,
           ssm_log_dt, ssm_b_re, ssm_b_im, ssm_c_re, ssm_c_im, ssm_d, ssm_w_glu, ssm_b_glu, grp_norm_g, w_out,
           norm2_g, ffn_w_gate, ffn_w_up, ffn_w_down, moe_w_router, moe_w_gate, moe_w_up, moe_w_down,
           final_norm_g):
    bsz, seq, d = x.shape
    depth = w_in.shape[0]
    t = bsz * seq
    rows = seq // GRID_W
    x2 = x.reshape(t, d).astype(F32)
    final_done = False
    na_bias = jax.vmap(lambda rpb: _na_bias_table(rpb, rows))(na_rpb)
    s5_mats = jax.vmap(lambda *p: _s5_discretise(*p, bsz))(ssm_a_re, ssm_a_im, ssm_log_dt, ssm_b_re, ssm_b_im,
                                                           ssm_c_re, ssm_c_im)
    for i in range(depth):
        qkv, rest = _in_proj(x2, norm1_g[i], w_in[i].astype(BF16))
        y_na = _na_attention(qkv, na_bias, i, bsz, seq)
        y_sc, y_cf = _conv_mixers(rest, sc_conv_w[i], cf_conv_w[i], cf_conv_b[i], cf_ln_g[i], cf_ln_b[i],
                                  bsz, seq)
        y_ssm = _s5_mixer(rest, SSM_COL, *s5_mats, i, ssm_d[i], ssm_w_glu[i].astype(BF16), ssm_b_glu[i],
                          bsz, seq)
        ys = (y_na, y_sc, y_cf, y_ssm)
        gn, g2 = grp_norm_g[i].reshape(4, GROUP_W), norm2_g[i].reshape(1, d)
        j = i // 2
        if i % 2 == 0:
            ffn_w = (ffn_w_gate[j].astype(BF16), ffn_w_up[j].astype(BF16), ffn_w_down[j].astype(BF16))
            x2 = _mixer_out(ys, gn, w_out[i].astype(BF16), x2, g2, "ffn", ffn_w)
        else:
            x2, route = _mixer_out(ys, gn, w_out[i].astype(BF16), x2, g2, "route", (moe_w_router[j],))
            n_tiles = (2 * t) // MOE_TM + N_EXPERTS + 1
            slot, tile_expert, n_active = _moe_plan(route, MOE_TM, n_tiles)
            plan = _invert(slot, (n_tiles + 1) * MOE_TM, MOE_TM, t)
            yk = _moe_ffn(x2, norm2_g[i], plan, tile_expert, n_active, moe_w_gate[j].astype(BF16),
                          moe_w_up[j].astype(BF16), moe_w_down[j].astype(BF16), MOE_TM)
            final_done = i == depth - 1
            x2 = _combine(yk, route, x2, final_norm_g, final_done)
    if not final_done:
        x2 = _final_norm(x2, final_norm_g)
    return x2.reshape(bsz, seq, d)
```
